```python
import math
import jax, jax.numpy as jnp
from jax import lax
import numpy as np


D_MODEL = 1024
BATCH = 4
SEQ = 8192
DEPTH = 2

GRID_W = 64
CTX_LEN = 256
CHUNK = 128
Q_BLOCK = 128
NORM_EPS = 1e-6
ROPE_THETA = 10000.0

SSD_HEADS = 16
SSD_HEAD_DIM = 64
SSD_INNER = SSD_HEADS * SSD_HEAD_DIM
SSD_GROUPS = 2
SSD_STATE = 128
SSD_CONV = 3
SSD_CONV_CH = SSD_INNER + 2 * SSD_GROUPS * SSD_STATE
RET_HEADS = 4
RET_QK_DIM = 128
RET_V_DIM = 256
RET_QK = RET_HEADS * RET_QK_DIM
RET_INNER = RET_HEADS * RET_V_DIM
EVEN_IN = SSD_INNER + SSD_CONV_CH + 2 * SSD_HEADS + 2 * RET_QK + 2 * RET_INNER
EVEN_MIX = SSD_INNER + RET_INNER
DIFF_HEADS = 8
DIFF_HEAD_DIM = 64
DIFF_V_DIM = 2 * DIFF_HEAD_DIM
DIFF_INNER = DIFF_HEADS * DIFF_V_DIM
ODD_IN = 3 * DIFF_INNER
FFN_DENSE = 2816
N_EXPERTS = 8
TOP_K = 2
FFN_EXPERT = 3584

kernel_name = "hybrid_ssd_retention_diffattn_moe_dit"


def rms_norm(x, gain=None):
    xf = x.astype(jnp.float32)
    y = xf * lax.rsqrt(jnp.mean(xf * xf, axis=-1, keepdims=True) + NORM_EPS)
    if gain is not None:
        y = y * gain.astype(jnp.float32)
    return y.astype(x.dtype)


def adaln(cond, w, b):
    m = jax.nn.silu(cond) @ w + b
    return jnp.split(m[..., None, :], 6, axis=-1)


def modulate(x, gain, shift, scale):
    return rms_norm(x, gain) * (1 + scale) + shift


def rope_angles(pos, dim, theta):
    inv = theta ** (-jnp.arange(dim // 2, dtype=jnp.float32) / (dim // 2))
    return pos.astype(jnp.float32)[:, None] * inv[None, :]


def apply_rope(x, ang):
    half = x.shape[-1] // 2
    shape = (1, ang.shape[0]) + (1,) * (x.ndim - 3) + (half,)
    cos = jnp.cos(ang).reshape(shape).astype(x.dtype)
    sin = jnp.sin(ang).reshape(shape).astype(x.dtype)
    x1, x2 = x[..., :half], x[..., half:]
    return jnp.concatenate([x1 * cos - x2 * sin, x1 * sin + x2 * cos], axis=-1)


def rope_2d(x, ang_row, ang_col):
    half = x.shape[-1] // 2
    return jnp.concatenate([apply_rope(x[..., :half], ang_row), apply_rope(x[..., half:], ang_col)], axis=-1)


def dwconv_centred(x, w, b):
    K, C = w.shape
    y = lax.conv_general_dilated(x, w[:, None, :].astype(x.dtype), window_strides=(1,),
                                 padding=[(K // 2, K // 2)], dimension_numbers=('NWC', 'WIO', 'NWC'),
                                 feature_group_count=C)
    return y + b


def chunked_scan(q, k, v, log_a, h0):
    Bsz, L, G, N = q.shape
    R, P = v.shape[-2:]
    nc = L // CHUNK
    qc = q.reshape(Bsz, nc, CHUNK, G, N)
    kc = k.reshape(Bsz, nc, CHUNK, G, N)
    vc = v.reshape(Bsz, nc, CHUNK, G, R, P)
    a_cs = jnp.cumsum(log_a.astype(jnp.float32).reshape(Bsz, nc, CHUNK, G, R), axis=2)
    idx = jnp.arange(CHUNK)
    causal = (idx[:, None] >= idx[None, :])[:, :, None, None]
    diff = a_cs[:, :, :, None] - a_cs[:, :, None, :]
    decay = jnp.exp(jnp.where(causal, diff, -jnp.inf))
    scores = jnp.einsum('bclgn,bcsgn->bclsg', qc, kc)
    y_diag = jnp.einsum('bclsgr,bcsgrp->bclgrp', scores[..., None] * decay, vc)
    to_end = jnp.exp(a_cs[:, :, -1:] - a_cs)
    states = jnp.einsum('bcsgn,bcsgrp->bcgrpn', kc, vc * to_end[..., None])
    a_tot = jnp.exp(a_cs[:, :, -1])

    def step(h, inp):
        s_c, g_c = inp
        return h * g_c[..., None, None] + s_c, h

    h_fin, h_in = lax.scan(step, h0.astype(jnp.float32), (jnp.moveaxis(states, 1, 0), jnp.moveaxis(a_tot, 1, 0)))
    h_in = jnp.moveaxis(h_in, 0, 1)
    y_off = jnp.einsum('bclgn,bcgrpn->bclgrp', qc, h_in) * jnp.exp(a_cs)[..., None]
    y = (y_diag + y_off).reshape(Bsz, L, G, R, P)
    return y.astype(v.dtype), h_fin


def bidir_scan(q, k, v_f, v_b, la_f, la_b, h0_f, h0_b):
    y_f, h_f = chunked_scan(q, k, v_f, la_f, h0_f)
    flip = lambda t: jnp.flip(t, axis=1)
    y_b, h_b = chunked_scan(flip(q), flip(k), flip(v_b), flip(la_b), h0_b)
    return y_f + flip(y_b), h_f, h_b


def even_prepare(h, ang, w_in, conv_w, conv_b, dt_bias, a_log, ret_decay):
    Bsz, L, _ = h.shape
    G, R = SSD_GROUPS, SSD_HEADS // SSD_GROUPS
    cuts = np.cumsum([SSD_INNER, SSD_CONV_CH, 2 * SSD_HEADS, RET_QK, RET_QK, RET_INNER]).tolist()
    z, xbc, dt, rq, rk, rv, rg = jnp.split(h @ w_in, cuts, axis=-1)
    xbc = jax.nn.silu(dwconv_centred(xbc, conv_w, conv_b))
    xs, bm, cm = jnp.split(xbc, [SSD_INNER, SSD_INNER + SSD_GROUPS * SSD_STATE], axis=-1)
    xs = xs.reshape(Bsz, L, G, R, SSD_HEAD_DIM)
    bm = bm.reshape(Bsz, L, G, SSD_STATE)
    cm = cm.reshape(Bsz, L, G, SSD_STATE)
    dt = jax.nn.softplus((dt.reshape(Bsz, L, 2, SSD_HEADS) + dt_bias).astype(jnp.float32))
    log_a = (dt * -jnp.exp(a_log.astype(jnp.float32))).reshape(Bsz, L, 2, G, R)
    xdt = xs[:, :, None] * dt.reshape(Bsz, L, 2, G, R, 1).astype(xs.dtype)
    ssd = (cm, bm, xdt[:, :, 0], xdt[:, :, 1], log_a[:, :, 0], log_a[:, :, 1])
    rq = apply_rope(rq.reshape(Bsz, L, RET_HEADS, RET_QK_DIM), ang)
    rk = apply_rope(rk.reshape(Bsz, L, RET_HEADS, RET_QK_DIM), ang) * (RET_QK_DIM ** -0.5)
    rv = rv.reshape(Bsz, L, RET_HEADS, 1, RET_V_DIM)
    rla = jnp.broadcast_to(-jnp.exp(ret_decay.astype(jnp.float32))[:, None, None, :, None],
                           (2, Bsz, L, RET_HEADS, 1))
    ret = (rq, rk, rv, rv, rla[0], rla[1])
    return ssd, ret, xs, z, rg


def even_finish(ssd_y, ret_y, xs, z, rg, d_skip, ssd_norm, w_out):
    Bsz, L = z.shape[:2]
    y = ssd_y + xs * d_skip.reshape(SSD_GROUPS, -1, 1)
    y = rms_norm(y.reshape(Bsz, L, SSD_INNER) * jax.nn.silu(z), ssd_norm)
    r = rms_norm(ret_y.reshape(Bsz, L, RET_HEADS, RET_V_DIM)).reshape(Bsz, L, RET_INNER) * jax.nn.silu(rg)
    return jnp.concatenate([y, r], axis=-1) @ w_out


def odd_qkv(h, w_in, q_norm, k_norm, ang_row=None, ang_col=None):
    Bsz, L, _ = h.shape
    q, k, v = jnp.split(h @ w_in, 3, axis=-1)
    q = rms_norm(q.reshape(Bsz, L, DIFF_HEADS, 2, DIFF_HEAD_DIM), q_norm)
    k = rms_norm(k.reshape(Bsz, L, DIFF_HEADS, 2, DIFF_HEAD_DIM), k_norm)
    if ang_row is not None:
        q = rope_2d(q, ang_row, ang_col)
        k = rope_2d(k, ang_row, ang_col)
    return q * (DIFF_HEAD_DIM ** -0.5), k, v.reshape(Bsz, L, DIFF_HEADS, DIFF_V_DIM)


def diff_attend(q, k, v, lam):
    s = jnp.einsum('bqhmd,bkhmd->bhmqk', q, k).astype(jnp.float32)
    p = jax.nn.softmax(s, axis=-1)
    a = p[:, :, 0] - lam.astype(jnp.float32) * p[:, :, 1]
    return jnp.einsum('bhqk,bkhe->bqhe', a.astype(v.dtype), v)


def diff_out(o, lam_init, subln, w_out):
    Bsz, L = o.shape[:2]
    o = rms_norm(o, subln) * (1.0 - lam_init)
    return o.reshape(Bsz, L, DIFF_INNER) @ w_out


def swiglu(h, w_gate, w_up, w_down):
    return (jax.nn.silu(h @ w_gate) * (h @ w_up)) @ w_down


def moe(h, w_router, w_gate, w_up, w_down):
    logits = (h @ w_router).astype(jnp.float32)
    top_val, top_idx = lax.top_k(logits, TOP_K)
    top_w = jax.nn.softmax(top_val, axis=-1)
    combine = jnp.sum(jax.nn.one_hot(top_idx, N_EXPERTS, dtype=jnp.float32) * top_w[..., None], axis=-2)
    out = jnp.zeros_like(h)
    for e in range(N_EXPERTS):
        out = out + combine[..., e:e + 1].astype(h.dtype) * swiglu(h, w_gate[e], w_up[e], w_down[e])
    return out


def setup_inputs(seed: int = 0) -> dict:
    key = jax.random.key(seed)
    ks = iter(jax.random.split(key, 64))
    f32 = jnp.float32
    D = D_MODEL
    ne, no = (DEPTH + 1) // 2, DEPTH // 2

    def nrm(shape, scale):
        return jax.random.normal(next(ks), shape, f32) * scale

    def gain(shape):
        return 1.0 + nrm(shape, 0.01)

    x = nrm((BATCH, SEQ, D), 1.0)
    c = nrm((BATCH, D), 1.0)
    ctx = nrm((BATCH, CTX_LEN, D), 1.0)
    c_ctx = nrm((D,), 1.0)
    u = jax.random.uniform(next(ks), (ne, 2, SSD_HEADS), f32)
    dt0 = jnp.exp(u * (math.log(0.1) - math.log(0.001)) + math.log(0.001))
    gam = 1.0 - 2.0 ** (-5.0 - jnp.arange(RET_HEADS, dtype=f32))
    return {
        "x": x, "c": c, "ctx": ctx, "c_ctx": c_ctx,
        "even_w_mod": nrm((ne, D, 6 * D), 0.5 * D ** -0.5),
        "even_b_mod": nrm((ne, 6 * D), 0.01),
        "even_norm1": gain((ne, D)),
        "even_norm2": gain((ne, D)),
        "even_w_in": nrm((ne, D, EVEN_IN), D ** -0.5),
        "even_conv_w": nrm((ne, SSD_CONV, SSD_CONV_CH), SSD_CONV ** -0.5),
        "even_conv_b": nrm((ne, SSD_CONV_CH), 0.01),
        "even_dt_bias": dt0 + jnp.log(-jnp.expm1(-dt0)),
        "even_a_log": jnp.log(jax.random.uniform(next(ks), (ne, 2, SSD_HEADS), f32, 1.0, 16.0)),
        "even_d": gain((ne, SSD_HEADS)),
        "even_ssd_norm": gain((ne, SSD_INNER)),
        "even_ret_decay": jnp.log(-jnp.log(gam)) + nrm((ne, 2, RET_HEADS), 0.05),
        "even_w_out": nrm((ne, EVEN_MIX, D), EVEN_MIX ** -0.5),
        "even_ffn_gate": nrm((ne, D, FFN_DENSE), D ** -0.5),
        "even_ffn_up": nrm((ne, D, FFN_DENSE), D ** -0.5),
        "even_ffn_down": nrm((ne, FFN_DENSE, D), FFN_DENSE ** -0.5),
        "odd_w_mod": nrm((no, D, 6 * D), 0.5 * D ** -0.5),
        "odd_b_mod": nrm((no, 6 * D), 0.01),
        "odd_norm1": gain((no, D)),
        "odd_norm2": gain((no, D)),
        "odd_w_in": nrm((no, D, ODD_IN), D ** -0.5),
        "odd_q_norm": gain((no, DIFF_HEAD_DIM)),
        "odd_k_norm": gain((no, DIFF_HEAD_DIM)),
        "odd_lambda": nrm((no, 4, DIFF_HEAD_DIM), 0.1),
        "odd_subln": gain((no, DIFF_V_DIM)),
        "odd_w_out": nrm((no, DIFF_INNER, D), DIFF_INNER ** -0.5),
        "odd_router": nrm((no, D, N_EXPERTS), D ** -0.5),
        "odd_exp_gate": nrm((no, N_EXPERTS, D, FFN_EXPERT), D ** -0.5),
        "odd_exp_up": nrm((no, N_EXPERTS, D, FFN_EXPERT), D ** -0.5),
        "odd_exp_down": nrm((no, N_EXPERTS, FFN_EXPERT, D), FFN_EXPERT ** -0.5),
    }


def reference(x, c, ctx, c_ctx, even_w_mod, even_b_mod, even_norm1, even_norm2, even_w_in, even_conv_w,
              even_conv_b, even_dt_bias, even_a_log, even_d, even_ssd_norm, even_ret_decay, even_w_out,
              even_ffn_gate, even_ffn_up, even_ffn_down, odd_w_mod, odd_b_mod, odd_norm1, odd_norm2, odd_w_in,
              odd_q_norm, odd_k_norm, odd_lambda, odd_subln, odd_w_out, odd_router, odd_exp_gate, odd_exp_up,
              odd_exp_down):
    Bsz, L, _ = x.shape
    Lc = ctx.shape[1]
    n_rows = L // GRID_W
    row = jnp.broadcast_to(jnp.arange(n_rows)[:, None], (n_rows, GRID_W)).reshape(-1)
    col = jnp.broadcast_to(jnp.arange(GRID_W)[None, :], (n_rows, GRID_W)).reshape(-1)
    ang_row = rope_angles(row, DIFF_HEAD_DIM // 2, ROPE_THETA)
    ang_col = rope_angles(col, DIFF_HEAD_DIM // 2, ROPE_THETA)
    ang_ret_ctx = rope_angles(jnp.arange(Lc), RET_QK_DIM, ROPE_THETA)
    ang_ret_lat = rope_angles(Lc + jnp.arange(L), RET_QK_DIM, ROPE_THETA)
    zeros_ssd = jnp.zeros((Bsz, SSD_GROUPS, SSD_HEADS // SSD_GROUPS, SSD_HEAD_DIM, SSD_STATE), jnp.float32)
    zeros_ret = jnp.zeros((Bsz, RET_HEADS, 1, RET_V_DIM, RET_QK_DIM), jnp.float32)

    xl, xc = x, ctx
    for i in range(DEPTH):
        j = i // 2
        last = i == DEPTH - 1
        if i % 2 == 0:
            sh1, sc1, g1, sh2, sc2, g2 = adaln(c, even_w_mod[j], even_b_mod[j])
            csh1, csc1, cg1, csh2, csc2, cg2 = adaln(c_ctx, even_w_mod[j], even_b_mod[j])
            hl = modulate(xl, even_norm1[j], sh1, sc1)
            hc = modulate(xc, even_norm1[j], csh1, csc1)
            p_args = (even_w_in[j], even_conv_w[j], even_conv_b[j], even_dt_bias[j], even_a_log[j], even_ret_decay[j])
            ssd_c, ret_c, xs_c, z_c, rg_c = even_prepare(hc, ang_ret_ctx, *p_args)
            ssd_l, ret_l, xs_l, z_l, rg_l = even_prepare(hl, ang_ret_lat, *p_args)
            ys_c, hs_f, hs_b = bidir_scan(*ssd_c, zeros_ssd, zeros_ssd)
            yr_c, hr_f, hr_b = bidir_scan(*ret_c, zeros_ret, zeros_ret)
            ys_l, _, _ = bidir_scan(*ssd_l, hs_f, hs_b)
            yr_l, _, _ = bidir_scan(*ret_l, hr_f, hr_b)
            xl = xl + g1 * even_finish(ys_l, yr_l, xs_l, z_l, rg_l, even_d[j], even_ssd_norm[j], even_w_out[j])
            hl = modulate(xl, even_norm2[j], sh2, sc2)
            xl = xl + g2 * swiglu(hl, even_ffn_gate[j], even_ffn_up[j], even_ffn_down[j])
            if not last:
                xc = xc + cg1 * even_finish(ys_c, yr_c, xs_c, z_c, rg_c, even_d[j], even_ssd_norm[j], even_w_out[j])
                hc = modulate(xc, even_norm2[j], csh2, csc2)
                xc = xc + cg2 * swiglu(hc, even_ffn_gate[j], even_ffn_up[j], even_ffn_down[j])
        else:
            sh1, sc1, g1, sh2, sc2, g2 = adaln(c, odd_w_mod[j], odd_b_mod[j])
            csh1, csc1, cg1, csh2, csc2, cg2 = adaln(c_ctx, odd_w_mod[j], odd_b_mod[j])
            hl = modulate(xl, odd_norm1[j], sh1, sc1)
            hc = modulate(xc, odd_norm1[j], csh1, csc1)
            lam_init = 0.8 - 0.6 * math.exp(-0.3 * i)
            lq1, lk1, lq2, lk2 = odd_lambda[j]
            lam = jnp.exp(jnp.sum(lq1 * lk1)) - jnp.exp(jnp.sum(lq2 * lk2)) + lam_init
            q_c, k_c, v_c = odd_qkv(hc, odd_w_in[j], odd_q_norm[j], odd_k_norm[j])
            q_l, k_l, v_l = odd_qkv(hl, odd_w_in[j], odd_q_norm[j], odd_k_norm[j], ang_row, ang_col)
            k_all = jnp.concatenate([k_c, k_l], axis=1)
            v_all = jnp.concatenate([v_c, v_l], axis=1)
            nb = L // Q_BLOCK
            qb = jnp.moveaxis(q_l.reshape(Bsz, nb, Q_BLOCK, DIFF_HEADS, 2, DIFF_HEAD_DIM), 1, 0)
            ob = lax.map(lambda q_blk: diff_attend(q_blk, k_all, v_all, lam), qb)
            o_l = jnp.moveaxis(ob, 0, 1).reshape(Bsz, L, DIFF_HEADS, DIFF_V_DIM)
            xl = xl + g1 * diff_out(o_l, lam_init, odd_subln[j], odd_w_out[j])
            hl = modulate(xl, odd_norm2[j], sh2, sc2)
            xl = xl + g2 * moe(hl, odd_router[j], odd_exp_gate[j], odd_exp_up[j], odd_exp_down[j])
            if not last:
                o_c = diff_attend(q_c, k_c, v_c, lam)
                xc = xc + cg1 * diff_out(o_c, lam_init, odd_subln[j], odd_w_out[j])
                hc = modulate(xc, odd_norm2[j], csh2, csc2)
                xc = xc + cg2 * moe(hc, odd_router[j], odd_exp_gate[j], odd_exp_up[j], odd_exp_down[j])
    return xl
```

```python
import functools
import math

import jax
import jax.numpy as jnp
from jax import lax
from jax.experimental import pallas as pl
from jax.experimental.pallas import tpu as pltpu

F32 = jnp.float32
BF16 = jnp.bfloat16
LANES = 128
SUBLANES = 8
VMEM_LIMIT = 56 * 1024 * 1024

GRID_W = 64
CHUNK = 128
NORM_EPS = 1e-6
ROPE_THETA = 10000.0
SSD_HEADS = 16
SSD_HEAD_DIM = 64
SSD_INNER = SSD_HEADS * SSD_HEAD_DIM
SSD_GROUPS = 2
SSD_STATE = 128
SSD_BC = SSD_GROUPS * SSD_STATE
SSD_CONV_CH = SSD_INNER + 2 * SSD_BC
RET_HEADS = 4
RET_QK_DIM = 128
RET_V_DIM = 256
RET_QK = RET_HEADS * RET_QK_DIM
RET_INNER = RET_HEADS * RET_V_DIM
DIFF_HEADS = 8
DIFF_HEAD_DIM = 64
DIFF_V_DIM = 2 * DIFF_HEAD_DIM
DIFF_INNER = DIFF_HEADS * DIFF_V_DIM
N_EXPERTS = 8
NEG_BIG = -1e30


def _params(sem):
    return pltpu.CompilerParams(dimension_semantics=sem, vmem_limit_bytes=VMEM_LIMIT)


def _tile(n, pref):
    t = min(n, pref)
    assert n % t == 0, (n, t)
    return t


def _silu(x):
    return x * jax.nn.sigmoid(x)


def _rms(x):
    return x * lax.rsqrt(jnp.mean(x * x, axis=-1, keepdims=True) + NORM_EPS)


def _modulate(x, gain, shift, scale):
    return _rms(x) * gain * (1.0 + scale) + shift


def _dot(a, b):
    return jnp.dot(a, b, preferred_element_type=F32)


def _dot_nt(a, b):
    return lax.dot_general(a, b, (((1,), (1,)), ((), ())), preferred_element_type=F32)


def _split3(a):
    hi = a.astype(BF16)
    r1 = a - hi.astype(F32)
    mid = r1.astype(BF16)
    lo = (r1 - mid.astype(F32)).astype(BF16)
    return hi, mid, lo


def _dot_f32(a, b):
    ah, am, al = _split3(a)
    bh, bm, bl = _split3(b)
    small = _dot(ah, bl) + _dot(al, bh) + _dot(am, bm)
    return small + (_dot(ah, bm) + _dot(am, bh)) + _dot(ah, bh)


def _adaln_kernel(c_ref, w_ref, b_ref, o_ref):
    a = _silu(c_ref[...]).astype(BF16)
    o_ref[...] = _dot(a, w_ref[...].astype(BF16)) + b_ref[...]


def _adaln(cond, w, b):
    R, D = cond.shape
    N = w.shape[1]
    tn = _tile(N, 1024)
    return pl.pallas_call(
        _adaln_kernel,
        grid=(N // tn,),
        in_specs=[pl.BlockSpec((R, D), lambda j: (0, 0)),
                  pl.BlockSpec((D, tn), lambda j: (0, j)),
                  pl.BlockSpec((1, tn), lambda j: (0, j))],
        out_specs=pl.BlockSpec((R, tn), lambda j: (0, j)),
        out_shape=jax.ShapeDtypeStruct((R, N), F32),
        compiler_params=_params(("arbitrary",)),
        name="adaln",
    )(cond, w, b.reshape(1, N))


EV_Z = 0
EV_XBC = EV_Z + SSD_INNER
EV_RQ = EV_XBC + SSD_CONV_CH
EV_RK = EV_RQ + RET_QK
EV_RV = EV_RK + RET_QK
EV_RG = EV_RV + RET_INNER
EV_DT = EV_RG + RET_INNER
EV_END = EV_DT + LANES


def _even_in_kernel(x_ref, mod_ref, gain_ref, w_ref, cos_ref, sin_ref,
                    z_ref, xbc_ref, dt_ref, rq_ref, rk_ref, rv_ref, rg_ref):
    mod = mod_ref[0]
    h = _modulate(x_ref[0], gain_ref[...], mod[0:1], mod[1:2]).astype(BF16)

    def mm(lo, hi):
        return _dot(h, w_ref[:, lo:hi])

    z_ref[0] = mm(EV_Z, EV_XBC)
    xbc_ref[0] = mm(EV_XBC, EV_RQ)
    cos = cos_ref[...]
    sin = sin_ref[...]
    k_scale = RET_QK_DIM ** -0.5
    for j in range(RET_HEADS):
        lo = j * RET_QK_DIM
        a = mm(EV_RQ + lo, EV_RQ + lo + RET_QK_DIM)
        rq_ref[0, :, lo:lo + RET_QK_DIM] = (a * cos + pltpu.roll(a, RET_QK_DIM // 2, 1) * sin).astype(BF16)
        a = mm(EV_RK + lo, EV_RK + lo + RET_QK_DIM)
        rk_ref[0, :, lo:lo + RET_QK_DIM] = (
            (a * cos + pltpu.roll(a, RET_QK_DIM // 2, 1) * sin) * k_scale).astype(BF16)
    rv_ref[0] = mm(EV_RV, EV_RG).astype(BF16)
    rg_ref[0] = mm(EV_RG, EV_DT)
    dt_ref[0] = mm(EV_DT, EV_END)


def _even_in(x, mod, gain, w, cosf, sinf, tm):
    B, S, D = x.shape
    tm = _tile(S, tm)
    bm = mod.shape[0]
    mod_map = (lambda b, i: (b, 0, 0)) if bm > 1 else (lambda b, i: (0, 0, 0))
    row = lambda n: pl.BlockSpec((1, tm, n), lambda b, i: (b, i, 0))
    outs = [(SSD_INNER, F32), (SSD_CONV_CH, F32), (LANES, F32), (RET_QK, BF16), (RET_QK, BF16),
            (RET_INNER, BF16), (RET_INNER, F32)]
    return pl.pallas_call(
        _even_in_kernel,
        grid=(B, S // tm),
        in_specs=[row(D),
                  pl.BlockSpec((1, 6, D), mod_map),
                  pl.BlockSpec((1, D), lambda b, i: (0, 0)),
                  pl.BlockSpec((D, EV_END), lambda b, i: (0, 0)),
                  pl.BlockSpec((tm, LANES), lambda b, i: (i, 0)),
                  pl.BlockSpec((tm, LANES), lambda b, i: (i, 0))],
        out_specs=[row(n) for n, _ in outs],
        out_shape=[jax.ShapeDtypeStruct((B, S, n), dt) for n, dt in outs],
        compiler_params=_params(("arbitrary", "arbitrary")),
        name="even_in",
    )(x, mod, gain, w, cosf, sinf)


def _ssd_prep_kernel(xbc_ref, prev_ref, next_ref, dt_ref, cw_ref, cb_ref, dtb_ref, alog_ref,
                     xs_ref, bm_ref, cm_ref, acol_ref, arow_ref, wrow_ref, *, tm, chunk):
    i = pl.program_id(1)
    n = pl.num_programs(1)
    x = xbc_ref[0]
    zero_row = jnp.zeros((1, x.shape[1]), F32)
    pr = jnp.where(i > 0, prev_ref[0][SUBLANES - 1:SUBLANES, :], zero_row)
    nx = jnp.where(i < n - 1, next_ref[0][0:1, :], zero_row)
    rows = lax.broadcasted_iota(jnp.int32, (tm, 1), 0)
    xp = jnp.where(rows == 0, pr, pltpu.roll(x, 1, 0))
    xn = jnp.where(rows == tm - 1, nx, pltpu.roll(x, tm - 1, 0))
    cw = cw_ref[...]
    y = _silu(xp * cw[0:1] + x * cw[1:2] + xn * cw[2:3] + cb_ref[...])
    xs_ref[0] = y[:, :SSD_INNER]
    bm_ref[0] = y[:, SSD_INNER:SSD_INNER + SSD_BC].astype(BF16)
    cm_ref[0] = y[:, SSD_INNER + SSD_BC:].astype(BF16)

    t = dt_ref[0] + dtb_ref[...]
    dt = jnp.maximum(t, 0.0) + jnp.log(1.0 + jnp.exp(-jnp.abs(t)))
    la = -dt * jnp.exp(alog_ref[...])
    rmod = rows % chunk
    fwd = la
    rev = la
    sh = 1
    while sh < chunk:
        fwd = fwd + jnp.where(rmod >= sh, pltpu.roll(fwd, sh, 0), 0.0)
        rev = rev + jnp.where(rmod < chunk - sh, pltpu.roll(rev, tm - sh, 0), 0.0)
        sh *= 2
    lane = lax.broadcasted_iota(jnp.int32, (1, LANES), 1)
    acc = jnp.where(lane < SSD_HEADS, fwd, rev)
    acc_t = acc.T
    dt_t = dt.T
    hp = SSD_HEADS // SSD_GROUPS
    for j in range(2 * SSD_GROUPS):
        acol_ref[0, j] = acc[:, j * hp:(j + 1) * hp]
        arow_ref[0, j] = acc_t[j * hp:(j + 1) * hp, :]
        wrow_ref[0, j] = dt_t[j * hp:(j + 1) * hp, :]


def _ssd_prep(xbc, dt, conv_w, conv_b, dtb, alog, tm):
    B, S, _ = xbc.shape
    tm = _tile(S, tm)
    assert tm % CHUNK == 0
    nh = S // SUBLANES
    hp = SSD_HEADS // SSD_GROUPS
    gd = 2 * SSD_GROUPS
    row = lambda n: pl.BlockSpec((1, tm, n), lambda b, i: (b, i, 0))
    const = lambda r, n: pl.BlockSpec((r, n), lambda b, i: (0, 0))
    return pl.pallas_call(
        functools.partial(_ssd_prep_kernel, tm=tm, chunk=CHUNK),
        grid=(B, S // tm),
        in_specs=[row(SSD_CONV_CH),
                  pl.BlockSpec((1, SUBLANES, SSD_CONV_CH),
                               lambda b, i: (b, jnp.maximum(i * (tm // SUBLANES) - 1, 0), 0)),
                  pl.BlockSpec((1, SUBLANES, SSD_CONV_CH),
                               lambda b, i: (b, jnp.minimum((i + 1) * (tm // SUBLANES), nh - 1), 0)),
                  row(LANES),
                  const(3, SSD_CONV_CH), const(1, SSD_CONV_CH), const(1, LANES), const(1, LANES)],
        out_specs=[row(SSD_INNER), row(SSD_BC), row(SSD_BC),
                   pl.BlockSpec((1, gd, tm, hp), lambda b, i: (b, 0, i, 0)),
                   pl.BlockSpec((1, gd, hp, tm), lambda b, i: (b, 0, 0, i)),
                   pl.BlockSpec((1, gd, hp, tm), lambda b, i: (b, 0, 0, i))],
        out_shape=[jax.ShapeDtypeStruct((B, S, SSD_INNER), F32),
                   jax.ShapeDtypeStruct((B, S, SSD_BC), BF16),
                   jax.ShapeDtypeStruct((B, S, SSD_BC), BF16),
                   jax.ShapeDtypeStruct((B, gd, S, hp), F32),
                   jax.ShapeDtypeStruct((B, gd, hp, S), F32),
                   jax.ShapeDtypeStruct((B, gd, hp, S), F32)],
        compiler_params=_params(("arbitrary", "arbitrary")),
        name="ssd_prep",
    )(xbc, xbc, xbc, dt, conv_w, conv_b, dtb, alog)


def _scan_kernel(*refs, R, P, reverse, has_prev):
    if has_prev:
        q_ref, k_ref, v_ref, ac_ref, ar_ref, wr_ref, h0_ref, yp_ref, y_ref, hf_ref, h_scr = refs
    else:
        q_ref, k_ref, v_ref, ac_ref, ar_ref, wr_ref, h0_ref, y_ref, hf_ref, h_scr = refs
        yp_ref = None
    c = pl.program_id(2)

    @pl.when(c == 0)
    def _():
        h_scr[...] = h0_ref[0, 0]

    q = q_ref[0]
    k = k_ref[0]
    C = q.shape[0]
    s = _dot_nt(q, k)
    k_t = k.astype(F32).T
    ac = ac_ref[0, 0]
    ar = ar_ref[0, 0]
    wr = wr_ref[0, 0]
    li = lax.broadcasted_iota(jnp.int32, (C, C), 0)
    si = lax.broadcasted_iota(jnp.int32, (C, C), 1)
    mask = (si >= li) if reverse else (li >= si)
    far = 0 if reverse else C - 1
    width = max(P, LANES)
    heads_per = width // P
    lane_head = lax.broadcasted_iota(jnp.int32, (1, width), 1) // P
    for u in range(R * P // width):
        lo = u * width
        v = v_ref[0, :, lo:lo + width].astype(BF16)
        h_old = h_scr[:, lo:lo + width]
        y_diag = None
        for j in range(heads_per):
            r = u * heads_per + j
            a_col = ac[:, r:r + 1]
            a_row = ar[r:r + 1, :]
            w_row = wr[r:r + 1, :]
            tot = a_row[:, far:far + 1]
            decay = jnp.exp(jnp.where(mask, a_col - a_row, NEG_BIG))
            m = (s * decay * w_row).astype(BF16)
            kw = (k_t * (jnp.exp(tot - a_row) * w_row)).astype(BF16)
            yd = _dot(m, v)
            st = _dot(kw, v)
            e_col = jnp.exp(a_col)
            e_tot = jnp.exp(tot)
            if y_diag is None:
                y_diag, state, col_scale, tot_scale = yd, st, e_col, e_tot
            else:
                sel = lane_head == j
                y_diag = jnp.where(sel, yd, y_diag)
                state = jnp.where(sel, st, state)
                col_scale = jnp.where(sel, e_col, col_scale)
                tot_scale = jnp.where(sel, e_tot, tot_scale)
        y = y_diag + col_scale * _dot(q, h_old.astype(BF16))
        if yp_ref is not None:
            y = y + yp_ref[0, :, lo:lo + width]
        y_ref[0, :, lo:lo + width] = y
        h_scr[:, lo:lo + width] = tot_scale * h_old + state

    @pl.when(c == pl.num_programs(2) - 1)
    def _():
        hf_ref[0, 0] = h_scr[...]


def _scan(q, k, v, acol, arow, wrow, h0, y_prev, *, G, R, P, N, d, reverse):
    B, S, _ = q.shape
    C = _tile(S, CHUNK)
    nc = S // C
    RP = R * P
    cidx = (lambda c: nc - 1 - c) if reverse else (lambda c: c)
    bx = (lambda b: b) if acol.shape[0] > 1 else (lambda b: 0)
    seq = lambda n: pl.BlockSpec((1, C, n), lambda b, g, c: (b, cidx(c), g))
    in_specs = [seq(N), seq(N), seq(RP),
                pl.BlockSpec((1, 1, C, R), lambda b, g, c: (bx(b), d * G + g, cidx(c), 0)),
                pl.BlockSpec((1, 1, R, C), lambda b, g, c: (bx(b), d * G + g, 0, cidx(c))),
                pl.BlockSpec((1, 1, R, C), lambda b, g, c: (bx(b), d * G + g, 0, cidx(c))),
                pl.BlockSpec((1, 1, N, RP), lambda b, g, c: (b, g, 0, 0))]
    args = [q, k, v, acol, arow, wrow, h0]
    if y_prev is not None:
        in_specs.append(seq(RP))
        args.append(y_prev)
    return pl.pallas_call(
        functools.partial(_scan_kernel, R=R, P=P, reverse=reverse, has_prev=y_prev is not None),
        grid=(B, G, nc),
        in_specs=in_specs,
        out_specs=[seq(RP), pl.BlockSpec((1, 1, N, RP), lambda b, g, c: (b, g, 0, 0))],
        out_shape=[jax.ShapeDtypeStruct((B, S, G * RP), F32),
                   jax.ShapeDtypeStruct((B, G, N, RP), F32)],
        scratch_shapes=[pltpu.VMEM((N, RP), F32)],
        compiler_params=_params(("arbitrary", "arbitrary", "arbitrary")),
        name="scan_rev" if reverse else "scan_fwd",
    )(*args)


def _bidir_scan(q, k, v, acol, arow, wrow, h0_f, h0_b, **kw):
    y_f, h_f = _scan(q, k, v, acol, arow, wrow, h0_f, None, d=0, reverse=False, **kw)
    y, h_b = _scan(q, k, v, acol, arow, wrow, h0_b, y_f, d=1, reverse=True, **kw)
    return y, h_f, h_b


def _even_out_kernel(ys_ref, xs_ref, z_ref, yr_ref, rg_ref, x_ref, mod_ref, dvec_ref, sn_ref, w_ref, o_ref):
    y = ys_ref[0] + xs_ref[0] * dvec_ref[...]
    y = _rms(y * _silu(z_ref[0])) * sn_ref[...]
    out = _dot(y.astype(BF16), w_ref[0:SSD_INNER, :])
    yr = yr_ref[0]
    rg = _silu(rg_ref[0])
    for j in range(RET_HEADS):
        lo = j * RET_V_DIM
        r = _rms(yr[:, lo:lo + RET_V_DIM]) * rg[:, lo:lo + RET_V_DIM]
        out = out + _dot(r.astype(BF16), w_ref[SSD_INNER + lo:SSD_INNER + lo + RET_V_DIM, :])
    o_ref[0] = x_ref[0] + mod_ref[0][2:3] * out


def _even_out(ys, xs, z, yr, rg, x, mod, dvec, ssd_norm, w, tm):
    B, S, D = x.shape
    tm = _tile(S, tm)
    bm = mod.shape[0]
    mod_map = (lambda b, i: (b, 0, 0)) if bm > 1 else (lambda b, i: (0, 0, 0))
    row = lambda n: pl.BlockSpec((1, tm, n), lambda b, i: (b, i, 0))
    const = lambda r, n: pl.BlockSpec((r, n), lambda b, i: (0, 0))
    return pl.pallas_call(
        _even_out_kernel,
        grid=(B, S // tm),
        in_specs=[row(SSD_INNER), row(SSD_INNER), row(SSD_INNER), row(RET_INNER), row(RET_INNER), row(D),
                  pl.BlockSpec((1, 6, D), mod_map),
                  const(1, SSD_INNER), const(1, SSD_INNER), const(SSD_INNER + RET_INNER, D)],
        out_specs=row(D),
        out_shape=jax.ShapeDtypeStruct((B, S, D), F32),
        compiler_params=_params(("arbitrary", "arbitrary")),
        name="even_out",
    )(ys, xs, z, yr, rg, x, mod, dvec, ssd_norm, w)


def _ffn_kernel(x_ref, mod_ref, gain_ref, wg_ref, wu_ref, wd_ref, o_ref, h_scr, acc_scr):
    f = pl.program_id(2)
    mod = mod_ref[0]

    @pl.when(f == 0)
    def _():
        h_scr[...] = _modulate(x_ref[0], gain_ref[...], mod[3:4], mod[4:5]).astype(BF16)
        acc_scr[...] = jnp.zeros_like(acc_scr)

    h = h_scr[...]
    a = (_silu(_dot(h, wg_ref[...])) * _dot(h, wu_ref[...])).astype(BF16)
    acc_scr[...] += _dot(a, wd_ref[...])

    @pl.when(f == pl.num_programs(2) - 1)
    def _():
        o_ref[0] = x_ref[0] + mod[5:6] * acc_scr[...]


def _ffn(x, mod, gain, wg, wu, wd, tm, tf):
    B, S, D = x.shape
    F = wg.shape[1]
    tm = _tile(S, tm)
    tf = _tile(F, tf)
    bm = mod.shape[0]
    mod_map = (lambda b, i, f: (b, 0, 0)) if bm > 1 else (lambda b, i, f: (0, 0, 0))
    return pl.pallas_call(
        _ffn_kernel,
        grid=(B, S // tm, F // tf),
        in_specs=[pl.BlockSpec((1, tm, D), lambda b, i, f: (b, i, 0)),
                  pl.BlockSpec((1, 6, D), mod_map),
                  pl.BlockSpec((1, D), lambda b, i, f: (0, 0)),
                  pl.BlockSpec((D, tf), lambda b, i, f: (0, f)),
                  pl.BlockSpec((D, tf), lambda b, i, f: (0, f)),
                  pl.BlockSpec((tf, D), lambda b, i, f: (f, 0))],
        out_specs=pl.BlockSpec((1, tm, D), lambda b, i, f: (b, i, 0)),
        out_shape=jax.ShapeDtypeStruct((B, S, D), F32),
        scratch_shapes=[pltpu.VMEM((tm, D), BF16), pltpu.VMEM((tm, D), F32)],
        compiler_params=_params(("arbitrary", "arbitrary", "arbitrary")),
        name="ffn_dense",
    )(x, mod, gain, wg, wu, wd)


def _odd_in_kernel(x_ref, mod_ref, gain_ref, w_ref, qn_ref, kn_ref, bd_ref, cos_ref, sin_ref,
                   q_ref, k_ref, v_ref):
    mod = mod_ref[0]
    h = _modulate(x_ref[0], gain_ref[...], mod[0:1], mod[1:2]).astype(BF16)
    cos = cos_ref[...]
    sin = sin_ref[...]
    bd = bd_ref[...]
    lane = lax.broadcasted_iota(jnp.int32, (1, LANES), 1)
    quarter = DIFF_HEAD_DIM // 4
    first = (lane % (2 * quarter)) < quarter
    for base, n_ref, o_ref, scale in ((0, qn_ref, q_ref, DIFF_HEAD_DIM ** -0.5),
                                      (DIFF_INNER, kn_ref, k_ref, 1.0)):
        for j in range(DIFF_INNER // LANES):
            lo = j * LANES
            a = _dot(h, w_ref[:, base + lo:base + lo + LANES])
            sq = a * a
            hi = sq.astype(BF16)
            rest = (sq - hi.astype(F32)).astype(BF16)
            ms = (_dot(hi, bd) + _dot(rest, bd)) * (1.0 / DIFF_HEAD_DIM)
            a = a * lax.rsqrt(ms + NORM_EPS) * n_ref[...]
            partner = jnp.where(first, pltpu.roll(a, LANES - quarter, 1), pltpu.roll(a, quarter, 1))
            o_ref[0, :, lo:lo + LANES] = ((a * cos + partner * sin) * scale).astype(BF16)
    v_ref[0] = _dot(h, w_ref[:, 2 * DIFF_INNER:3 * DIFF_INNER]).astype(BF16)


def _odd_in(x, mod, gain, w, qn, kn, bd, cosf, sinf, tm):
    B, S, D = x.shape
    tm = _tile(S, tm)
    bm = mod.shape[0]
    mod_map = (lambda b, i: (b, 0, 0)) if bm > 1 else (lambda b, i: (0, 0, 0))
    row = lambda n: pl.BlockSpec((1, tm, n), lambda b, i: (b, i, 0))
    const = lambda r, n: pl.BlockSpec((r, n), lambda b, i: (0, 0))
    return pl.pallas_call(
        _odd_in_kernel,
        grid=(B, S // tm),
        in_specs=[row(D), pl.BlockSpec((1, 6, D), mod_map), const(1, D), const(D, 3 * DIFF_INNER),
                  const(1, LANES), const(1, LANES), const(LANES, LANES),
                  pl.BlockSpec((tm, LANES), lambda b, i: (i, 0)),
                  pl.BlockSpec((tm, LANES), lambda b, i: (i, 0))],
        out_specs=[row(DIFF_INNER)] * 3,
        out_shape=[jax.ShapeDtypeStruct((B, S, DIFF_INNER), BF16)] * 3,
        compiler_params=_params(("arbitrary", "arbitrary")),
        name="odd_in",
    )(x, mod, gain, w, qn, kn, bd, cosf, sinf)


def _attn_kernel(lam_ref, q_ref, kl_ref, vl_ref, kc_ref, vc_ref, sub_ref, o_ref,
                 q2_scr, m_scr, l_scr, acc_scr, *, tq, tk, out_scale):
    q = q_ref[0]
    lane = lax.broadcasted_iota(jnp.int32, (1, LANES), 1)
    lo_half = lane < DIFF_HEAD_DIM
    zero = jnp.zeros_like(q)
    q2_scr[0:tq, :] = jnp.where(lo_half, q, zero)
    q2_scr[tq:2 * tq, :] = jnp.where(lo_half, zero, q)
    m_scr[...] = jnp.full_like(m_scr, -jnp.inf)
    l_scr[...] = jnp.zeros_like(l_scr)
    acc_scr[...] = jnp.zeros_like(acc_scr)

    def update(k, v):
        s = _dot_nt(q2_scr[...], k)
        m_prev = m_scr[...]
        m_new = jnp.maximum(m_prev, jnp.max(s, axis=1, keepdims=True))
        alpha = jnp.exp(m_prev - m_new)
        p = jnp.exp(s - m_new[:, 0:1])
        l_scr[...] = alpha * l_scr[...] + jnp.sum(p, axis=1, keepdims=True)
        acc_scr[...] = alpha * acc_scr[...] + _dot(p.astype(BF16), v)
        m_scr[...] = m_new

    def body(j, carry):
        start = pl.multiple_of(j * tk, tk)
        update(kl_ref[0, pl.ds(start, tk), :], vl_ref[0, pl.ds(start, tk), :])
        return carry

    lax.fori_loop(0, kl_ref.shape[1] // tk, body, 0)
    update(kc_ref[0], vc_ref[0])

    o = acc_scr[...] / l_scr[...]
    o = o[0:tq] - lam_ref[0, 0] * o[tq:2 * tq]
    o_ref[0] = (_rms(o) * sub_ref[...] * out_scale).astype(BF16)


def _attention(lam, q, k_lat, v_lat, k_ctx, v_ctx, subln, out_scale, tq, tk):
    B, L, _ = q.shape
    Lc = k_ctx.shape[1]
    tq = _tile(L, tq)
    tk = _tile(L, tk)
    head = lambda n: pl.BlockSpec((1, n, LANES), lambda b, h, i: (b, 0, h))
    return pl.pallas_call(
        functools.partial(_attn_kernel, tq=tq, tk=tk, out_scale=out_scale),
        grid=(B, DIFF_HEADS, L // tq),
        in_specs=[pl.BlockSpec(memory_space=pltpu.SMEM),
                  pl.BlockSpec((1, tq, LANES), lambda b, h, i: (b, i, h)),
                  head(L), head(L), head(Lc), head(Lc),
                  pl.BlockSpec((1, LANES), lambda b, h, i: (0, 0))],
        out_specs=pl.BlockSpec((1, tq, LANES), lambda b, h, i: (b, i, h)),
        out_shape=jax.ShapeDtypeStruct((B, L, DIFF_INNER), BF16),
        scratch_shapes=[pltpu.VMEM((2 * tq, LANES), BF16), pltpu.VMEM((2 * tq, LANES), F32),
                        pltpu.VMEM((2 * tq, LANES), F32), pltpu.VMEM((2 * tq, LANES), F32)],
        compiler_params=_params(("arbitrary", "arbitrary", "arbitrary")),
        name="diff_attn",
    )(lam, q, k_lat, v_lat, k_ctx, v_ctx, subln)


def _odd_out_kernel(o_ref, x_ref, mod_ref, gain_ref, w_ref, wr_ref, xo_ref, h_ref, route_ref):
    mod = mod_ref[0]
    x1 = x_ref[0] + mod[2:3] * _dot(o_ref[0], w_ref[...])
    xo_ref[0] = x1
    h = _modulate(x1, gain_ref[...], mod[3:4], mod[4:5])
    h_ref[0] = h
    logits = _dot_f32(h, wr_ref[...])
    lane = lax.broadcasted_iota(jnp.int32, logits.shape, 1)
    lg = jnp.where(lane < N_EXPERTS, logits, -jnp.inf)
    v1 = jnp.max(lg, axis=1, keepdims=True)
    i1 = jnp.min(jnp.where(lg == v1, lane, LANES), axis=1, keepdims=True)
    lg2 = jnp.where(lane == i1, -jnp.inf, lg)
    v2 = jnp.max(lg2, axis=1, keepdims=True)
    i2 = jnp.min(jnp.where(lg2 == v2, lane, LANES), axis=1, keepdims=True)
    e = jnp.exp(v2 - v1)
    w1 = 1.0 / (1.0 + e)
    w2 = e * w1
    route = jnp.where(lane == 0, i1.astype(F32),
                      jnp.where(lane == 1, i2.astype(F32),
                                jnp.where(lane == 2, w1, jnp.where(lane == 3, w2, 0.0))))
    route_ref[0] = route


def _odd_out(o, x, mod, gain, w, w_router, tm):
    B, S, D = x.shape
    tm = _tile(S, tm)
    row = lambda n: pl.BlockSpec((1, tm, n), lambda b, i: (b, i, 0))
    const = lambda r, n: pl.BlockSpec((r, n), lambda b, i: (0, 0))
    return pl.pallas_call(
        _odd_out_kernel,
        grid=(B, S // tm),
        in_specs=[row(DIFF_INNER), row(D), pl.BlockSpec((1, 6, D), lambda b, i: (b, 0, 0)),
                  const(1, D), const(DIFF_INNER, D), const(D, LANES)],
        out_specs=[row(D), row(D), row(LANES)],
        out_shape=[jax.ShapeDtypeStruct((B, S, D), F32), jax.ShapeDtypeStruct((B, S, D), F32),
                   jax.ShapeDtypeStruct((B, S, LANES), F32)],
        compiler_params=_params(("arbitrary", "arbitrary")),
        name="odd_out",
    )(o, x, mod, gain, w, w_router)


def _row_copy(src, src_row, dst, dst_row, sem):
    return pltpu.make_async_copy(src.at[pl.ds(src_row, 1), :], dst.at[pl.ds(dst_row, 1), :], sem)


def _dispatch_kernel(slot_ref, h_ref, hs_in_ref, hs_ref, sem, *, tt):
    del hs_in_ref
    base = pl.program_id(0) * tt

    def issue(r, carry):
        _row_copy(h_ref, base + r, hs_ref, slot_ref[0, 0, 2 * r], sem).start()
        _row_copy(h_ref, base + r, hs_ref, slot_ref[0, 0, 2 * r + 1], sem).start()
        return carry

    lax.fori_loop(0, tt, issue, 0)

    def drain(r, carry):
        _row_copy(h_ref, base + r, hs_ref, slot_ref[0, 0, 2 * r], sem).wait()
        _row_copy(h_ref, base + r, hs_ref, slot_ref[0, 0, 2 * r + 1], sem).wait()
        return carry

    lax.fori_loop(0, tt, drain, 0)


def _dispatch(slots, h, n_rows, tt):
    T, D = h.shape
    tt = _tile(T, tt)
    hs0 = jnp.zeros((n_rows, D), F32)
    return pl.pallas_call(
        functools.partial(_dispatch_kernel, tt=tt),
        grid=(T // tt,),
        in_specs=[pl.BlockSpec((1, 1, 2 * tt), lambda i: (i, 0, 0), memory_space=pltpu.SMEM),
                  pl.BlockSpec(memory_space=pl.ANY),
                  pl.BlockSpec(memory_space=pl.ANY)],
        out_specs=pl.BlockSpec(memory_space=pl.ANY),
        out_shape=jax.ShapeDtypeStruct((n_rows, D), F32),
        scratch_shapes=[pltpu.SemaphoreType.DMA(())],
        input_output_aliases={2: 0},
        compiler_params=_params(("arbitrary",)),
        name="moe_dispatch",
    )(slots.reshape(T // tt, 1, 2 * tt), h, hs0)


def _expert_kernel(te_ref, tv_ref, hs_ref, wg_ref, wu_ref, wd_ref, ys_ref, h_scr, acc_scr):
    i = pl.program_id(0)
    f = pl.program_id(1)

    @pl.when(tv_ref[i] > 0)
    def _():
        @pl.when(f == 0)
        def _():
            h_scr[...] = hs_ref[...].astype(BF16)
            acc_scr[...] = jnp.zeros_like(acc_scr)

        h = h_scr[...]
        g = _dot(h, wg_ref[0].astype(BF16))
        u = _dot(h, wu_ref[0].astype(BF16))
        acc_scr[...] += _dot((_silu(g) * u).astype(BF16), wd_ref[0].astype(BF16))

        @pl.when(f == pl.num_programs(1) - 1)
        def _():
            ys_ref[...] = acc_scr[...]

    @pl.when((tv_ref[i] == 0) & (f == pl.num_programs(1) - 1))
    def _():
        ys_ref[...] = jnp.zeros_like(ys_ref)


def _experts(tile_expert, tile_valid, hs, wg, wu, wd, tm, tf):
    n_rows, D = hs.shape
    F = wg.shape[2]
    tf = _tile(F, tf)
    nf = F // tf
    fi = lambda i, f, te, tv: jnp.where(tv[i] > 0, f, nf - 1)
    grid_spec = pltpu.PrefetchScalarGridSpec(
        num_scalar_prefetch=2,
        grid=(n_rows // tm, nf),
        in_specs=[pl.BlockSpec((tm, D), lambda i, f, te, tv: (i, 0)),
                  pl.BlockSpec((1, D, tf), lambda i, f, te, tv: (te[i], 0, fi(i, f, te, tv))),
                  pl.BlockSpec((1, D, tf), lambda i, f, te, tv: (te[i], 0, fi(i, f, te, tv))),
                  pl.BlockSpec((1, tf, D), lambda i, f, te, tv: (te[i], fi(i, f, te, tv), 0))],
        out_specs=pl.BlockSpec((tm, D), lambda i, f, te, tv: (i, 0)),
        scratch_shapes=[pltpu.VMEM((tm, D), BF16), pltpu.VMEM((tm, D), F32)])
    return pl.pallas_call(
        _expert_kernel,
        grid_spec=grid_spec,
        out_shape=jax.ShapeDtypeStruct((n_rows, D), F32),
        compiler_params=_params(("arbitrary", "arbitrary")),
        name="moe_experts",
    )(tile_expert, tile_valid, hs, wg, wu, wd)


def _combine_kernel(slot_ref, ys_ref, x_ref, mod_ref, route_ref, o_ref, buf0, buf1, sem, *, tt):
    def issue(r, carry):
        _row_copy(ys_ref, slot_ref[0, 0, 2 * r], buf0, r, sem).start()
        _row_copy(ys_ref, slot_ref[0, 0, 2 * r + 1], buf1, r, sem).start()
        return carry

    lax.fori_loop(0, tt, issue, 0)

    def drain(r, carry):
        _row_copy(ys_ref, slot_ref[0, 0, 2 * r], buf0, r, sem).wait()
        _row_copy(ys_ref, slot_ref[0, 0, 2 * r + 1], buf1, r, sem).wait()
        return carry

    lax.fori_loop(0, tt, drain, 0)
    route = route_ref[0]
    y = route[:, 2:3] * buf0[...] + route[:, 3:4] * buf1[...]
    o_ref[0] = x_ref[0] + mod_ref[0][5:6] * y


def _combine(slots, ys, x, mod, route, tt):
    B, S, D = x.shape
    tt = _tile(S, tt)
    n = S // tt
    return pl.pallas_call(
        functools.partial(_combine_kernel, tt=tt),
        grid=(B, n),
        in_specs=[pl.BlockSpec((1, 1, 2 * tt), lambda b, i: (b * n + i, 0, 0), memory_space=pltpu.SMEM),
                  pl.BlockSpec(memory_space=pl.ANY),
                  pl.BlockSpec((1, tt, D), lambda b, i: (b, i, 0)),
                  pl.BlockSpec((1, 6, D), lambda b, i: (b, 0, 0)),
                  pl.BlockSpec((1, tt, LANES), lambda b, i: (b, i, 0))],
        out_specs=pl.BlockSpec((1, tt, D), lambda b, i: (b, i, 0)),
        out_shape=jax.ShapeDtypeStruct((B, S, D), F32),
        scratch_shapes=[pltpu.VMEM((tt, D), F32), pltpu.VMEM((tt, D), F32), pltpu.SemaphoreType.DMA(())],
        compiler_params=_params(("arbitrary", "arbitrary")),
        name="moe_combine",
    )(slots.reshape(B * n, 1, 2 * tt), ys, x, mod, route)


def _routing_tables(route, tm):
    T = route.shape[0]
    experts = route[:, 0:2].astype(jnp.int32).reshape(2 * T)
    onehot = (experts[:, None] == jnp.arange(N_EXPERTS, dtype=jnp.int32)[None, :]).astype(jnp.int32)
    rank = jnp.sum((jnp.cumsum(onehot, axis=0) - onehot) * onehot, axis=1)
    counts = jnp.sum(onehot, axis=0)
    tiles = (counts + tm - 1) // tm
    tile_end = jnp.cumsum(tiles)
    offsets = (tile_end - tiles) * tm
    slots = jnp.sum(onehot * offsets[None, :], axis=1) + rank
    n_tiles = (2 * T) // tm + N_EXPERTS
    ids = jnp.arange(n_tiles, dtype=jnp.int32)
    tile_expert = jnp.minimum(jnp.sum((ids[:, None] >= tile_end[None, :]).astype(jnp.int32), axis=1),
                              N_EXPERTS - 1)
    last_used = jnp.minimum(jnp.sum((tile_end[-1] - 1 >= tile_end).astype(jnp.int32)), N_EXPERTS - 1)
    tile_valid = (ids < tile_end[-1]).astype(jnp.int32)
    tile_expert = jnp.where(tile_valid > 0, tile_expert, last_used)
    return slots.astype(jnp.int32), tile_expert.astype(jnp.int32), tile_valid, n_tiles


def _rope_angles(pos, dim):
    inv = ROPE_THETA ** (-jnp.arange(dim // 2, dtype=F32) / (dim // 2))
    return pos.astype(F32)[:, None] * inv[None, :]


def _ret_rope_tables(pos):
    ang = _rope_angles(pos, RET_QK_DIM)
    cos, sin = jnp.cos(ang), jnp.sin(ang)
    return jnp.concatenate([cos, cos], axis=1), jnp.concatenate([-sin, sin], axis=1)


def _axial_rope_tables(L):
    n_rows = L // GRID_W
    row = jnp.broadcast_to(jnp.arange(n_rows)[:, None], (n_rows, GRID_W)).reshape(-1)
    col = jnp.broadcast_to(jnp.arange(GRID_W)[None, :], (n_rows, GRID_W)).reshape(-1)
    ar = _rope_angles(row, DIFF_HEAD_DIM // 2)
    ac = _rope_angles(col, DIFF_HEAD_DIM // 2)
    cos = jnp.concatenate([jnp.cos(ar), jnp.cos(ar), jnp.cos(ac), jnp.cos(ac)], axis=1)
    sin = jnp.concatenate([-jnp.sin(ar), jnp.sin(ar), -jnp.sin(ac), jnp.sin(ac)], axis=1)
    return jnp.tile(cos, (1, LANES // DIFF_HEAD_DIM)), jnp.tile(sin, (1, LANES // DIFF_HEAD_DIM))


def _ret_decay_tables(ret_decay, S):
    C = min(S, CHUNK)
    lam = -jnp.exp(ret_decay.astype(F32))
    pos = jnp.arange(S) % C
    steps = jnp.stack([pos + 1, C - pos]).astype(F32)
    a = (lam[:, :, None] * steps[:, None, :]).reshape(1, 2 * RET_HEADS, S)
    return a[:, :, :, None], a[:, :, None, :], jnp.ones((1, 2 * RET_HEADS, 1, S), F32)


def kernel(x, c, ctx, c_ctx, even_w_mod, even_b_mod, even_norm1, even_norm2, even_w_in, even_conv_w, even_conv_b, even_dt_bias, even_a_log, even_d, even_ssd_norm, even_ret_decay, even_w_out, even_ffn_gate, even_ffn_up, even_ffn_down, odd_w_mod, odd_b_mod, odd_norm1, odd_norm2, odd_w_in, odd_q_norm, odd_k_norm, odd_lambda, odd_subln, odd_w_out, odd_router, odd_exp_gate, odd_exp_up, odd_exp_down):
    B, L, D = x.shape
    Lc = ctx.shape[1]
    T = B * L
    row = lambda v: v.reshape(1, -1)
    pad_lanes = lambda v: jnp.pad(v.reshape(1, -1), ((0, 0), (0, LANES - v.size)))

    cond = jnp.concatenate([c, c_ctx[None, :], jnp.zeros((SUBLANES - B - 1, D), F32)], axis=0)

    mod = _adaln(cond, even_w_mod[0], even_b_mod[0]).reshape(SUBLANES, 6, D)
    mod_l, mod_c = mod[:B], mod[B:B + 1]
    w_in = even_w_in[0]
    cut = SSD_INNER + SSD_CONV_CH
    w_in = jnp.concatenate([w_in[:, :cut], w_in[:, cut + 2 * SSD_HEADS:], w_in[:, cut:cut + 2 * SSD_HEADS],
                            jnp.zeros((D, LANES - 2 * SSD_HEADS), F32)], axis=1).astype(BF16)
    g1 = row(even_norm1[0])
    cos_c, sin_c = _ret_rope_tables(jnp.arange(Lc))
    cos_l, sin_l = _ret_rope_tables(Lc + jnp.arange(L))
    dtb = pad_lanes(even_dt_bias[0])
    alog = pad_lanes(even_a_log[0])
    conv_b = row(even_conv_b[0])
    ssd_kw = dict(G=SSD_GROUPS, R=SSD_HEADS // SSD_GROUPS, P=SSD_HEAD_DIM, N=SSD_STATE)
    ret_kw = dict(G=RET_HEADS, R=1, P=RET_V_DIM, N=RET_QK_DIM)

    def mix(xin, modv, cosf, sinf, hs, hr, tm):
        z, xbc, dt, rq, rk, rv, rg = _even_in(xin, modv, g1, w_in, cosf, sinf, tm)
        xs, bmat, cmat, acol, arow, wrow = _ssd_prep(xbc, dt, even_conv_w[0], conv_b, dtb, alog, tm)
        ys, hs_f, hs_b = _bidir_scan(cmat, bmat, xs, acol, arow, wrow, hs[0], hs[1], **ssd_kw)
        racol, rarow, rwrow = _ret_decay_tables(even_ret_decay[0], xin.shape[1])
        yr, hr_f, hr_b = _bidir_scan(rq, rk, rv, racol, rarow, rwrow, hr[0], hr[1], **ret_kw)
        return (ys, xs, z, yr, rg), (hs_f, hs_b), (hr_f, hr_b)

    zs = jnp.zeros((B, SSD_GROUPS, SSD_STATE, SSD_INNER // SSD_GROUPS), F32)
    zr = jnp.zeros((B, RET_HEADS, RET_QK_DIM, RET_V_DIM), F32)
    tm_c = 256
    tm_l = 512
    mixed_c, hs, hr = mix(ctx, mod_c, cos_c, sin_c, (zs, zs), (zr, zr), tm_c)
    mixed_l, _, _ = mix(x, mod_l, cos_l, sin_l, hs, hr, tm_l)

    dvec = row(jnp.repeat(even_d[0], SSD_HEAD_DIM))
    snorm = row(even_ssd_norm[0])
    w_out = even_w_out[0].astype(BF16)
    g2 = row(even_norm2[0])
    wg = even_ffn_gate[0].astype(BF16)
    wu = even_ffn_up[0].astype(BF16)
    wd = even_ffn_down[0].astype(BF16)
    ffn_tf = 1408

    xl = _even_out(*mixed_l, x, mod_l, dvec, snorm, w_out, tm_l)
    xl = _ffn(xl, mod_l, g2, wg, wu, wd, tm_l, ffn_tf)
    xc = _even_out(*mixed_c, ctx, mod_c, dvec, snorm, w_out, tm_c)
    xc = _ffn(xc, mod_c, g2, wg, wu, wd, tm_c, ffn_tf)

    mod = _adaln(cond, odd_w_mod[0], odd_b_mod[0]).reshape(SUBLANES, 6, D)
    mod_l, mod_c = mod[:B], mod[B:B + 1]
    lam_init = 0.8 - 0.6 * math.exp(-0.3 * 1)
    lq1, lk1, lq2, lk2 = odd_lambda[0]
    lam = (jnp.exp(jnp.sum(lq1 * lk1)) - jnp.exp(jnp.sum(lq2 * lk2)) + lam_init).reshape(1, 1)
    w_in = odd_w_in[0].astype(BF16)
    g1 = row(odd_norm1[0])
    qn = row(jnp.tile(odd_q_norm[0], LANES // DIFF_HEAD_DIM))
    kn = row(jnp.tile(odd_k_norm[0], LANES // DIFF_HEAD_DIM))
    gid = jnp.arange(LANES) // DIFF_HEAD_DIM
    bd = (gid[:, None] == gid[None, :]).astype(BF16)
    cos_l, sin_l = _axial_rope_tables(L)
    cos_c, sin_c = jnp.ones((Lc, LANES), F32), jnp.zeros((Lc, LANES), F32)
    q_l, k_l, v_l = _odd_in(xl, mod_l, g1, w_in, qn, kn, bd, cos_l, sin_l, tm_l)
    _, k_c, v_c = _odd_in(xc, mod_c, g1, w_in, qn, kn, bd, cos_c, sin_c, tm_c)
    o = _attention(lam, q_l, k_l, v_l, k_c, v_c, row(odd_subln[0]), 1.0 - lam_init, 512, 512)

    w_router = jnp.pad(odd_router[0], ((0, 0), (0, LANES - N_EXPERTS)))
    xl, h, route = _odd_out(o, xl, mod_l, row(odd_norm2[0]), odd_w_out[0].astype(BF16), w_router, tm_l)
    moe_tm = min(1024, T)
    slots, tile_expert, tile_valid, n_tiles = _routing_tables(route.reshape(T, LANES), moe_tm)
    hs_sorted = _dispatch(slots, h.reshape(T, D), n_tiles * moe_tm, 512)
    ys_sorted = _experts(tile_expert, tile_valid, hs_sorted, odd_exp_gate[0], odd_exp_up[0], odd_exp_down[0],
                         moe_tm, 512)
    return _combine(slots, ys_sorted, xl, mod_l, route, 256)
```

```python
import functools
import math

import jax
import jax.numpy as jnp
from jax import lax
from jax.experimental import pallas as pl
from jax.experimental.pallas import tpu as pltpu

F32 = jnp.float32
BF16 = jnp.bfloat16
LANES = 128
SUBLANES = 8
VMEM_LIMIT = 56 * 1024 * 1024

GRID_W = 64
CHUNK = 128
NORM_EPS = 1e-6
ROPE_THETA = 10000.0
SSD_HEADS = 16
SSD_HEAD_DIM = 64
SSD_INNER = SSD_HEADS * SSD_HEAD_DIM
SSD_GROUPS = 2
SSD_STATE = 128
SSD_BC = SSD_GROUPS * SSD_STATE
SSD_CONV_CH = SSD_INNER + 2 * SSD_BC
RET_HEADS = 4
RET_QK_DIM = 128
RET_V_DIM = 256
RET_QK = RET_HEADS * RET_QK_DIM
RET_INNER = RET_HEADS * RET_V_DIM
DIFF_HEADS = 8
DIFF_HEAD_DIM = 64
DIFF_V_DIM = 2 * DIFF_HEAD_DIM
DIFF_INNER = DIFF_HEADS * DIFF_V_DIM
N_EXPERTS = 8
NEG_BIG = -1e30
SHIFT_HEADROOM = 30.0
SHIFT_MAX_BOUND = 55.0


def _params(sem):
    return pltpu.CompilerParams(dimension_semantics=sem, vmem_limit_bytes=VMEM_LIMIT)


def _tile(n, pref):
    t = min(n, pref)
    assert n % t == 0, (n, t)
    return t


def _silu(x):
    return x * jax.nn.sigmoid(x)


def _rms(x):
    return x * lax.rsqrt(jnp.mean(x * x, axis=-1, keepdims=True) + NORM_EPS)


def _modulate(x, gain, shift, scale):
    return _rms(x) * gain * (1.0 + scale) + shift


def _dot(a, b):
    return jnp.dot(a, b, preferred_element_type=F32)


def _dot_nt(a, b):
    return lax.dot_general(a, b, (((1,), (1,)), ((), ())), preferred_element_type=F32)


def _split3(a):
    hi = a.astype(BF16)
    r1 = a - hi.astype(F32)
    mid = r1.astype(BF16)
    lo = (r1 - mid.astype(F32)).astype(BF16)
    return hi, mid, lo


def _dot_f32(a, b):
    ah, am, al = _split3(a)
    bh, bm, bl = _split3(b)
    small = _dot(ah, bl) + _dot(al, bh) + _dot(am, bm)
    return small + (_dot(ah, bm) + _dot(am, bh)) + _dot(ah, bh)


def _adaln_kernel(c_ref, w_ref, b_ref, o_ref):
    a = _silu(c_ref[...]).astype(BF16)
    o_ref[...] = _dot(a, w_ref[...].astype(BF16)) + b_ref[...]


def _adaln(cond, w, b):
    R, D = cond.shape
    N = w.shape[1]
    tn = _tile(N, 1024)
    return pl.pallas_call(
        _adaln_kernel,
        grid=(N // tn,),
        in_specs=[pl.BlockSpec((R, D), lambda j: (0, 0)),
                  pl.BlockSpec((D, tn), lambda j: (0, j)),
                  pl.BlockSpec((1, tn), lambda j: (0, j))],
        out_specs=pl.BlockSpec((R, tn), lambda j: (0, j)),
        out_shape=jax.ShapeDtypeStruct((R, N), F32),
        compiler_params=_params(("arbitrary",)),
        name="adaln",
    )(cond, w, b.reshape(1, N))


EV_Z = 0
EV_XBC = EV_Z + SSD_INNER
EV_RQ = EV_XBC + SSD_CONV_CH
EV_RK = EV_RQ + RET_QK
EV_RV = EV_RK + RET_QK
EV_RG = EV_RV + RET_INNER
EV_DT = EV_RG + RET_INNER
EV_END = EV_DT + LANES


def _even_in_kernel(x_ref, mod_ref, gain_ref, w_ref, cos_ref, sin_ref,
                    z_ref, xbc_ref, dt_ref, rq_ref, rk_ref, rv_ref, rg_ref):
    mod = mod_ref[0]
    h = _modulate(x_ref[0], gain_ref[...], mod[0:1], mod[1:2]).astype(BF16)

    def mm(lo, hi):
        return _dot(h, w_ref[:, lo:hi])

    z_ref[0] = mm(EV_Z, EV_XBC)
    xbc_ref[0] = mm(EV_XBC, EV_RQ)
    cos = cos_ref[...]
    sin = sin_ref[...]
    k_scale = RET_QK_DIM ** -0.5
    for j in range(RET_HEADS):
        lo = j * RET_QK_DIM
        a = mm(EV_RQ + lo, EV_RQ + lo + RET_QK_DIM)
        rq_ref[0, :, lo:lo + RET_QK_DIM] = (a * cos + pltpu.roll(a, RET_QK_DIM // 2, 1) * sin).astype(BF16)
        a = mm(EV_RK + lo, EV_RK + lo + RET_QK_DIM)
        rk_ref[0, :, lo:lo + RET_QK_DIM] = (
            (a * cos + pltpu.roll(a, RET_QK_DIM // 2, 1) * sin) * k_scale).astype(BF16)
    rv_ref[0] = mm(EV_RV, EV_RG).astype(BF16)
    rg_ref[0] = mm(EV_RG, EV_DT)
    dt_ref[0] = mm(EV_DT, EV_END)


def _even_in(x, mod, gain, w, cosf, sinf, tm):
    B, S, D = x.shape
    tm = _tile(S, tm)
    bm = mod.shape[0]
    mod_map = (lambda b, i: (b, 0, 0)) if bm > 1 else (lambda b, i: (0, 0, 0))
    row = lambda n: pl.BlockSpec((1, tm, n), lambda b, i: (b, i, 0))
    outs = [(SSD_INNER, F32), (SSD_CONV_CH, F32), (LANES, F32), (RET_QK, BF16), (RET_QK, BF16),
            (RET_INNER, BF16), (RET_INNER, F32)]
    return pl.pallas_call(
        _even_in_kernel,
        grid=(B, S // tm),
        in_specs=[row(D),
                  pl.BlockSpec((1, 6, D), mod_map),
                  pl.BlockSpec((1, D), lambda b, i: (0, 0)),
                  pl.BlockSpec((D, EV_END), lambda b, i: (0, 0)),
                  pl.BlockSpec((tm, LANES), lambda b, i: (i, 0)),
                  pl.BlockSpec((tm, LANES), lambda b, i: (i, 0))],
        out_specs=[row(n) for n, _ in outs],
        out_shape=[jax.ShapeDtypeStruct((B, S, n), dt) for n, dt in outs],
        compiler_params=_params(("arbitrary", "arbitrary")),
        name="even_in",
    )(x, mod, gain, w, cosf, sinf)


def _ssd_prep_kernel(xbc_ref, prev_ref, next_ref, dt_ref, cw_ref, cb_ref, dtb_ref, alog_ref,
                     xs_ref, bm_ref, cm_ref, acol_ref, arow_ref, wrow_ref, *, tm, chunk):
    i = pl.program_id(1)
    n = pl.num_programs(1)
    x = xbc_ref[0]
    zero_row = jnp.zeros((1, x.shape[1]), F32)
    pr = jnp.where(i > 0, prev_ref[0][SUBLANES - 1:SUBLANES, :], zero_row)
    nx = jnp.where(i < n - 1, next_ref[0][0:1, :], zero_row)
    rows = lax.broadcasted_iota(jnp.int32, (tm, 1), 0)
    xp = jnp.where(rows == 0, pr, pltpu.roll(x, 1, 0))
    xn = jnp.where(rows == tm - 1, nx, pltpu.roll(x, tm - 1, 0))
    cw = cw_ref[...]
    y = _silu(xp * cw[0:1] + x * cw[1:2] + xn * cw[2:3] + cb_ref[...])
    xs_ref[0] = y[:, :SSD_INNER]
    bm_ref[0] = y[:, SSD_INNER:SSD_INNER + SSD_BC].astype(BF16)
    cm_ref[0] = y[:, SSD_INNER + SSD_BC:].astype(BF16)

    t = dt_ref[0] + dtb_ref[...]
    dt = jnp.maximum(t, 0.0) + jnp.log(1.0 + jnp.exp(-jnp.abs(t)))
    la = -dt * jnp.exp(alog_ref[...])
    rmod = rows % chunk
    fwd = la
    rev = la
    sh = 1
    while sh < chunk:
        fwd = fwd + jnp.where(rmod >= sh, pltpu.roll(fwd, sh, 0), 0.0)
        rev = rev + jnp.where(rmod < chunk - sh, pltpu.roll(rev, tm - sh, 0), 0.0)
        sh *= 2
    lane = lax.broadcasted_iota(jnp.int32, (1, LANES), 1)
    acc = jnp.where(lane < SSD_HEADS, fwd, rev)
    acc_t = acc.T
    dt_t = dt.T
    hp = SSD_HEADS // SSD_GROUPS
    for j in range(2 * SSD_GROUPS):
        acol_ref[0, j] = acc[:, j * hp:(j + 1) * hp]
        arow_ref[0, j] = acc_t[j * hp:(j + 1) * hp, :]
        wrow_ref[0, j] = dt_t[j * hp:(j + 1) * hp, :]


def _ssd_prep(xbc, dt, conv_w, conv_b, dtb, alog, tm):
    B, S, _ = xbc.shape
    tm = _tile(S, tm)
    assert tm % CHUNK == 0
    nh = S // SUBLANES
    hp = SSD_HEADS // SSD_GROUPS
    gd = 2 * SSD_GROUPS
    row = lambda n: pl.BlockSpec((1, tm, n), lambda b, i: (b, i, 0))
    const = lambda r, n: pl.BlockSpec((r, n), lambda b, i: (0, 0))
    return pl.pallas_call(
        functools.partial(_ssd_prep_kernel, tm=tm, chunk=CHUNK),
        grid=(B, S // tm),
        in_specs=[row(SSD_CONV_CH),
                  pl.BlockSpec((1, SUBLANES, SSD_CONV_CH),
                               lambda b, i: (b, jnp.maximum(i * (tm // SUBLANES) - 1, 0), 0)),
                  pl.BlockSpec((1, SUBLANES, SSD_CONV_CH),
                               lambda b, i: (b, jnp.minimum((i + 1) * (tm // SUBLANES), nh - 1), 0)),
                  row(LANES),
                  const(3, SSD_CONV_CH), const(1, SSD_CONV_CH), const(1, LANES), const(1, LANES)],
        out_specs=[row(SSD_INNER), row(SSD_BC), row(SSD_BC),
                   pl.BlockSpec((1, gd, tm, hp), lambda b, i: (b, 0, i, 0)),
                   pl.BlockSpec((1, gd, hp, tm), lambda b, i: (b, 0, 0, i)),
                   pl.BlockSpec((1, gd, hp, tm), lambda b, i: (b, 0, 0, i))],
        out_shape=[jax.ShapeDtypeStruct((B, S, SSD_INNER), F32),
                   jax.ShapeDtypeStruct((B, S, SSD_BC), BF16),
                   jax.ShapeDtypeStruct((B, S, SSD_BC), BF16),
                   jax.ShapeDtypeStruct((B, gd, S, hp), F32),
                   jax.ShapeDtypeStruct((B, gd, hp, S), F32),
                   jax.ShapeDtypeStruct((B, gd, hp, S), F32)],
        compiler_params=_params(("arbitrary", "arbitrary")),
        name="ssd_prep",
    )(xbc, xbc, xbc, dt, conv_w, conv_b, dtb, alog)


def _scan_direction(q_ref, k_ref, v_ref, ac_ref, ar_ref, wr_ref, y_ref, h_scr, d, *, G, R, P, N):
    reverse = d == 1
    C = q_ref.shape[1]
    li = lax.broadcasted_iota(jnp.int32, (C, C), 0)
    si = lax.broadcasted_iota(jnp.int32, (C, C), 1)
    mask = (si >= li) if reverse else (li >= si)
    far = 0 if reverse else C - 1
    width = max(P, LANES)
    heads_per = width // P
    lane_head = lax.broadcasted_iota(jnp.int32, (1, width), 1) // P
    RP = R * P
    for g in range(G):
        q = q_ref[0, :, g * N:(g + 1) * N]
        k = k_ref[0, :, g * N:(g + 1) * N]
        s = _dot_nt(q, k)
        k_t = k.astype(F32).T
        ac = ac_ref[0, g]
        ar = ar_ref[0, g]
        wr = wr_ref[0, g]
        for u in range(RP // width):
            lo = u * width
            v = v_ref[0, :, g * RP + lo:g * RP + lo + width].astype(BF16)
            h_old = h_scr[d, g, :, lo:lo + width]
            y_diag = None
            for j in range(heads_per):
                r = u * heads_per + j
                a_col = ac[:, r:r + 1]
                a_row = ar[r:r + 1, :]
                w_row = wr[r:r + 1, :]
                tot = a_row[:, far:far + 1]
                decay = jnp.exp(jnp.where(mask, a_col - a_row, NEG_BIG))
                m = (s * decay * w_row).astype(BF16)
                kw = (k_t * (jnp.exp(tot - a_row) * w_row)).astype(BF16)
                yd = _dot(m, v)
                st = _dot(kw, v)
                e_col = jnp.exp(a_col)
                e_tot = jnp.exp(tot)
                if y_diag is None:
                    y_diag, state, col_scale, tot_scale = yd, st, e_col, e_tot
                else:
                    sel = lane_head == j
                    y_diag = jnp.where(sel, yd, y_diag)
                    state = jnp.where(sel, st, state)
                    col_scale = jnp.where(sel, e_col, col_scale)
                    tot_scale = jnp.where(sel, e_tot, tot_scale)
            y_ref[0, :, g * RP + lo:g * RP + lo + width] = y_diag + col_scale * _dot(q, h_old.astype(BF16))
            h_scr[d, g, :, lo:lo + width] = tot_scale * h_old + state


def _scan_kernel(qf_ref, kf_ref, vf_ref, qb_ref, kb_ref, vb_ref, acf_ref, arf_ref, wrf_ref,
                 acb_ref, arb_ref, wrb_ref, h0f_ref, h0b_ref, yf_ref, yb_ref, hff_ref, hfb_ref, h_scr, **dims):
    c = pl.program_id(1)

    @pl.when(c == 0)
    def _():
        h_scr[0] = h0f_ref[0]
        h_scr[1] = h0b_ref[0]

    _scan_direction(qf_ref, kf_ref, vf_ref, acf_ref, arf_ref, wrf_ref, yf_ref, h_scr, 0, **dims)
    _scan_direction(qb_ref, kb_ref, vb_ref, acb_ref, arb_ref, wrb_ref, yb_ref, h_scr, 1, **dims)

    @pl.when(c == pl.num_programs(1) - 1)
    def _():
        hff_ref[0] = h_scr[0]
        hfb_ref[0] = h_scr[1]


def _bidir_scan(q, k, v, acol, arow, wrow, h0_f, h0_b, *, G, R, P, N):
    B, S, _ = q.shape
    C = _tile(S, CHUNK)
    nc = S // C
    RP = R * P
    bx = (lambda b: b) if acol.shape[0] > 1 else (lambda b: 0)
    fwd = lambda c: c
    rev = lambda c: nc - 1 - c
    seq = lambda n, ci: pl.BlockSpec((1, C, n), lambda b, c: (b, ci(c), 0))
    col = lambda d, ci: pl.BlockSpec((1, G, C, R), lambda b, c: (bx(b), d, ci(c), 0))
    rowt = lambda d, ci: pl.BlockSpec((1, G, R, C), lambda b, c: (bx(b), d, 0, ci(c)))
    state = pl.BlockSpec((1, G, N, RP), lambda b, c: (b, 0, 0, 0))
    return pl.pallas_call(
        functools.partial(_scan_kernel, G=G, R=R, P=P, N=N),
        grid=(B, nc),
        in_specs=[seq(G * N, fwd), seq(G * N, fwd), seq(G * RP, fwd),
                  seq(G * N, rev), seq(G * N, rev), seq(G * RP, rev),
                  col(0, fwd), rowt(0, fwd), rowt(0, fwd), col(1, rev), rowt(1, rev), rowt(1, rev),
                  state, state],
        out_specs=[seq(G * RP, fwd), seq(G * RP, rev), state, state],
        out_shape=[jax.ShapeDtypeStruct((B, S, G * RP), F32), jax.ShapeDtypeStruct((B, S, G * RP), F32),
                   jax.ShapeDtypeStruct((B, G, N, RP), F32), jax.ShapeDtypeStruct((B, G, N, RP), F32)],
        scratch_shapes=[pltpu.VMEM((2, G, N, RP), F32)],
        compiler_params=_params(("arbitrary", "arbitrary")),
        name="bidir_scan",
    )(q, k, v, q, k, v, acol, arow, wrow, acol, arow, wrow, h0_f, h0_b)


def _even_out_kernel(ysf_ref, ysb_ref, xs_ref, z_ref, yrf_ref, yrb_ref, rg_ref, x_ref, mod_ref, dvec_ref, sn_ref,
                     w_ref, o_ref):
    y = ysf_ref[0] + ysb_ref[0] + xs_ref[0] * dvec_ref[...]
    y = _rms(y * _silu(z_ref[0])) * sn_ref[...]
    out = _dot(y.astype(BF16), w_ref[0:SSD_INNER, :])
    yr = yrf_ref[0] + yrb_ref[0]
    rg = _silu(rg_ref[0])
    for j in range(RET_HEADS):
        lo = j * RET_V_DIM
        r = _rms(yr[:, lo:lo + RET_V_DIM]) * rg[:, lo:lo + RET_V_DIM]
        out = out + _dot(r.astype(BF16), w_ref[SSD_INNER + lo:SSD_INNER + lo + RET_V_DIM, :])
    o_ref[0] = x_ref[0] + mod_ref[0][2:3] * out


def _even_out(ysf, ysb, xs, z, yrf, yrb, rg, x, mod, dvec, ssd_norm, w, tm):
    B, S, D = x.shape
    tm = _tile(S, tm)
    bm = mod.shape[0]
    mod_map = (lambda b, i: (b, 0, 0)) if bm > 1 else (lambda b, i: (0, 0, 0))
    row = lambda n: pl.BlockSpec((1, tm, n), lambda b, i: (b, i, 0))
    const = lambda r, n: pl.BlockSpec((r, n), lambda b, i: (0, 0))
    return pl.pallas_call(
        _even_out_kernel,
        grid=(B, S // tm),
        in_specs=[row(SSD_INNER), row(SSD_INNER), row(SSD_INNER), row(SSD_INNER),
                  row(RET_INNER), row(RET_INNER), row(RET_INNER), row(D),
                  pl.BlockSpec((1, 6, D), mod_map),
                  const(1, SSD_INNER), const(1, SSD_INNER), const(SSD_INNER + RET_INNER, D)],
        out_specs=row(D),
        out_shape=jax.ShapeDtypeStruct((B, S, D), F32),
        compiler_params=_params(("arbitrary", "arbitrary")),
        name="even_out",
    )(ysf, ysb, xs, z, yrf, yrb, rg, x, mod, dvec, ssd_norm, w)


def _ffn_kernel(x_ref, mod_ref, gain_ref, wg_ref, wu_ref, wd_ref, o_ref, h_scr, acc_scr):
    f = pl.program_id(2)
    mod = mod_ref[0]

    @pl.when(f == 0)
    def _():
        h_scr[...] = _modulate(x_ref[0], gain_ref[...], mod[3:4], mod[4:5]).astype(BF16)
        acc_scr[...] = jnp.zeros_like(acc_scr)

    h = h_scr[...]
    a = (_silu(_dot(h, wg_ref[...])) * _dot(h, wu_ref[...])).astype(BF16)
    acc_scr[...] += _dot(a, wd_ref[...])

    @pl.when(f == pl.num_programs(2) - 1)
    def _():
        o_ref[0] = x_ref[0] + mod[5:6] * acc_scr[...]


def _ffn(x, mod, gain, wg, wu, wd, tm, tf):
    B, S, D = x.shape
    F = wg.shape[1]
    tm = _tile(S, tm)
    tf = _tile(F, tf)
    bm = mod.shape[0]
    mod_map = (lambda b, i, f: (b, 0, 0)) if bm > 1 else (lambda b, i, f: (0, 0, 0))
    return pl.pallas_call(
        _ffn_kernel,
        grid=(B, S // tm, F // tf),
        in_specs=[pl.BlockSpec((1, tm, D), lambda b, i, f: (b, i, 0)),
                  pl.BlockSpec((1, 6, D), mod_map),
                  pl.BlockSpec((1, D), lambda b, i, f: (0, 0)),
                  pl.BlockSpec((D, tf), lambda b, i, f: (0, f)),
                  pl.BlockSpec((D, tf), lambda b, i, f: (0, f)),
                  pl.BlockSpec((tf, D), lambda b, i, f: (f, 0))],
        out_specs=pl.BlockSpec((1, tm, D), lambda b, i, f: (b, i, 0)),
        out_shape=jax.ShapeDtypeStruct((B, S, D), F32),
        scratch_shapes=[pltpu.VMEM((tm, D), BF16), pltpu.VMEM((tm, D), F32)],
        compiler_params=_params(("arbitrary", "arbitrary", "arbitrary")),
        name="ffn_dense",
    )(x, mod, gain, wg, wu, wd)


def _odd_in_kernel(x_ref, mod_ref, gain_ref, w_ref, qn_ref, kn_ref, bd_ref, cos_ref, sin_ref,
                   q_ref, k_ref, v_ref):
    mod = mod_ref[0]
    h = _modulate(x_ref[0], gain_ref[...], mod[0:1], mod[1:2]).astype(BF16)
    cos = cos_ref[...]
    sin = sin_ref[...]
    bd = bd_ref[...]
    lane = lax.broadcasted_iota(jnp.int32, (1, LANES), 1)
    quarter = DIFF_HEAD_DIM // 4
    first = (lane % (2 * quarter)) < quarter
    for base, n_ref, o_ref, scale in ((0, qn_ref, q_ref, DIFF_HEAD_DIM ** -0.5),
                                      (DIFF_INNER, kn_ref, k_ref, 1.0)):
        for j in range(DIFF_INNER // LANES):
            lo = j * LANES
            a = _dot(h, w_ref[:, base + lo:base + lo + LANES])
            sq = a * a
            hi = sq.astype(BF16)
            rest = (sq - hi.astype(F32)).astype(BF16)
            ms = (_dot(hi, bd) + _dot(rest, bd)) * (1.0 / DIFF_HEAD_DIM)
            a = a * lax.rsqrt(ms + NORM_EPS) * n_ref[...]
            partner = jnp.where(first, pltpu.roll(a, LANES - quarter, 1), pltpu.roll(a, quarter, 1))
            o_ref[0, :, lo:lo + LANES] = ((a * cos + partner * sin) * scale).astype(BF16)
    v_ref[0] = _dot(h, w_ref[:, 2 * DIFF_INNER:3 * DIFF_INNER]).astype(BF16)


def _odd_in(x, mod, gain, w, qn, kn, bd, cosf, sinf, tm):
    B, S, D = x.shape
    tm = _tile(S, tm)
    bm = mod.shape[0]
    mod_map = (lambda b, i: (b, 0, 0)) if bm > 1 else (lambda b, i: (0, 0, 0))
    row = lambda n: pl.BlockSpec((1, tm, n), lambda b, i: (b, i, 0))
    const = lambda r, n: pl.BlockSpec((r, n), lambda b, i: (0, 0))
    return pl.pallas_call(
        _odd_in_kernel,
        grid=(B, S // tm),
        in_specs=[row(D), pl.BlockSpec((1, 6, D), mod_map), const(1, D), const(D, 3 * DIFF_INNER),
                  const(1, LANES), const(1, LANES), const(LANES, LANES),
                  pl.BlockSpec((tm, LANES), lambda b, i: (i, 0)),
                  pl.BlockSpec((tm, LANES), lambda b, i: (i, 0))],
        out_specs=[row(DIFF_INNER)] * 3,
        out_shape=[jax.ShapeDtypeStruct((B, S, DIFF_INNER), BF16)] * 3,
        compiler_params=_params(("arbitrary", "arbitrary")),
        name="odd_in",
    )(x, mod, gain, w, qn, kn, bd, cosf, sinf)


def _attn_kernel(sc_ref, q_ref, kl_ref, vl_ref, kc_ref, vc_ref, sub_ref, o_ref,
                 q2_scr, m_scr, l_scr, acc_scr, *, tq, tk, out_scale):
    q = q_ref[0]
    lane = lax.broadcasted_iota(jnp.int32, (1, LANES), 1)
    lo_half = lane < DIFF_HEAD_DIM
    zero = jnp.zeros_like(q)
    q2_scr[0:tq, :] = jnp.where(lo_half, q, zero)
    q2_scr[tq:2 * tq, :] = jnp.where(lo_half, zero, q)
    l_scr[...] = jnp.zeros_like(l_scr)
    acc_scr[...] = jnp.zeros_like(acc_scr)
    n_lat = kl_ref.shape[1] // tk
    use_shift = sc_ref[0, 2] > 0.0

    def over_keys(update):
        def body(j, carry):
            start = pl.multiple_of(j * tk, tk)
            update(kl_ref[0, pl.ds(start, tk), :], vl_ref[0, pl.ds(start, tk), :])
            return carry

        lax.fori_loop(0, n_lat, body, 0)
        update(kc_ref[0], vc_ref[0])

    @pl.when(use_shift)
    def _():
        shift = sc_ref[0, 1]

        def update(k, v):
            p = jnp.exp(_dot_nt(q2_scr[...], k) - shift)
            part = p[:, 0:LANES]
            for c in range(1, p.shape[1] // LANES):
                part = part + p[:, c * LANES:(c + 1) * LANES]
            l_scr[...] += part
            acc_scr[...] += _dot(p.astype(BF16), v)

        over_keys(update)
        acc_scr[...] = acc_scr[...] / jnp.sum(l_scr[...], axis=1, keepdims=True)

    @pl.when(jnp.logical_not(use_shift))
    def _():
        m_scr[...] = jnp.full_like(m_scr, -jnp.inf)

        def update(k, v):
            s = _dot_nt(q2_scr[...], k)
            m_prev = m_scr[...]
            m_new = jnp.maximum(m_prev, jnp.max(s, axis=1, keepdims=True))
            alpha = jnp.exp(m_prev - m_new)
            p = jnp.exp(s - m_new[:, 0:1])
            l_scr[...] = alpha * l_scr[...] + jnp.sum(p, axis=1, keepdims=True)
            acc_scr[...] = alpha * acc_scr[...] + _dot(p.astype(BF16), v)
            m_scr[...] = m_new

        over_keys(update)
        acc_scr[...] = acc_scr[...] / l_scr[...]

    o = acc_scr[0:tq, :] - sc_ref[0, 0] * acc_scr[tq:2 * tq, :]
    o_ref[0] = (_rms(o) * sub_ref[...] * out_scale).astype(BF16)


def _attention(lam, q, k_lat, v_lat, k_ctx, v_ctx, subln, out_scale, tq, tk):
    B, L, _ = q.shape
    Lc = k_ctx.shape[1]
    tq = _tile(L, tq)
    tk = _tile(L, tk)
    head = lambda n: pl.BlockSpec((1, n, LANES), lambda b, h, i: (b, 0, h))
    return pl.pallas_call(
        functools.partial(_attn_kernel, tq=tq, tk=tk, out_scale=out_scale),
        grid=(B, DIFF_HEADS, L // tq),
        in_specs=[pl.BlockSpec(memory_space=pltpu.SMEM),
                  pl.BlockSpec((1, tq, LANES), lambda b, h, i: (b, i, h)),
                  head(L), head(L), head(Lc), head(Lc),
                  pl.BlockSpec((1, LANES), lambda b, h, i: (0, 0))],
        out_specs=pl.BlockSpec((1, tq, LANES), lambda b, h, i: (b, i, h)),
        out_shape=jax.ShapeDtypeStruct((B, L, DIFF_INNER), BF16),
        scratch_shapes=[pltpu.VMEM((2 * tq, LANES), BF16), pltpu.VMEM((2 * tq, LANES), F32),
                        pltpu.VMEM((2 * tq, LANES), F32), pltpu.VMEM((2 * tq, LANES), F32)],
        compiler_params=_params(("arbitrary", "arbitrary", "arbitrary")),
        name="diff_attn",
    )(lam, q, k_lat, v_lat, k_ctx, v_ctx, subln)


def _odd_out_kernel(o_ref, x_ref, mod_ref, gain_ref, w_ref, wr_ref, xo_ref, h_ref, route_ref):
    mod = mod_ref[0]
    x1 = x_ref[0] + mod[2:3] * _dot(o_ref[0], w_ref[...])
    xo_ref[0] = x1
    h = _modulate(x1, gain_ref[...], mod[3:4], mod[4:5])
    h_ref[0] = h
    logits = _dot_f32(h, wr_ref[...])
    lane = lax.broadcasted_iota(jnp.int32, logits.shape, 1)
    lg = jnp.where(lane < N_EXPERTS, logits, -jnp.inf)
    v1 = jnp.max(lg, axis=1, keepdims=True)
    i1 = jnp.min(jnp.where(lg == v1, lane, LANES), axis=1, keepdims=True)
    lg2 = jnp.where(lane == i1, -jnp.inf, lg)
    v2 = jnp.max(lg2, axis=1, keepdims=True)
    i2 = jnp.min(jnp.where(lg2 == v2, lane, LANES), axis=1, keepdims=True)
    e = jnp.exp(v2 - v1)
    w1 = 1.0 / (1.0 + e)
    w2 = e * w1
    route = jnp.where(lane == 0, i1.astype(F32),
                      jnp.where(lane == 1, i2.astype(F32),
                                jnp.where(lane == 2, w1, jnp.where(lane == 3, w2, 0.0))))
    route_ref[0] = route


def _odd_out(o, x, mod, gain, w, w_router, tm):
    B, S, D = x.shape
    tm = _tile(S, tm)
    row = lambda n: pl.BlockSpec((1, tm, n), lambda b, i: (b, i, 0))
    const = lambda r, n: pl.BlockSpec((r, n), lambda b, i: (0, 0))
    return pl.pallas_call(
        _odd_out_kernel,
        grid=(B, S // tm),
        in_specs=[row(DIFF_INNER), row(D), pl.BlockSpec((1, 6, D), lambda b, i: (b, 0, 0)),
                  const(1, D), const(DIFF_INNER, D), const(D, LANES)],
        out_specs=[row(D), row(D), row(LANES)],
        out_shape=[jax.ShapeDtypeStruct((B, S, D), F32), jax.ShapeDtypeStruct((B, S, D), F32),
                   jax.ShapeDtypeStruct((B, S, LANES), F32)],
        compiler_params=_params(("arbitrary", "arbitrary")),
        name="odd_out",
    )(o, x, mod, gain, w, w_router)


def _row_copy(src, src_row, dst, dst_row, sem):
    return pltpu.make_async_copy(src.at[pl.ds(src_row, 1), :], dst.at[pl.ds(dst_row, 1), :], sem)


def _dispatch_kernel(slot_ref, h_ref, hs_in_ref, hs_ref, sem, *, tt):
    del hs_in_ref

    def issue(r, carry):
        _row_copy(h_ref, r, hs_ref, slot_ref[0, 0, 2 * r], sem).start()
        _row_copy(h_ref, r, hs_ref, slot_ref[0, 0, 2 * r + 1], sem).start()
        return carry

    lax.fori_loop(0, tt, issue, 0)

    def drain(r, carry):
        _row_copy(h_ref, r, hs_ref, slot_ref[0, 0, 2 * r], sem).wait()
        _row_copy(h_ref, r, hs_ref, slot_ref[0, 0, 2 * r + 1], sem).wait()
        return carry

    lax.fori_loop(0, tt, drain, 0)


def _dispatch(slots, h, n_rows, tt):
    T, D = h.shape
    tt = _tile(T, tt)
    hs0 = jnp.zeros((n_rows, D), F32)
    return pl.pallas_call(
        functools.partial(_dispatch_kernel, tt=tt),
        grid=(T // tt,),
        in_specs=[pl.BlockSpec((1, 1, 2 * tt), lambda i: (i, 0, 0), memory_space=pltpu.SMEM),
                  pl.BlockSpec((tt, D), lambda i: (i, 0)),
                  pl.BlockSpec(memory_space=pl.ANY)],
        out_specs=pl.BlockSpec(memory_space=pl.ANY),
        out_shape=jax.ShapeDtypeStruct((n_rows, D), F32),
        scratch_shapes=[pltpu.SemaphoreType.DMA(())],
        input_output_aliases={2: 0},
        compiler_params=_params(("arbitrary",)),
        name="moe_dispatch",
    )(slots.reshape(T // tt, 1, 2 * tt), h, hs0)


def _expert_kernel(te_ref, tv_ref, hs_ref, wg_ref, wu_ref, wd_ref, ys_ref, h_scr, acc_scr):
    i = pl.program_id(0)
    f = pl.program_id(1)

    @pl.when(tv_ref[i] > 0)
    def _():
        @pl.when(f == 0)
        def _():
            h_scr[...] = hs_ref[...].astype(BF16)
            acc_scr[...] = jnp.zeros_like(acc_scr)

        h = h_scr[...]
        g = _dot(h, wg_ref[0].astype(BF16))
        u = _dot(h, wu_ref[0].astype(BF16))
        acc_scr[...] += _dot((_silu(g) * u).astype(BF16), wd_ref[0].astype(BF16))

        @pl.when(f == pl.num_programs(1) - 1)
        def _():
            ys_ref[...] = acc_scr[...]

    @pl.when((tv_ref[i] == 0) & (f == pl.num_programs(1) - 1))
    def _():
        ys_ref[...] = jnp.zeros_like(ys_ref)


def _experts(tile_expert, tile_valid, hs, wg, wu, wd, tm, tf):
    n_rows, D = hs.shape
    F = wg.shape[2]
    tf = _tile(F, tf)
    nf = F // tf
    fi = lambda i, f, te, tv: jnp.where(tv[i] > 0, f, nf - 1)
    grid_spec = pltpu.PrefetchScalarGridSpec(
        num_scalar_prefetch=2,
        grid=(n_rows // tm, nf),
        in_specs=[pl.BlockSpec((tm, D), lambda i, f, te, tv: (i, 0)),
                  pl.BlockSpec((1, D, tf), lambda i, f, te, tv: (te[i], 0, fi(i, f, te, tv))),
                  pl.BlockSpec((1, D, tf), lambda i, f, te, tv: (te[i], 0, fi(i, f, te, tv))),
                  pl.BlockSpec((1, tf, D), lambda i, f, te, tv: (te[i], fi(i, f, te, tv), 0))],
        out_specs=pl.BlockSpec((tm, D), lambda i, f, te, tv: (i, 0)),
        scratch_shapes=[pltpu.VMEM((tm, D), BF16), pltpu.VMEM((tm, D), F32)])
    return pl.pallas_call(
        _expert_kernel,
        grid_spec=grid_spec,
        out_shape=jax.ShapeDtypeStruct((n_rows, D), F32),
        compiler_params=_params(("arbitrary", "arbitrary")),
        name="moe_experts",
    )(tile_expert, tile_valid, hs, wg, wu, wd)


def _combine_kernel(slot_ref, ys_ref, x_ref, mod_ref, route_ref, o_ref, buf0, buf1, sem, *, tt):
    def issue(r, carry):
        _row_copy(ys_ref, slot_ref[0, 0, 2 * r], buf0, r, sem).start()
        _row_copy(ys_ref, slot_ref[0, 0, 2 * r + 1], buf1, r, sem).start()
        return carry

    lax.fori_loop(0, tt, issue, 0)

    def drain(r, carry):
        _row_copy(ys_ref, slot_ref[0, 0, 2 * r], buf0, r, sem).wait()
        _row_copy(ys_ref, slot_ref[0, 0, 2 * r + 1], buf1, r, sem).wait()
        return carry

    lax.fori_loop(0, tt, drain, 0)
    route = route_ref[0]
    y = route[:, 2:3] * buf0[...] + route[:, 3:4] * buf1[...]
    o_ref[0] = x_ref[0] + mod_ref[0][5:6] * y


def _combine(slots, ys, x, mod, route, tt):
    B, S, D = x.shape
    tt = _tile(S, tt)
    n = S // tt
    return pl.pallas_call(
        functools.partial(_combine_kernel, tt=tt),
        grid=(B, n),
        in_specs=[pl.BlockSpec((1, 1, 2 * tt), lambda b, i: (b * n + i, 0, 0), memory_space=pltpu.SMEM),
                  pl.BlockSpec(memory_space=pl.ANY),
                  pl.BlockSpec((1, tt, D), lambda b, i: (b, i, 0)),
                  pl.BlockSpec((1, 6, D), lambda b, i: (b, 0, 0)),
                  pl.BlockSpec((1, tt, LANES), lambda b, i: (b, i, 0))],
        out_specs=pl.BlockSpec((1, tt, D), lambda b, i: (b, i, 0)),
        out_shape=jax.ShapeDtypeStruct((B, S, D), F32),
        scratch_shapes=[pltpu.VMEM((tt, D), F32), pltpu.VMEM((tt, D), F32), pltpu.SemaphoreType.DMA(())],
        compiler_params=_params(("arbitrary", "arbitrary")),
        name="moe_combine",
    )(slots.reshape(B * n, 1, 2 * tt), ys, x, mod, route)


def _routing_tables(route, tm):
    T = route.shape[0]
    experts = route[:, 0:2].astype(jnp.int32).reshape(2 * T)
    onehot = (experts[:, None] == jnp.arange(N_EXPERTS, dtype=jnp.int32)[None, :]).astype(jnp.int32)
    rank = jnp.sum((jnp.cumsum(onehot, axis=0) - onehot) * onehot, axis=1)
    counts = jnp.sum(onehot, axis=0)
    tiles = (counts + tm - 1) // tm
    tile_end = jnp.cumsum(tiles)
    offsets = (tile_end - tiles) * tm
    slots = jnp.sum(onehot * offsets[None, :], axis=1) + rank
    n_tiles = (2 * T) // tm + N_EXPERTS
    ids = jnp.arange(n_tiles, dtype=jnp.int32)
    tile_expert = jnp.minimum(jnp.sum((ids[:, None] >= tile_end[None, :]).astype(jnp.int32), axis=1),
                              N_EXPERTS - 1)
    last_used = jnp.minimum(jnp.sum((tile_end[-1] - 1 >= tile_end).astype(jnp.int32)), N_EXPERTS - 1)
    tile_valid = (ids < tile_end[-1]).astype(jnp.int32)
    tile_expert = jnp.where(tile_valid > 0, tile_expert, last_used)
    return slots.astype(jnp.int32), tile_expert.astype(jnp.int32), tile_valid, n_tiles


def _rope_angles(pos, dim):
    inv = ROPE_THETA ** (-jnp.arange(dim // 2, dtype=F32) / (dim // 2))
    return pos.astype(F32)[:, None] * inv[None, :]


def _ret_rope_tables(pos):
    ang = _rope_angles(pos, RET_QK_DIM)
    cos, sin = jnp.cos(ang), jnp.sin(ang)
    return jnp.concatenate([cos, cos], axis=1), jnp.concatenate([-sin, sin], axis=1)


def _axial_rope_tables(L):
    n_rows = L // GRID_W
    row = jnp.broadcast_to(jnp.arange(n_rows)[:, None], (n_rows, GRID_W)).reshape(-1)
    col = jnp.broadcast_to(jnp.arange(GRID_W)[None, :], (n_rows, GRID_W)).reshape(-1)
    ar = _rope_angles(row, DIFF_HEAD_DIM // 2)
    ac = _rope_angles(col, DIFF_HEAD_DIM // 2)
    cos = jnp.concatenate([jnp.cos(ar), jnp.cos(ar), jnp.cos(ac), jnp.cos(ac)], axis=1)
    sin = jnp.concatenate([-jnp.sin(ar), jnp.sin(ar), -jnp.sin(ac), jnp.sin(ac)], axis=1)
    return jnp.tile(cos, (1, LANES // DIFF_HEAD_DIM)), jnp.tile(sin, (1, LANES // DIFF_HEAD_DIM))


def _ret_decay_tables(ret_decay, S):
    C = min(S, CHUNK)
    lam = -jnp.exp(ret_decay.astype(F32))
    pos = jnp.arange(S) % C
    steps = jnp.stack([pos + 1, C - pos]).astype(F32)
    a = (lam[:, :, None] * steps[:, None, :]).reshape(1, 2 * RET_HEADS, S)
    return a[:, :, :, None], a[:, :, None, :], jnp.ones((1, 2 * RET_HEADS, 1, S), F32)


def kernel(x, c, ctx, c_ctx, even_w_mod, even_b_mod, even_norm1, even_norm2, even_w_in, even_conv_w, even_conv_b, even_dt_bias, even_a_log, even_d, even_ssd_norm, even_ret_decay, even_w_out, even_ffn_gate, even_ffn_up, even_ffn_down, odd_w_mod, odd_b_mod, odd_norm1, odd_norm2, odd_w_in, odd_q_norm, odd_k_norm, odd_lambda, odd_subln, odd_w_out, odd_router, odd_exp_gate, odd_exp_up, odd_exp_down):
    B, L, D = x.shape
    Lc = ctx.shape[1]
    T = B * L
    row = lambda v: v.reshape(1, -1)
    pad_lanes = lambda v: jnp.pad(v.reshape(1, -1), ((0, 0), (0, LANES - v.size)))

    cond = jnp.concatenate([c, c_ctx[None, :], jnp.zeros((SUBLANES - B - 1, D), F32)], axis=0)

    mod = _adaln(cond, even_w_mod[0], even_b_mod[0]).reshape(SUBLANES, 6, D)
    mod_l, mod_c = mod[:B], mod[B:B + 1]
    w_in = even_w_in[0]
    cut = SSD_INNER + SSD_CONV_CH
    w_in = jnp.concatenate([w_in[:, :cut], w_in[:, cut + 2 * SSD_HEADS:], w_in[:, cut:cut + 2 * SSD_HEADS],
                            jnp.zeros((D, LANES - 2 * SSD_HEADS), F32)], axis=1).astype(BF16)
    g1 = row(even_norm1[0])
    cos_c, sin_c = _ret_rope_tables(jnp.arange(Lc))
    cos_l, sin_l = _ret_rope_tables(Lc + jnp.arange(L))
    dtb = pad_lanes(even_dt_bias[0])
    alog = pad_lanes(even_a_log[0])
    conv_b = row(even_conv_b[0])
    ssd_kw = dict(G=SSD_GROUPS, R=SSD_HEADS // SSD_GROUPS, P=SSD_HEAD_DIM, N=SSD_STATE)
    ret_kw = dict(G=RET_HEADS, R=1, P=RET_V_DIM, N=RET_QK_DIM)

    def mix(xin, modv, cosf, sinf, hs, hr, tm):
        z, xbc, dt, rq, rk, rv, rg = _even_in(xin, modv, g1, w_in, cosf, sinf, tm)
        xs, bmat, cmat, acol, arow, wrow = _ssd_prep(xbc, dt, even_conv_w[0], conv_b, dtb, alog, tm)
        ysf, ysb, hs_f, hs_b = _bidir_scan(cmat, bmat, xs, acol, arow, wrow, hs[0], hs[1], **ssd_kw)
        racol, rarow, rwrow = _ret_decay_tables(even_ret_decay[0], xin.shape[1])
        yrf, yrb, hr_f, hr_b = _bidir_scan(rq, rk, rv, racol, rarow, rwrow, hr[0], hr[1], **ret_kw)
        return (ysf, ysb, xs, z, yrf, yrb, rg), (hs_f, hs_b), (hr_f, hr_b)

    zs = jnp.zeros((B, SSD_GROUPS, SSD_STATE, SSD_INNER // SSD_GROUPS), F32)
    zr = jnp.zeros((B, RET_HEADS, RET_QK_DIM, RET_V_DIM), F32)
    tm_c = 256
    tm_l = 512
    mixed_c, hs, hr = mix(ctx, mod_c, cos_c, sin_c, (zs, zs), (zr, zr), tm_c)
    mixed_l, _, _ = mix(x, mod_l, cos_l, sin_l, hs, hr, tm_l)

    dvec = row(jnp.repeat(even_d[0], SSD_HEAD_DIM))
    snorm = row(even_ssd_norm[0])
    w_out = even_w_out[0].astype(BF16)
    g2 = row(even_norm2[0])
    wg = even_ffn_gate[0].astype(BF16)
    wu = even_ffn_up[0].astype(BF16)
    wd = even_ffn_down[0].astype(BF16)
    ffn_tf = 1408

    xl = _even_out(*mixed_l, x, mod_l, dvec, snorm, w_out, 256)
    xl = _ffn(xl, mod_l, g2, wg, wu, wd, tm_l, ffn_tf)
    xc = _even_out(*mixed_c, ctx, mod_c, dvec, snorm, w_out, tm_c)
    xc = _ffn(xc, mod_c, g2, wg, wu, wd, tm_c, ffn_tf)

    mod = _adaln(cond, odd_w_mod[0], odd_b_mod[0]).reshape(SUBLANES, 6, D)
    mod_l, mod_c = mod[:B], mod[B:B + 1]
    lam_init = 0.8 - 0.6 * math.exp(-0.3 * 1)
    lq1, lk1, lq2, lk2 = odd_lambda[0]
    lam = jnp.exp(jnp.sum(lq1 * lk1)) - jnp.exp(jnp.sum(lq2 * lk2)) + lam_init
    bound = 1.02 * DIFF_HEAD_DIM ** 0.5 * jnp.max(jnp.abs(odd_q_norm[0])) * jnp.max(jnp.abs(odd_k_norm[0]))
    lam = jnp.stack([lam, bound - SHIFT_HEADROOM, (bound <= SHIFT_MAX_BOUND).astype(F32),
                     jnp.zeros((), F32)]).reshape(1, 4)
    w_in = odd_w_in[0].astype(BF16)
    g1 = row(odd_norm1[0])
    qn = row(jnp.tile(odd_q_norm[0], LANES // DIFF_HEAD_DIM))
    kn = row(jnp.tile(odd_k_norm[0], LANES // DIFF_HEAD_DIM))
    gid = jnp.arange(LANES) // DIFF_HEAD_DIM
    bd = (gid[:, None] == gid[None, :]).astype(BF16)
    cos_l, sin_l = _axial_rope_tables(L)
    cos_c, sin_c = jnp.ones((Lc, LANES), F32), jnp.zeros((Lc, LANES), F32)
    q_l, k_l, v_l = _odd_in(xl, mod_l, g1, w_in, qn, kn, bd, cos_l, sin_l, tm_l)
    _, k_c, v_c = _odd_in(xc, mod_c, g1, w_in, qn, kn, bd, cos_c, sin_c, tm_c)
    o = _attention(lam, q_l, k_l, v_l, k_c, v_c, row(odd_subln[0]), 1.0 - lam_init, 512, 512)

    w_router = jnp.pad(odd_router[0], ((0, 0), (0, LANES - N_EXPERTS)))
    xl, h, route = _odd_out(o, xl, mod_l, row(odd_norm2[0]), odd_w_out[0].astype(BF16), w_router, tm_l)
    moe_tm = min(1024, T)
    slots, tile_expert, tile_valid, n_tiles = _routing_tables(route.reshape(T, LANES), moe_tm)
    hs_sorted = _dispatch(slots, h.reshape(T, D), n_tiles * moe_tm, 512)
    ys_sorted = _experts(tile_expert, tile_valid, hs_sorted, odd_exp_gate[0], odd_exp_up[0], odd_exp_down[0],
                         moe_tm, 512)
    return _combine(slots, ys_sorted, xl, mod_l, route, 256)
```

```python
import functools
import math

import jax
import jax.numpy as jnp
from jax import lax
from jax.experimental import pallas as pl
from jax.experimental.pallas import tpu as pltpu

F32 = jnp.float32
BF16 = jnp.bfloat16
LANES = 128
SUBLANES = 8
MXU_N = 256
VMEM_LIMIT = 56 * 1024 * 1024

GRID_W = 64
CHUNK = 128
NORM_EPS = 1e-6
ROPE_THETA = 10000.0
SSD_HEADS = 16
SSD_HEAD_DIM = 64
SSD_INNER = SSD_HEADS * SSD_HEAD_DIM
SSD_GROUPS = 2
SSD_STATE = 128
SSD_BC = SSD_GROUPS * SSD_STATE
SSD_CONV_CH = SSD_INNER + 2 * SSD_BC
RET_HEADS = 4
RET_QK_DIM = 128
RET_V_DIM = 256
RET_QK = RET_HEADS * RET_QK_DIM
RET_INNER = RET_HEADS * RET_V_DIM
DIFF_HEADS = 8
DIFF_HEAD_DIM = 64
DIFF_V_DIM = 2 * DIFF_HEAD_DIM
DIFF_INNER = DIFF_HEADS * DIFF_V_DIM
N_EXPERTS = 8
NEG_BIG = -1e30
SHIFT_HEADROOM = 30.0
SHIFT_MAX_BOUND = 55.0
DMA_ISSUE_UNROLL = 8


def _params(sem):
    return pltpu.CompilerParams(dimension_semantics=sem, vmem_limit_bytes=VMEM_LIMIT)


def _tile(n, pref):
    t = min(n, pref)
    assert n % t == 0, (n, t)
    return t


def _silu(x):
    return x * jax.nn.sigmoid(x)


def _rms(x):
    return x * lax.rsqrt(jnp.mean(x * x, axis=-1, keepdims=True) + NORM_EPS)


def _modulate(x, gain, shift, scale):
    return _rms(x) * gain * (1.0 + scale) + shift


def _dot(a, b):
    return jnp.dot(a, b, preferred_element_type=F32)


def _dot_nt(a, b):
    return lax.dot_general(a, b, (((1,), (1,)), ((), ())), preferred_element_type=F32)


def _split3(a):
    hi = a.astype(BF16)
    r1 = a - hi.astype(F32)
    mid = r1.astype(BF16)
    lo = (r1 - mid.astype(F32)).astype(BF16)
    return hi, mid, lo


def _dot_f32(a, b):
    ah, am, al = _split3(a)
    bh, bm, bl = _split3(b)
    small = _dot(ah, bl) + _dot(al, bh) + _dot(am, bm)
    return small + (_dot(ah, bm) + _dot(am, bh)) + _dot(ah, bh)


def _adaln_kernel(c_ref, w_ref, b_ref, o_ref):
    a = _silu(c_ref[...]).astype(BF16)
    o_ref[...] = _dot(a, w_ref[...].astype(BF16)) + b_ref[...]


def _adaln(cond, w, b):
    R, D = cond.shape
    N = w.shape[1]
    tn = _tile(N, 1024)
    return pl.pallas_call(
        _adaln_kernel,
        grid=(N // tn,),
        in_specs=[pl.BlockSpec((R, D), lambda j: (0, 0)),
                  pl.BlockSpec((D, tn), lambda j: (0, j)),
                  pl.BlockSpec((1, tn), lambda j: (0, j))],
        out_specs=pl.BlockSpec((R, tn), lambda j: (0, j)),
        out_shape=jax.ShapeDtypeStruct((R, N), F32),
        compiler_params=_params(("arbitrary",)),
        name="adaln",
    )(cond, w, b.reshape(1, N))


EV_Z = 0
EV_XBC = EV_Z + SSD_INNER
EV_RQ = EV_XBC + SSD_CONV_CH
EV_RK = EV_RQ + RET_QK
EV_RV = EV_RK + RET_QK
EV_RG = EV_RV + RET_INNER
EV_DT = EV_RG + RET_INNER
EV_END = EV_DT + LANES


def _even_in_kernel(x_ref, mod_ref, gain_ref, w_ref, cos_ref, sin_ref,
                    z_ref, xbc_ref, dt_ref, rq_ref, rk_ref, rv_ref, rg_ref):
    mod = mod_ref[0]
    h = _modulate(x_ref[0], gain_ref[...], mod[0:1], mod[1:2]).astype(BF16)

    def mm(lo, hi):
        return _dot(h, w_ref[:, lo:hi])

    z_ref[0] = mm(EV_Z, EV_XBC).astype(BF16)
    xbc_ref[0] = mm(EV_XBC, EV_RQ)
    cos = cos_ref[...]
    sin = sin_ref[...]
    k_scale = RET_QK_DIM ** -0.5
    for j in range(RET_HEADS):
        lo = j * RET_QK_DIM
        a = mm(EV_RQ + lo, EV_RQ + lo + RET_QK_DIM)
        rq_ref[0, :, lo:lo + RET_QK_DIM] = (a * cos + pltpu.roll(a, RET_QK_DIM // 2, 1) * sin).astype(BF16)
        a = mm(EV_RK + lo, EV_RK + lo + RET_QK_DIM)
        rk_ref[0, :, lo:lo + RET_QK_DIM] = (
            (a * cos + pltpu.roll(a, RET_QK_DIM // 2, 1) * sin) * k_scale).astype(BF16)
    rv_ref[0] = mm(EV_RV, EV_RG).astype(BF16)
    rg_ref[0] = mm(EV_RG, EV_DT).astype(BF16)
    dt_ref[0] = mm(EV_DT, EV_END)


def _even_in(x, mod, gain, w, cosf, sinf, tm):
    B, S, D = x.shape
    tm = _tile(S, tm)
    bm = mod.shape[0]
    mod_map = (lambda b, i: (b, 0, 0)) if bm > 1 else (lambda b, i: (0, 0, 0))
    row = lambda n: pl.BlockSpec((1, tm, n), lambda b, i: (b, i, 0))
    outs = [(SSD_INNER, BF16), (SSD_CONV_CH, F32), (LANES, F32), (RET_QK, BF16), (RET_QK, BF16),
            (RET_INNER, BF16), (RET_INNER, BF16)]
    return pl.pallas_call(
        _even_in_kernel,
        grid=(B, S // tm),
        in_specs=[row(D),
                  pl.BlockSpec((1, 6, D), mod_map),
                  pl.BlockSpec((1, D), lambda b, i: (0, 0)),
                  pl.BlockSpec((D, EV_END), lambda b, i: (0, 0)),
                  pl.BlockSpec((tm, LANES), lambda b, i: (i, 0)),
                  pl.BlockSpec((tm, LANES), lambda b, i: (i, 0))],
        out_specs=[row(n) for n, _ in outs],
        out_shape=[jax.ShapeDtypeStruct((B, S, n), dt) for n, dt in outs],
        compiler_params=_params(("arbitrary", "arbitrary")),
        name="even_in",
    )(x, mod, gain, w, cosf, sinf)


def _ssd_prep_kernel(xbc_ref, prev_ref, next_ref, dt_ref, cw_ref, cb_ref, dtb_ref, alog_ref,
                     xs_ref, bm_ref, cm_ref, acol_ref, arow_ref, wrow_ref, *, tm, chunk):
    i = pl.program_id(1)
    n = pl.num_programs(1)
    x = xbc_ref[0]
    zero_row = jnp.zeros((1, x.shape[1]), F32)
    pr = jnp.where(i > 0, prev_ref[0][SUBLANES - 1:SUBLANES, :], zero_row)
    nx = jnp.where(i < n - 1, next_ref[0][0:1, :], zero_row)
    rows = lax.broadcasted_iota(jnp.int32, (tm, 1), 0)
    xp = jnp.where(rows == 0, pr, pltpu.roll(x, 1, 0))
    xn = jnp.where(rows == tm - 1, nx, pltpu.roll(x, tm - 1, 0))
    cw = cw_ref[...]
    y = _silu(xp * cw[0:1] + x * cw[1:2] + xn * cw[2:3] + cb_ref[...])
    xs_ref[0] = y[:, :SSD_INNER].astype(BF16)
    bm_ref[0] = y[:, SSD_INNER:SSD_INNER + SSD_BC].astype(BF16)
    cm_ref[0] = y[:, SSD_INNER + SSD_BC:].astype(BF16)

    t = dt_ref[0] + dtb_ref[...]
    dt = jnp.maximum(t, 0.0) + jnp.log(1.0 + jnp.exp(-jnp.abs(t)))
    la = -dt * jnp.exp(alog_ref[...])
    rmod = rows % chunk
    fwd = la
    rev = la
    sh = 1
    while sh < chunk:
        fwd = fwd + jnp.where(rmod >= sh, pltpu.roll(fwd, sh, 0), 0.0)
        rev = rev + jnp.where(rmod < chunk - sh, pltpu.roll(rev, tm - sh, 0), 0.0)
        sh *= 2
    lane = lax.broadcasted_iota(jnp.int32, (1, LANES), 1)
    acc = jnp.where(lane < SSD_HEADS, fwd, rev)
    acc_t = acc.T
    dt_t = dt.T
    hp = SSD_HEADS // SSD_GROUPS
    for j in range(2 * SSD_GROUPS):
        acol_ref[0, j] = acc[:, j * hp:(j + 1) * hp]
        arow_ref[0, j] = acc_t[j * hp:(j + 1) * hp, :]
        wrow_ref[0, j] = dt_t[j * hp:(j + 1) * hp, :]


def _ssd_prep(xbc, dt, conv_w, conv_b, dtb, alog, tm):
    B, S, _ = xbc.shape
    tm = _tile(S, tm)
    assert tm % CHUNK == 0
    nh = S // SUBLANES
    hp = SSD_HEADS // SSD_GROUPS
    gd = 2 * SSD_GROUPS
    row = lambda n: pl.BlockSpec((1, tm, n), lambda b, i: (b, i, 0))
    const = lambda r, n: pl.BlockSpec((r, n), lambda b, i: (0, 0))
    return pl.pallas_call(
        functools.partial(_ssd_prep_kernel, tm=tm, chunk=CHUNK),
        grid=(B, S // tm),
        in_specs=[row(SSD_CONV_CH),
                  pl.BlockSpec((1, SUBLANES, SSD_CONV_CH),
                               lambda b, i: (b, jnp.maximum(i * (tm // SUBLANES) - 1, 0), 0)),
                  pl.BlockSpec((1, SUBLANES, SSD_CONV_CH),
                               lambda b, i: (b, jnp.minimum((i + 1) * (tm // SUBLANES), nh - 1), 0)),
                  row(LANES),
                  const(3, SSD_CONV_CH), const(1, SSD_CONV_CH), const(1, LANES), const(1, LANES)],
        out_specs=[row(SSD_INNER), row(SSD_BC), row(SSD_BC),
                   pl.BlockSpec((1, gd, tm, hp), lambda b, i: (b, 0, i, 0)),
                   pl.BlockSpec((1, gd, hp, tm), lambda b, i: (b, 0, 0, i)),
                   pl.BlockSpec((1, gd, hp, tm), lambda b, i: (b, 0, 0, i))],
        out_shape=[jax.ShapeDtypeStruct((B, S, SSD_INNER), BF16),
                   jax.ShapeDtypeStruct((B, S, SSD_BC), BF16),
                   jax.ShapeDtypeStruct((B, S, SSD_BC), BF16),
                   jax.ShapeDtypeStruct((B, gd, S, hp), F32),
                   jax.ShapeDtypeStruct((B, gd, hp, S), F32),
                   jax.ShapeDtypeStruct((B, gd, hp, S), F32)],
        compiler_params=_params(("arbitrary", "arbitrary")),
        name="ssd_prep",
    )(xbc, xbc, xbc, dt, conv_w, conv_b, dtb, alog)


def _scan_direction(q_ref, k_ref, v_ref, ac_ref, ar_ref, wr_ref, y_ref, h_scr, d, *, G, R, P, N):
    reverse = d == 1
    C = q_ref.shape[1]
    li = lax.broadcasted_iota(jnp.int32, (C, C), 0)
    si = lax.broadcasted_iota(jnp.int32, (C, C), 1)
    mask = (si >= li) if reverse else (li >= si)
    far = 0 if reverse else C - 1
    width = max(P, LANES)
    heads_per = width // P
    lane_head = lax.broadcasted_iota(jnp.int32, (1, width), 1) // P
    RP = R * P
    for g in range(G):
        q = q_ref[0, :, g * N:(g + 1) * N]
        k = k_ref[0, :, g * N:(g + 1) * N]
        s = _dot_nt(q, k)
        k_t = k.astype(F32).T
        ac = ac_ref[0, g]
        ar = ar_ref[0, g]
        wr = wr_ref[0, g]
        for u in range(RP // width):
            lo = u * width
            v = v_ref[0, :, g * RP + lo:g * RP + lo + width].astype(BF16)
            h_old = h_scr[d, g, :, lo:lo + width]
            y_diag = None
            for j in range(heads_per):
                r = u * heads_per + j
                a_col = ac[:, r:r + 1]
                a_row = ar[r:r + 1, :]
                w_row = wr[r:r + 1, :]
                tot = a_row[:, far:far + 1]
                decay = jnp.exp(jnp.where(mask, a_col - a_row, NEG_BIG))
                m = (s * decay * w_row).astype(BF16)
                kw = (k_t * (jnp.exp(tot - a_row) * w_row)).astype(BF16)
                yd = _dot(m, v)
                st = _dot(kw, v)
                e_col = jnp.exp(a_col)
                e_tot = jnp.exp(tot)
                if y_diag is None:
                    y_diag, state, col_scale, tot_scale = yd, st, e_col, e_tot
                else:
                    sel = lane_head == j
                    y_diag = jnp.where(sel, yd, y_diag)
                    state = jnp.where(sel, st, state)
                    col_scale = jnp.where(sel, e_col, col_scale)
                    tot_scale = jnp.where(sel, e_tot, tot_scale)
            y = y_diag + col_scale * _dot(q, h_old.astype(BF16))
            y_ref[0, :, g * RP + lo:g * RP + lo + width] = y.astype(y_ref.dtype)
            h_scr[d, g, :, lo:lo + width] = tot_scale * h_old + state


def _scan_kernel(qf_ref, kf_ref, vf_ref, qb_ref, kb_ref, vb_ref, acf_ref, arf_ref, wrf_ref,
                 acb_ref, arb_ref, wrb_ref, h0f_ref, h0b_ref, yf_ref, yb_ref, hff_ref, hfb_ref, h_scr, **dims):
    c = pl.program_id(1)

    @pl.when(c == 0)
    def _():
        h_scr[0] = h0f_ref[0]
        h_scr[1] = h0b_ref[0]

    _scan_direction(qf_ref, kf_ref, vf_ref, acf_ref, arf_ref, wrf_ref, yf_ref, h_scr, 0, **dims)
    _scan_direction(qb_ref, kb_ref, vb_ref, acb_ref, arb_ref, wrb_ref, yb_ref, h_scr, 1, **dims)

    @pl.when(c == pl.num_programs(1) - 1)
    def _():
        hff_ref[0] = h_scr[0]
        hfb_ref[0] = h_scr[1]


def _bidir_scan(q, k, v, acol, arow, wrow, h0_f, h0_b, *, G, R, P, N):
    B, S, _ = q.shape
    C = _tile(S, CHUNK)
    nc = S // C
    RP = R * P
    bx = (lambda b: b) if acol.shape[0] > 1 else (lambda b: 0)
    fwd = lambda c: c
    rev = lambda c: nc - 1 - c
    seq = lambda n, ci: pl.BlockSpec((1, C, n), lambda b, c: (b, ci(c), 0))
    col = lambda d, ci: pl.BlockSpec((1, G, C, R), lambda b, c: (bx(b), d, ci(c), 0))
    rowt = lambda d, ci: pl.BlockSpec((1, G, R, C), lambda b, c: (bx(b), d, 0, ci(c)))
    state = pl.BlockSpec((1, G, N, RP), lambda b, c: (b, 0, 0, 0))
    return pl.pallas_call(
        functools.partial(_scan_kernel, G=G, R=R, P=P, N=N),
        grid=(B, nc),
        in_specs=[seq(G * N, fwd), seq(G * N, fwd), seq(G * RP, fwd),
                  seq(G * N, rev), seq(G * N, rev), seq(G * RP, rev),
                  col(0, fwd), rowt(0, fwd), rowt(0, fwd), col(1, rev), rowt(1, rev), rowt(1, rev),
                  state, state],
        out_specs=[seq(G * RP, fwd), seq(G * RP, rev), state, state],
        out_shape=[jax.ShapeDtypeStruct((B, S, G * RP), BF16), jax.ShapeDtypeStruct((B, S, G * RP), BF16),
                   jax.ShapeDtypeStruct((B, G, N, RP), F32), jax.ShapeDtypeStruct((B, G, N, RP), F32)],
        scratch_shapes=[pltpu.VMEM((2, G, N, RP), F32)],
        compiler_params=_params(("arbitrary", "arbitrary")),
        name="bidir_scan",
    )(q, k, v, q, k, v, acol, arow, wrow, acol, arow, wrow, h0_f, h0_b)


def _even_out_kernel(ysf_ref, ysb_ref, xs_ref, z_ref, yrf_ref, yrb_ref, rg_ref, x_ref, mod_ref, dvec_ref, sn_ref,
                     w_ref, o_ref):
    f32 = lambda ref: ref[0].astype(F32)
    y = f32(ysf_ref) + f32(ysb_ref) + f32(xs_ref) * dvec_ref[...]
    y = _rms(y * _silu(f32(z_ref))) * sn_ref[...]
    out = _dot(y.astype(BF16), w_ref[0:SSD_INNER, :])
    yr = f32(yrf_ref) + f32(yrb_ref)
    rg = _silu(f32(rg_ref))
    for j in range(RET_HEADS):
        lo = j * RET_V_DIM
        r = _rms(yr[:, lo:lo + RET_V_DIM]) * rg[:, lo:lo + RET_V_DIM]
        out = out + _dot(r.astype(BF16), w_ref[SSD_INNER + lo:SSD_INNER + lo + RET_V_DIM, :])
    o_ref[0] = x_ref[0] + mod_ref[0][2:3] * out


def _even_out(ysf, ysb, xs, z, yrf, yrb, rg, x, mod, dvec, ssd_norm, w, tm):
    B, S, D = x.shape
    tm = _tile(S, tm)
    bm = mod.shape[0]
    mod_map = (lambda b, i: (b, 0, 0)) if bm > 1 else (lambda b, i: (0, 0, 0))
    row = lambda n: pl.BlockSpec((1, tm, n), lambda b, i: (b, i, 0))
    const = lambda r, n: pl.BlockSpec((r, n), lambda b, i: (0, 0))
    return pl.pallas_call(
        _even_out_kernel,
        grid=(B, S // tm),
        in_specs=[row(SSD_INNER), row(SSD_INNER), row(SSD_INNER), row(SSD_INNER),
                  row(RET_INNER), row(RET_INNER), row(RET_INNER), row(D),
                  pl.BlockSpec((1, 6, D), mod_map),
                  const(1, SSD_INNER), const(1, SSD_INNER), const(SSD_INNER + RET_INNER, D)],
        out_specs=row(D),
        out_shape=jax.ShapeDtypeStruct((B, S, D), F32),
        compiler_params=_params(("arbitrary", "arbitrary")),
        name="even_out",
    )(ysf, ysb, xs, z, yrf, yrb, rg, x, mod, dvec, ssd_norm, w)


def _ffn_kernel(x_ref, mod_ref, gain_ref, wg_ref, wu_ref, wd_ref, o_ref, h_scr, acc_scr):
    f = pl.program_id(2)
    mod = mod_ref[0]

    @pl.when(f == 0)
    def _():
        h_scr[...] = _modulate(x_ref[0], gain_ref[...], mod[3:4], mod[4:5]).astype(BF16)
        acc_scr[...] = jnp.zeros_like(acc_scr)

    h = h_scr[...]
    a = (_silu(_dot(h, wg_ref[...])) * _dot(h, wu_ref[...])).astype(BF16)
    acc_scr[...] += _dot(a, wd_ref[...])

    @pl.when(f == pl.num_programs(2) - 1)
    def _():
        o_ref[0] = x_ref[0] + mod[5:6] * acc_scr[...]


def _ffn(x, mod, gain, wg, wu, wd, tm, tf):
    B, S, D = x.shape
    F = wg.shape[1]
    tm = _tile(S, tm)
    tf = _tile(F, tf)
    bm = mod.shape[0]
    mod_map = (lambda b, i, f: (b, 0, 0)) if bm > 1 else (lambda b, i, f: (0, 0, 0))
    return pl.pallas_call(
        _ffn_kernel,
        grid=(B, S // tm, F // tf),
        in_specs=[pl.BlockSpec((1, tm, D), lambda b, i, f: (b, i, 0)),
                  pl.BlockSpec((1, 6, D), mod_map),
                  pl.BlockSpec((1, D), lambda b, i, f: (0, 0)),
                  pl.BlockSpec((D, tf), lambda b, i, f: (0, f)),
                  pl.BlockSpec((D, tf), lambda b, i, f: (0, f)),
                  pl.BlockSpec((tf, D), lambda b, i, f: (f, 0))],
        out_specs=pl.BlockSpec((1, tm, D), lambda b, i, f: (b, i, 0)),
        out_shape=jax.ShapeDtypeStruct((B, S, D), F32),
        scratch_shapes=[pltpu.VMEM((tm, D), BF16), pltpu.VMEM((tm, D), F32)],
        compiler_params=_params(("arbitrary", "arbitrary", "arbitrary")),
        name="ffn_dense",
    )(x, mod, gain, wg, wu, wd)


def _odd_in_kernel(x_ref, mod_ref, gain_ref, w_ref, qn_ref, kn_ref, bd_ref, cos_ref, sin_ref,
                   q_ref, k_ref, v_ref):
    mod = mod_ref[0]
    h = _modulate(x_ref[0], gain_ref[...], mod[0:1], mod[1:2]).astype(BF16)
    cos = cos_ref[...]
    sin = sin_ref[...]
    bd = bd_ref[...]
    lane = lax.broadcasted_iota(jnp.int32, (1, LANES), 1)
    quarter = DIFF_HEAD_DIM // 4
    first = (lane % (2 * quarter)) < quarter
    for base, n_ref, o_ref, scale in ((0, qn_ref, q_ref, DIFF_HEAD_DIM ** -0.5),
                                      (DIFF_INNER, kn_ref, k_ref, 1.0)):
        for j in range(DIFF_INNER // MXU_N):
            lo = j * MXU_N
            a = _dot(h, w_ref[:, base + lo:base + lo + MXU_N])
            sq = a * a
            hi = sq.astype(BF16)
            rest = (sq - hi.astype(F32)).astype(BF16)
            ms = (_dot(hi, bd) + _dot(rest, bd)) * (1.0 / DIFF_HEAD_DIM)
            a = a * lax.rsqrt(ms + NORM_EPS)
            for c in range(MXU_N // LANES):
                b = a[:, c * LANES:(c + 1) * LANES] * n_ref[...]
                partner = jnp.where(first, pltpu.roll(b, LANES - quarter, 1), pltpu.roll(b, quarter, 1))
                o_ref[0, :, lo + c * LANES:lo + (c + 1) * LANES] = ((b * cos + partner * sin) * scale).astype(BF16)
    v_ref[0] = _dot(h, w_ref[:, 2 * DIFF_INNER:3 * DIFF_INNER]).astype(BF16)


def _odd_in(x, mod, gain, w, qn, kn, bd, cosf, sinf, tm):
    B, S, D = x.shape
    tm = _tile(S, tm)
    bm = mod.shape[0]
    mod_map = (lambda b, i: (b, 0, 0)) if bm > 1 else (lambda b, i: (0, 0, 0))
    row = lambda n: pl.BlockSpec((1, tm, n), lambda b, i: (b, i, 0))
    const = lambda r, n: pl.BlockSpec((r, n), lambda b, i: (0, 0))
    return pl.pallas_call(
        _odd_in_kernel,
        grid=(B, S // tm),
        in_specs=[row(D), pl.BlockSpec((1, 6, D), mod_map), const(1, D), const(D, 3 * DIFF_INNER),
                  const(1, LANES), const(1, LANES), const(MXU_N, MXU_N),
                  pl.BlockSpec((tm, LANES), lambda b, i: (i, 0)),
                  pl.BlockSpec((tm, LANES), lambda b, i: (i, 0))],
        out_specs=[row(DIFF_INNER)] * 3,
        out_shape=[jax.ShapeDtypeStruct((B, S, DIFF_INNER), BF16)] * 3,
        compiler_params=_params(("arbitrary", "arbitrary")),
        name="odd_in",
    )(x, mod, gain, w, qn, kn, bd, cosf, sinf)


def _attn_kernel(sc_ref, q_ref, kl_ref, vl_ref, kc_ref, vc_ref, sub_ref, o_ref,
                 q2_scr, m_scr, l_scr, acc_scr, *, tq, tk, unroll, out_scale):
    q = q_ref[0]
    lane = lax.broadcasted_iota(jnp.int32, (1, LANES), 1)
    lo_half = lane < DIFF_HEAD_DIM
    zero = jnp.zeros_like(q)
    q2_scr[0:tq, :] = jnp.where(lo_half, q, zero)
    q2_scr[tq:2 * tq, :] = jnp.where(lo_half, zero, q)
    l_scr[...] = jnp.zeros_like(l_scr)
    acc_scr[...] = jnp.zeros_like(acc_scr)
    n_lat = kl_ref.shape[1] // tk
    use_shift = sc_ref[0, 2] > 0.0

    def over_keys(update, unroll):
        def body(j, carry):
            start = pl.multiple_of(j * tk, tk)
            update(kl_ref[0, pl.ds(start, tk), :], vl_ref[0, pl.ds(start, tk), :])
            return carry

        lax.fori_loop(0, n_lat, body, 0, unroll=unroll)
        update(kc_ref[0], vc_ref[0])

    @pl.when(use_shift)
    def _():
        shift = sc_ref[0, 1]

        def update(k, v):
            p = jnp.exp(_dot_nt(q2_scr[...], k) - shift)
            part = p[:, 0:LANES]
            for c in range(1, p.shape[1] // LANES):
                part = part + p[:, c * LANES:(c + 1) * LANES]
            l_scr[...] += part
            acc_scr[...] += _dot(p.astype(BF16), v)

        over_keys(update, unroll)
        acc_scr[...] = acc_scr[...] / jnp.sum(l_scr[...], axis=1, keepdims=True)

    @pl.when(jnp.logical_not(use_shift))
    def _():
        m_scr[...] = jnp.full_like(m_scr, -jnp.inf)

        def update(k, v):
            s = _dot_nt(q2_scr[...], k)
            m_prev = m_scr[...]
            m_new = jnp.maximum(m_prev, jnp.max(s, axis=1, keepdims=True))
            alpha = jnp.exp(m_prev - m_new)
            p = jnp.exp(s - m_new[:, 0:1])
            l_scr[...] = alpha * l_scr[...] + jnp.sum(p, axis=1, keepdims=True)
            acc_scr[...] = alpha * acc_scr[...] + _dot(p.astype(BF16), v)
            m_scr[...] = m_new

        over_keys(update, 1)
        acc_scr[...] = acc_scr[...] / l_scr[...]

    o = acc_scr[0:tq, :] - sc_ref[0, 0] * acc_scr[tq:2 * tq, :]
    o_ref[0] = (_rms(o) * sub_ref[...] * out_scale).astype(BF16)


def _attention(lam, q, k_lat, v_lat, k_ctx, v_ctx, subln, out_scale, tq, tk, unroll):
    B, L, _ = q.shape
    Lc = k_ctx.shape[1]
    tq = _tile(L, tq)
    tk = _tile(L, tk)
    unroll = min(unroll, L // tk)
    head = lambda n: pl.BlockSpec((1, n, LANES), lambda b, h, i: (b, 0, h))
    return pl.pallas_call(
        functools.partial(_attn_kernel, tq=tq, tk=tk, unroll=unroll, out_scale=out_scale),
        grid=(B, DIFF_HEADS, L // tq),
        in_specs=[pl.BlockSpec(memory_space=pltpu.SMEM),
                  pl.BlockSpec((1, tq, LANES), lambda b, h, i: (b, i, h)),
                  head(L), head(L), head(Lc), head(Lc),
                  pl.BlockSpec((1, LANES), lambda b, h, i: (0, 0))],
        out_specs=pl.BlockSpec((1, tq, LANES), lambda b, h, i: (b, i, h)),
        out_shape=jax.ShapeDtypeStruct((B, L, DIFF_INNER), BF16),
        scratch_shapes=[pltpu.VMEM((2 * tq, LANES), BF16), pltpu.VMEM((2 * tq, LANES), F32),
                        pltpu.VMEM((2 * tq, LANES), F32), pltpu.VMEM((2 * tq, LANES), F32)],
        compiler_params=_params(("arbitrary", "arbitrary", "arbitrary")),
        name="diff_attn",
    )(lam, q, k_lat, v_lat, k_ctx, v_ctx, subln)


def _odd_out_kernel(o_ref, x_ref, mod_ref, gain_ref, w_ref, wr_ref, xo_ref, h_ref, route_ref):
    mod = mod_ref[0]
    x1 = x_ref[0] + mod[2:3] * _dot(o_ref[0], w_ref[...])
    xo_ref[0] = x1
    h = _modulate(x1, gain_ref[...], mod[3:4], mod[4:5])
    h_ref[0] = h
    logits = _dot_f32(h, wr_ref[...])
    lane = lax.broadcasted_iota(jnp.int32, logits.shape, 1)
    lg = jnp.where(lane < N_EXPERTS, logits, -jnp.inf)
    v1 = jnp.max(lg, axis=1, keepdims=True)
    i1 = jnp.min(jnp.where(lg == v1, lane, LANES), axis=1, keepdims=True)
    lg2 = jnp.where(lane == i1, -jnp.inf, lg)
    v2 = jnp.max(lg2, axis=1, keepdims=True)
    i2 = jnp.min(jnp.where(lg2 == v2, lane, LANES), axis=1, keepdims=True)
    e = jnp.exp(v2 - v1)
    w1 = 1.0 / (1.0 + e)
    w2 = e * w1
    route = jnp.where(lane == 0, i1.astype(F32),
                      jnp.where(lane == 1, i2.astype(F32),
                                jnp.where(lane == 2, w1, jnp.where(lane == 3, w2, 0.0))))
    route_ref[0] = route


def _odd_out(o, x, mod, gain, w, w_router, tm):
    B, S, D = x.shape
    tm = _tile(S, tm)
    row = lambda n: pl.BlockSpec((1, tm, n), lambda b, i: (b, i, 0))
    const = lambda r, n: pl.BlockSpec((r, n), lambda b, i: (0, 0))
    return pl.pallas_call(
        _odd_out_kernel,
        grid=(B, S // tm),
        in_specs=[row(DIFF_INNER), row(D), pl.BlockSpec((1, 6, D), lambda b, i: (b, 0, 0)),
                  const(1, D), const(DIFF_INNER, D), const(D, LANES)],
        out_specs=[row(D), row(D), row(LANES)],
        out_shape=[jax.ShapeDtypeStruct((B, S, D), F32), jax.ShapeDtypeStruct((B, S, D), F32),
                   jax.ShapeDtypeStruct((B, S, LANES), F32)],
        compiler_params=_params(("arbitrary", "arbitrary")),
        name="odd_out",
    )(o, x, mod, gain, w, w_router)


def _row_copy(src, src_row, dst, dst_row, sem):
    return pltpu.make_async_copy(src.at[pl.ds(src_row, 1), :], dst.at[pl.ds(dst_row, 1), :], sem)


def _dispatch_kernel(last_tile_ref, slot_ref, h_ref, hs_ref, zero_buf, sem, zero_sem, *, tt, tm):
    @pl.when(pl.program_id(0) == 0)
    def _():
        zero_buf[...] = jnp.zeros_like(zero_buf)
        for e in range(2 * N_EXPERTS):
            start = pl.multiple_of(last_tile_ref[e] * tm, tm)
            clear = pltpu.make_async_copy(zero_buf, hs_ref.at[pl.ds(start, tm), :], zero_sem)
            clear.start()
            clear.wait()

    def issue(r, carry):
        _row_copy(h_ref, r, hs_ref, slot_ref[0, 0, 2 * r], sem).start()
        _row_copy(h_ref, r, hs_ref, slot_ref[0, 0, 2 * r + 1], sem).start()
        return carry

    lax.fori_loop(0, tt, issue, 0, unroll=DMA_ISSUE_UNROLL)
    for _ in range(2):
        pltpu.make_async_copy(h_ref, hs_ref.at[pl.ds(0, tt), :], sem).wait()


def _dispatch(last_tile, slots, h, n_rows, tt, tm):
    T, D = h.shape
    tt = _tile(T, tt)
    grid_spec = pltpu.PrefetchScalarGridSpec(
        num_scalar_prefetch=1,
        grid=(T // tt,),
        in_specs=[pl.BlockSpec((1, 1, 2 * tt), lambda i, lt: (i, 0, 0), memory_space=pltpu.SMEM),
                  pl.BlockSpec((tt, D), lambda i, lt: (i, 0))],
        out_specs=pl.BlockSpec(memory_space=pl.ANY),
        scratch_shapes=[pltpu.VMEM((tm, D), F32), pltpu.SemaphoreType.DMA(()), pltpu.SemaphoreType.DMA(())])
    return pl.pallas_call(
        functools.partial(_dispatch_kernel, tt=tt, tm=tm),
        grid_spec=grid_spec,
        out_shape=jax.ShapeDtypeStruct((n_rows, D), F32),
        compiler_params=_params(("arbitrary",)),
        name="moe_dispatch",
    )(last_tile, slots.reshape(T // tt, 1, 2 * tt), h)


def _expert_kernel(te_ref, tv_ref, ts_ref, hs_ref, wg_ref, wu_ref, wd_ref, ys_ref, h_scr, acc_scr):
    del te_ref, ts_ref
    i = pl.program_id(0)
    f = pl.program_id(1)

    @pl.when(tv_ref[i] > 0)
    def _():
        @pl.when(f == 0)
        def _():
            h_scr[...] = hs_ref[...].astype(BF16)
            acc_scr[...] = jnp.zeros_like(acc_scr)

        h = h_scr[...]
        g = _dot(h, wg_ref[0].astype(BF16))
        u = _dot(h, wu_ref[0].astype(BF16))
        acc_scr[...] += _dot((_silu(g) * u).astype(BF16), wd_ref[0].astype(BF16))

        @pl.when(f == pl.num_programs(1) - 1)
        def _():
            ys_ref[...] = acc_scr[...]

    @pl.when((tv_ref[i] == 0) & (f == pl.num_programs(1) - 1))
    def _():
        ys_ref[...] = jnp.zeros_like(ys_ref)


def _experts(tile_expert, tile_valid, tile_src, hs, wg, wu, wd, tm, tf):
    n_rows, D = hs.shape
    F = wg.shape[2]
    tf = _tile(F, tf)
    nf = F // tf
    fi = lambda i, f, tv: jnp.where(tv[i] > 0, f, nf - 1)
    grid_spec = pltpu.PrefetchScalarGridSpec(
        num_scalar_prefetch=3,
        grid=(n_rows // tm, nf),
        in_specs=[pl.BlockSpec((tm, D), lambda i, f, te, tv, ts: (ts[i], 0)),
                  pl.BlockSpec((1, D, tf), lambda i, f, te, tv, ts: (te[i], 0, fi(i, f, tv))),
                  pl.BlockSpec((1, D, tf), lambda i, f, te, tv, ts: (te[i], 0, fi(i, f, tv))),
                  pl.BlockSpec((1, tf, D), lambda i, f, te, tv, ts: (te[i], fi(i, f, tv), 0))],
        out_specs=pl.BlockSpec((tm, D), lambda i, f, te, tv, ts: (i, 0)),
        scratch_shapes=[pltpu.VMEM((tm, D), BF16), pltpu.VMEM((tm, D), F32)])
    return pl.pallas_call(
        _expert_kernel,
        grid_spec=grid_spec,
        out_shape=jax.ShapeDtypeStruct((n_rows, D), F32),
        compiler_params=_params(("arbitrary", "arbitrary")),
        name="moe_experts",
    )(tile_expert, tile_valid, tile_src, hs, wg, wu, wd)


def _combine_kernel(slot_ref, ys_ref, x_ref, mod_ref, route_ref, o_ref, buf0, buf1, sem, *, tt):
    def issue(r, carry):
        _row_copy(ys_ref, slot_ref[0, 0, 2 * r], buf0, r, sem).start()
        _row_copy(ys_ref, slot_ref[0, 0, 2 * r + 1], buf1, r, sem).start()
        return carry

    lax.fori_loop(0, tt, issue, 0, unroll=DMA_ISSUE_UNROLL)
    pltpu.make_async_copy(ys_ref.at[pl.ds(0, tt), :], buf0, sem).wait()
    pltpu.make_async_copy(ys_ref.at[pl.ds(0, tt), :], buf1, sem).wait()
    route = route_ref[0]
    y = route[:, 2:3] * buf0[...] + route[:, 3:4] * buf1[...]
    o_ref[0] = x_ref[0] + mod_ref[0][5:6] * y


def _combine(slots, ys, x, mod, route, tt):
    B, S, D = x.shape
    tt = _tile(S, tt)
    n = S // tt
    return pl.pallas_call(
        functools.partial(_combine_kernel, tt=tt),
        grid=(B, n),
        in_specs=[pl.BlockSpec((1, 1, 2 * tt), lambda b, i: (b * n + i, 0, 0), memory_space=pltpu.SMEM),
                  pl.BlockSpec(memory_space=pl.ANY),
                  pl.BlockSpec((1, tt, D), lambda b, i: (b, i, 0)),
                  pl.BlockSpec((1, 6, D), lambda b, i: (b, 0, 0)),
                  pl.BlockSpec((1, tt, LANES), lambda b, i: (b, i, 0))],
        out_specs=pl.BlockSpec((1, tt, D), lambda b, i: (b, i, 0)),
        out_shape=jax.ShapeDtypeStruct((B, S, D), F32),
        scratch_shapes=[pltpu.VMEM((tt, D), F32), pltpu.VMEM((tt, D), F32), pltpu.SemaphoreType.DMA(())],
        compiler_params=_params(("arbitrary", "arbitrary")),
        name="moe_combine",
    )(slots.reshape(B * n, 1, 2 * tt), ys, x, mod, route)


def _routing_tables(route, tm):
    T = route.shape[0]
    experts = route[:, 0:2].astype(jnp.int32).reshape(2 * T)
    onehot = (experts[:, None] == jnp.arange(N_EXPERTS, dtype=jnp.int32)[None, :]).astype(jnp.int32)
    rank = jnp.sum((jnp.cumsum(onehot, axis=0) - onehot) * onehot, axis=1)
    counts = jnp.sum(onehot, axis=0)
    tiles = (counts + tm - 1) // tm
    tile_end = jnp.cumsum(tiles)
    offsets = (tile_end - tiles) * tm
    slots = jnp.sum(onehot * offsets[None, :], axis=1) + rank
    n_tiles = (2 * T) // tm + N_EXPERTS
    ids = jnp.arange(n_tiles, dtype=jnp.int32)
    tile_expert = jnp.minimum(jnp.sum((ids[:, None] >= tile_end[None, :]).astype(jnp.int32), axis=1),
                              N_EXPERTS - 1)
    last_used = jnp.minimum(jnp.sum((tile_end[-1] - 1 >= tile_end).astype(jnp.int32)), N_EXPERTS - 1)
    tile_valid = (ids < tile_end[-1]).astype(jnp.int32)
    tile_expert = jnp.where(tile_valid > 0, tile_expert, last_used)
    tile_src = jnp.minimum(ids, tile_end[-1] - 1)
    last_tile = jnp.maximum(tile_end - 1, 0)
    tail = ids[n_tiles - N_EXPERTS:]
    last_tile = jnp.concatenate([last_tile, jnp.where(tail >= tile_end[-1], tail, last_tile[0])])
    return (slots.astype(jnp.int32), tile_expert.astype(jnp.int32), tile_valid, tile_src.astype(jnp.int32),
            last_tile.astype(jnp.int32), n_tiles)


def _rope_angles(pos, dim):
    inv = ROPE_THETA ** (-jnp.arange(dim // 2, dtype=F32) / (dim // 2))
    return pos.astype(F32)[:, None] * inv[None, :]


def _ret_rope_tables(pos):
    ang = _rope_angles(pos, RET_QK_DIM)
    cos, sin = jnp.cos(ang), jnp.sin(ang)
    return jnp.concatenate([cos, cos], axis=1), jnp.concatenate([-sin, sin], axis=1)


def _axial_rope_tables(L):
    n_rows = L // GRID_W
    row = jnp.broadcast_to(jnp.arange(n_rows)[:, None], (n_rows, GRID_W)).reshape(-1)
    col = jnp.broadcast_to(jnp.arange(GRID_W)[None, :], (n_rows, GRID_W)).reshape(-1)
    ar = _rope_angles(row, DIFF_HEAD_DIM // 2)
    ac = _rope_angles(col, DIFF_HEAD_DIM // 2)
    cos = jnp.concatenate([jnp.cos(ar), jnp.cos(ar), jnp.cos(ac), jnp.cos(ac)], axis=1)
    sin = jnp.concatenate([-jnp.sin(ar), jnp.sin(ar), -jnp.sin(ac), jnp.sin(ac)], axis=1)
    return jnp.tile(cos, (1, LANES // DIFF_HEAD_DIM)), jnp.tile(sin, (1, LANES // DIFF_HEAD_DIM))


def _ret_decay_tables(ret_decay, S):
    C = min(S, CHUNK)
    lam = -jnp.exp(ret_decay.astype(F32))
    pos = jnp.arange(S) % C
    steps = jnp.stack([pos + 1, C - pos]).astype(F32)
    a = (lam[:, :, None] * steps[:, None, :]).reshape(1, 2 * RET_HEADS, S)
    return a[:, :, :, None], a[:, :, None, :], jnp.ones((1, 2 * RET_HEADS, 1, S), F32)


def kernel(x, c, ctx, c_ctx, even_w_mod, even_b_mod, even_norm1, even_norm2, even_w_in, even_conv_w, even_conv_b, even_dt_bias, even_a_log, even_d, even_ssd_norm, even_ret_decay, even_w_out, even_ffn_gate, even_ffn_up, even_ffn_down, odd_w_mod, odd_b_mod, odd_norm1, odd_norm2, odd_w_in, odd_q_norm, odd_k_norm, odd_lambda, odd_subln, odd_w_out, odd_router, odd_exp_gate, odd_exp_up, odd_exp_down):
    B, L, D = x.shape
    Lc = ctx.shape[1]
    T = B * L
    row = lambda v: v.reshape(1, -1)
    pad_lanes = lambda v: jnp.pad(v.reshape(1, -1), ((0, 0), (0, LANES - v.size)))

    cond = jnp.concatenate([c, c_ctx[None, :], jnp.zeros((SUBLANES - B - 1, D), F32)], axis=0)

    mod = _adaln(cond, even_w_mod[0], even_b_mod[0]).reshape(SUBLANES, 6, D)
    mod_l, mod_c = mod[:B], mod[B:B + 1]
    w_in = even_w_in[0]
    cut = SSD_INNER + SSD_CONV_CH
    w_in = jnp.concatenate([w_in[:, :cut], w_in[:, cut + 2 * SSD_HEADS:], w_in[:, cut:cut + 2 * SSD_HEADS],
                            jnp.zeros((D, LANES - 2 * SSD_HEADS), F32)], axis=1).astype(BF16)
    g1 = row(even_norm1[0])
    cos_c, sin_c = _ret_rope_tables(jnp.arange(Lc))
    cos_l, sin_l = _ret_rope_tables(Lc + jnp.arange(L))
    dtb = pad_lanes(even_dt_bias[0])
    alog = pad_lanes(even_a_log[0])
    conv_b = row(even_conv_b[0])
    ssd_kw = dict(G=SSD_GROUPS, R=SSD_HEADS // SSD_GROUPS, P=SSD_HEAD_DIM, N=SSD_STATE)
    ret_kw = dict(G=RET_HEADS, R=1, P=RET_V_DIM, N=RET_QK_DIM)

    def mix(xin, modv, cosf, sinf, hs, hr, tm):
        z, xbc, dt, rq, rk, rv, rg = _even_in(xin, modv, g1, w_in, cosf, sinf, tm)
        xs, bmat, cmat, acol, arow, wrow = _ssd_prep(xbc, dt, even_conv_w[0], conv_b, dtb, alog, tm)
        ysf, ysb, hs_f, hs_b = _bidir_scan(cmat, bmat, xs, acol, arow, wrow, hs[0], hs[1], **ssd_kw)
        racol, rarow, rwrow = _ret_decay_tables(even_ret_decay[0], xin.shape[1])
        yrf, yrb, hr_f, hr_b = _bidir_scan(rq, rk, rv, racol, rarow, rwrow, hr[0], hr[1], **ret_kw)
        return (ysf, ysb, xs, z, yrf, yrb, rg), (hs_f, hs_b), (hr_f, hr_b)

    zs = jnp.zeros((B, SSD_GROUPS, SSD_STATE, SSD_INNER // SSD_GROUPS), F32)
    zr = jnp.zeros((B, RET_HEADS, RET_QK_DIM, RET_V_DIM), F32)
    tm_c = 256
    tm_l = 512
    mixed_c, hs, hr = mix(ctx, mod_c, cos_c, sin_c, (zs, zs), (zr, zr), tm_c)
    mixed_l, _, _ = mix(x, mod_l, cos_l, sin_l, hs, hr, tm_l)

    dvec = row(jnp.repeat(even_d[0], SSD_HEAD_DIM))
    snorm = row(even_ssd_norm[0])
    w_out = even_w_out[0].astype(BF16)
    g2 = row(even_norm2[0])
    wg = even_ffn_gate[0].astype(BF16)
    wu = even_ffn_up[0].astype(BF16)
    wd = even_ffn_down[0].astype(BF16)
    ffn_tf = 1408

    xl = _even_out(*mixed_l, x, mod_l, dvec, snorm, w_out, tm_l)
    xl = _ffn(xl, mod_l, g2, wg, wu, wd, tm_l, ffn_tf)
    xc = _even_out(*mixed_c, ctx, mod_c, dvec, snorm, w_out, tm_c)
    xc = _ffn(xc, mod_c, g2, wg, wu, wd, tm_c, ffn_tf)

    mod = _adaln(cond, odd_w_mod[0], odd_b_mod[0]).reshape(SUBLANES, 6, D)
    mod_l, mod_c = mod[:B], mod[B:B + 1]
    lam_init = 0.8 - 0.6 * math.exp(-0.3 * 1)
    lq1, lk1, lq2, lk2 = odd_lambda[0]
    lam = jnp.exp(jnp.sum(lq1 * lk1)) - jnp.exp(jnp.sum(lq2 * lk2)) + lam_init
    bound = 1.02 * DIFF_HEAD_DIM ** 0.5 * jnp.max(jnp.abs(odd_q_norm[0])) * jnp.max(jnp.abs(odd_k_norm[0]))
    lam = jnp.stack([lam, bound - SHIFT_HEADROOM, (bound <= SHIFT_MAX_BOUND).astype(F32),
                     jnp.zeros((), F32)]).reshape(1, 4)
    w_in = odd_w_in[0].astype(BF16)
    g1 = row(odd_norm1[0])
    qn = row(jnp.tile(odd_q_norm[0], LANES // DIFF_HEAD_DIM))
    kn = row(jnp.tile(odd_k_norm[0], LANES // DIFF_HEAD_DIM))
    gid = jnp.arange(MXU_N) // DIFF_HEAD_DIM
    bd = (gid[:, None] == gid[None, :]).astype(BF16)
    cos_l, sin_l = _axial_rope_tables(L)
    cos_c, sin_c = jnp.ones((Lc, LANES), F32), jnp.zeros((Lc, LANES), F32)
    q_l, k_l, v_l = _odd_in(xl, mod_l, g1, w_in, qn, kn, bd, cos_l, sin_l, tm_l)
    _, k_c, v_c = _odd_in(xc, mod_c, g1, w_in, qn, kn, bd, cos_c, sin_c, tm_c)
    o = _attention(lam, q_l, k_l, v_l, k_c, v_c, row(odd_subln[0]), 1.0 - lam_init, 1024, 512, 4)

    w_router = jnp.pad(odd_router[0], ((0, 0), (0, LANES - N_EXPERTS)))
    xl, h, route = _odd_out(o, xl, mod_l, row(odd_norm2[0]), odd_w_out[0].astype(BF16), w_router, tm_l)
    moe_tm = min(1024, T)
    slots, tile_expert, tile_valid, tile_src, last_tile, n_tiles = _routing_tables(route.reshape(T, LANES), moe_tm)
    hs_sorted = _dispatch(last_tile, slots, h.reshape(T, D), n_tiles * moe_tm, 512, moe_tm)
    ys_sorted = _experts(tile_expert, tile_valid, tile_src, hs_sorted, odd_exp_gate[0], odd_exp_up[0],
                         odd_exp_down[0], moe_tm, 512)
    return _combine(slots, ys_sorted, xl, mod_l, route, 256)
```

```python
import functools
import math

import jax
import jax.numpy as jnp
from jax import lax
from jax.experimental import pallas as pl
from jax.experimental.pallas import tpu as pltpu

F32 = jnp.float32
BF16 = jnp.bfloat16
LANES = 128
SUBLANES = 8
MXU_N = 256
VMEM_LIMIT = 56 * 1024 * 1024

GRID_W = 64
CHUNK = 128
NORM_EPS = 1e-6
ROPE_THETA = 10000.0
SSD_HEADS = 16
SSD_HEAD_DIM = 64
SSD_INNER = SSD_HEADS * SSD_HEAD_DIM
SSD_GROUPS = 2
SSD_STATE = 128
SSD_BC = SSD_GROUPS * SSD_STATE
SSD_CONV_CH = SSD_INNER + 2 * SSD_BC
RET_HEADS = 4
RET_QK_DIM = 128
RET_V_DIM = 256
RET_QK = RET_HEADS * RET_QK_DIM
RET_INNER = RET_HEADS * RET_V_DIM
DIFF_HEADS = 8
DIFF_HEAD_DIM = 64
DIFF_V_DIM = 2 * DIFF_HEAD_DIM
DIFF_INNER = DIFF_HEADS * DIFF_V_DIM
N_EXPERTS = 8
NEG_BIG = -1e30
SHIFT_HEADROOM = 30.0
SHIFT_MAX_BOUND = 55.0
DMA_ISSUE_UNROLL = 8


def _params(sem):
    return pltpu.CompilerParams(dimension_semantics=sem, vmem_limit_bytes=VMEM_LIMIT)


def _tile(n, pref):
    t = min(n, pref)
    assert n % t == 0, (n, t)
    return t


def _silu(x):
    return x * jax.nn.sigmoid(x)


def _rms(x):
    return x * lax.rsqrt(jnp.mean(x * x, axis=-1, keepdims=True) + NORM_EPS)


def _modulate(x, gain, shift, scale):
    return _rms(x) * gain * (1.0 + scale) + shift


def _dot(a, b):
    return jnp.dot(a, b, preferred_element_type=F32)


def _dot_nt(a, b):
    return lax.dot_general(a, b, (((1,), (1,)), ((), ())), preferred_element_type=F32)


def _split3(a):
    hi = a.astype(BF16)
    r1 = a - hi.astype(F32)
    mid = r1.astype(BF16)
    lo = (r1 - mid.astype(F32)).astype(BF16)
    return hi, mid, lo


def _adaln_kernel(c_ref, w_ref, b_ref, o_ref):
    a = _silu(c_ref[...]).astype(BF16)
    o_ref[...] = _dot(a, w_ref[...].astype(BF16)) + b_ref[...]


def _adaln(cond, w, b):
    R, D = cond.shape
    N = w.shape[1]
    tn = _tile(N, 1024)
    return pl.pallas_call(
        _adaln_kernel,
        grid=(N // tn,),
        in_specs=[pl.BlockSpec((R, D), lambda j: (0, 0)),
                  pl.BlockSpec((D, tn), lambda j: (0, j)),
                  pl.BlockSpec((1, tn), lambda j: (0, j))],
        out_specs=pl.BlockSpec((R, tn), lambda j: (0, j)),
        out_shape=jax.ShapeDtypeStruct((R, N), F32),
        compiler_params=_params(("arbitrary",)),
        name="adaln",
    )(cond, w, b.reshape(1, N))


EV_Z = 0
EV_XBC = EV_Z + SSD_INNER
EV_RQ = EV_XBC + SSD_CONV_CH
EV_RK = EV_RQ + RET_QK
EV_RV = EV_RK + RET_QK
EV_RG = EV_RV + RET_INNER
EV_DT = EV_RG + RET_INNER
EV_END = EV_DT + LANES


def _even_in_kernel(x_ref, mod_ref, gain_ref, w_ref, cos_ref, sin_ref,
                    z_ref, xbc_ref, dt_ref, rq_ref, rk_ref, rv_ref, rg_ref):
    mod = mod_ref[0]
    h = _modulate(x_ref[0], gain_ref[...], mod[0:1], mod[1:2]).astype(BF16)

    def mm(lo, hi):
        return _dot(h, w_ref[:, lo:hi])

    z_ref[0] = mm(EV_Z, EV_XBC).astype(BF16)
    xbc_ref[0] = mm(EV_XBC, EV_RQ)
    cos = cos_ref[...]
    sin = sin_ref[...]
    k_scale = RET_QK_DIM ** -0.5
    qk = mm(EV_RQ, EV_RV)
    for j in range(RET_HEADS):
        lo = j * RET_QK_DIM
        a = qk[:, lo:lo + RET_QK_DIM]
        rq_ref[0, :, lo:lo + RET_QK_DIM] = (a * cos + pltpu.roll(a, RET_QK_DIM // 2, 1) * sin).astype(BF16)
        a = qk[:, RET_QK + lo:RET_QK + lo + RET_QK_DIM]
        rk_ref[0, :, lo:lo + RET_QK_DIM] = (
            (a * cos + pltpu.roll(a, RET_QK_DIM // 2, 1) * sin) * k_scale).astype(BF16)
    rv_ref[0] = mm(EV_RV, EV_RG).astype(BF16)
    rg_ref[0] = mm(EV_RG, EV_DT).astype(BF16)
    dt_ref[0] = mm(EV_DT, EV_END)


def _even_in(x, mod, gain, w, cosf, sinf, tm):
    B, S, D = x.shape
    tm = _tile(S, tm)
    bm = mod.shape[0]
    mod_map = (lambda b, i: (b, 0, 0)) if bm > 1 else (lambda b, i: (0, 0, 0))
    row = lambda n: pl.BlockSpec((1, tm, n), lambda b, i: (b, i, 0))
    outs = [(SSD_INNER, BF16), (SSD_CONV_CH, F32), (LANES, F32), (RET_QK, BF16), (RET_QK, BF16),
            (RET_INNER, BF16), (RET_INNER, BF16)]
    return pl.pallas_call(
        _even_in_kernel,
        grid=(B, S // tm),
        in_specs=[row(D),
                  pl.BlockSpec((1, 6, D), mod_map),
                  pl.BlockSpec((1, D), lambda b, i: (0, 0)),
                  pl.BlockSpec((D, EV_END), lambda b, i: (0, 0)),
                  pl.BlockSpec((tm, LANES), lambda b, i: (i, 0)),
                  pl.BlockSpec((tm, LANES), lambda b, i: (i, 0))],
        out_specs=[row(n) for n, _ in outs],
        out_shape=[jax.ShapeDtypeStruct((B, S, n), dt) for n, dt in outs],
        compiler_params=_params(("arbitrary", "arbitrary")),
        name="even_in",
    )(x, mod, gain, w, cosf, sinf)


def _ssd_prep_kernel(xbc_ref, prev_ref, next_ref, dt_ref, cw_ref, cb_ref, dtb_ref, alog_ref,
                     xs_ref, bm_ref, cm_ref, acol_ref, arow_ref, wrow_ref, *, tm, chunk):
    i = pl.program_id(1)
    n = pl.num_programs(1)
    x = xbc_ref[0]
    zero_row = jnp.zeros((1, x.shape[1]), F32)
    pr = jnp.where(i > 0, prev_ref[0][SUBLANES - 1:SUBLANES, :], zero_row)
    nx = jnp.where(i < n - 1, next_ref[0][0:1, :], zero_row)
    rows = lax.broadcasted_iota(jnp.int32, (tm, 1), 0)
    xp = jnp.where(rows == 0, pr, pltpu.roll(x, 1, 0))
    xn = jnp.where(rows == tm - 1, nx, pltpu.roll(x, tm - 1, 0))
    cw = cw_ref[...]
    y = _silu(xp * cw[0:1] + x * cw[1:2] + xn * cw[2:3] + cb_ref[...])
    xs_ref[0] = y[:, :SSD_INNER].astype(BF16)
    bm_ref[0] = y[:, SSD_INNER:SSD_INNER + SSD_BC].astype(BF16)
    cm_ref[0] = y[:, SSD_INNER + SSD_BC:].astype(BF16)

    t = dt_ref[0] + dtb_ref[...]
    dt = jnp.maximum(t, 0.0) + jnp.log(1.0 + jnp.exp(-jnp.abs(t)))
    la = -dt * jnp.exp(alog_ref[...])
    rmod = rows % chunk
    fwd = la
    rev = la
    sh = 1
    while sh < chunk:
        fwd = fwd + jnp.where(rmod >= sh, pltpu.roll(fwd, sh, 0), 0.0)
        rev = rev + jnp.where(rmod < chunk - sh, pltpu.roll(rev, tm - sh, 0), 0.0)
        sh *= 2
    lane = lax.broadcasted_iota(jnp.int32, (1, LANES), 1)
    acc = jnp.where(lane < SSD_HEADS, fwd, rev)
    acc_t = acc.T
    dt_t = dt.T
    hp = SSD_HEADS // SSD_GROUPS
    for j in range(2 * SSD_GROUPS):
        acol_ref[0, j] = acc[:, j * hp:(j + 1) * hp]
        arow_ref[0, j] = acc_t[j * hp:(j + 1) * hp, :]
        wrow_ref[0, j] = dt_t[j * hp:(j + 1) * hp, :]


def _ssd_prep(xbc, dt, conv_w, conv_b, dtb, alog, tm):
    B, S, _ = xbc.shape
    tm = _tile(S, tm)
    assert tm % CHUNK == 0
    nh = S // SUBLANES
    hp = SSD_HEADS // SSD_GROUPS
    gd = 2 * SSD_GROUPS
    row = lambda n: pl.BlockSpec((1, tm, n), lambda b, i: (b, i, 0))
    const = lambda r, n: pl.BlockSpec((r, n), lambda b, i: (0, 0))
    return pl.pallas_call(
        functools.partial(_ssd_prep_kernel, tm=tm, chunk=CHUNK),
        grid=(B, S // tm),
        in_specs=[row(SSD_CONV_CH),
                  pl.BlockSpec((1, SUBLANES, SSD_CONV_CH),
                               lambda b, i: (b, jnp.maximum(i * (tm // SUBLANES) - 1, 0), 0)),
                  pl.BlockSpec((1, SUBLANES, SSD_CONV_CH),
                               lambda b, i: (b, jnp.minimum((i + 1) * (tm // SUBLANES), nh - 1), 0)),
                  row(LANES),
                  const(3, SSD_CONV_CH), const(1, SSD_CONV_CH), const(1, LANES), const(1, LANES)],
        out_specs=[row(SSD_INNER), row(SSD_BC), row(SSD_BC),
                   pl.BlockSpec((1, gd, tm, hp), lambda b, i: (b, 0, i, 0)),
                   pl.BlockSpec((1, gd, hp, tm), lambda b, i: (b, 0, 0, i)),
                   pl.BlockSpec((1, gd, hp, tm), lambda b, i: (b, 0, 0, i))],
        out_shape=[jax.ShapeDtypeStruct((B, S, SSD_INNER), BF16),
                   jax.ShapeDtypeStruct((B, S, SSD_BC), BF16),
                   jax.ShapeDtypeStruct((B, S, SSD_BC), BF16),
                   jax.ShapeDtypeStruct((B, gd, S, hp), F32),
                   jax.ShapeDtypeStruct((B, gd, hp, S), F32),
                   jax.ShapeDtypeStruct((B, gd, hp, S), F32)],
        compiler_params=_params(("arbitrary", "arbitrary")),
        name="ssd_prep",
    )(xbc, xbc, xbc, dt, conv_w, conv_b, dtb, alog)


def _scan_direction(q_ref, k_ref, v_ref, ac_ref, ar_ref, wr_ref, y_ref, h_scr, d, *, G, R, P, N):
    reverse = d == 1
    C = q_ref.shape[1]
    li = lax.broadcasted_iota(jnp.int32, (C, C), 0)
    si = lax.broadcasted_iota(jnp.int32, (C, C), 1)
    mask = (si >= li) if reverse else (li >= si)
    far = 0 if reverse else C - 1
    width = max(P, LANES)
    heads_per = width // P
    lane_head = lax.broadcasted_iota(jnp.int32, (1, width), 1) // P
    RP = R * P
    for g in range(G):
        q = q_ref[0, :, g * N:(g + 1) * N]
        k = k_ref[0, :, g * N:(g + 1) * N]
        s = _dot_nt(q, k)
        k_t = k.astype(F32).T
        ac = ac_ref[0, g]
        ar = ar_ref[0, g]
        wr = wr_ref[0, g]
        for u in range(RP // width):
            lo = u * width
            v = v_ref[0, :, g * RP + lo:g * RP + lo + width].astype(BF16)
            h_old = h_scr[d, g, :, lo:lo + width]
            y_diag = None
            for j in range(heads_per):
                r = u * heads_per + j
                a_col = ac[:, r:r + 1]
                a_row = ar[r:r + 1, :]
                w_row = wr[r:r + 1, :]
                tot = a_row[:, far:far + 1]
                decay = jnp.exp(jnp.where(mask, a_col - a_row, NEG_BIG))
                m = (s * decay * w_row).astype(BF16)
                kw = (k_t * (jnp.exp(tot - a_row) * w_row)).astype(BF16)
                yd = _dot(m, v)
                st = _dot(kw, v)
                e_col = jnp.exp(a_col)
                e_tot = jnp.exp(tot)
                if y_diag is None:
                    y_diag, state, col_scale, tot_scale = yd, st, e_col, e_tot
                else:
                    sel = lane_head == j
                    y_diag = jnp.where(sel, yd, y_diag)
                    state = jnp.where(sel, st, state)
                    col_scale = jnp.where(sel, e_col, col_scale)
                    tot_scale = jnp.where(sel, e_tot, tot_scale)
            y = y_diag + col_scale * _dot(q, h_old.astype(BF16))
            y_ref[0, :, g * RP + lo:g * RP + lo + width] = y.astype(y_ref.dtype)
            h_scr[d, g, :, lo:lo + width] = tot_scale * h_old + state


SCAN_INS = 14
SCAN_OUTS = 4


def _scan_kernel(*refs, families):
    n = len(families)
    ins = [refs[SCAN_INS * i:SCAN_INS * (i + 1)] for i in range(n)]
    outs = [refs[SCAN_INS * n + SCAN_OUTS * i:SCAN_INS * n + SCAN_OUTS * (i + 1)] for i in range(n)]
    scrs = refs[(SCAN_INS + SCAN_OUTS) * n:]
    c = pl.program_id(1)

    @pl.when(c == 0)
    def _():
        for fam_in, h_scr in zip(ins, scrs):
            h_scr[0] = fam_in[12][0]
            h_scr[1] = fam_in[13][0]

    for dims, fam_in, fam_out, h_scr in zip(families, ins, outs, scrs):
        qf, kf, vf, qb, kb, vb, acf, arf, wrf, acb, arb, wrb = fam_in[:12]
        _scan_direction(qf, kf, vf, acf, arf, wrf, fam_out[0], h_scr, 0, **dims)
        _scan_direction(qb, kb, vb, acb, arb, wrb, fam_out[1], h_scr, 1, **dims)

    @pl.when(c == pl.num_programs(1) - 1)
    def _():
        for fam_out, h_scr in zip(outs, scrs):
            fam_out[2][0] = h_scr[0]
            fam_out[3][0] = h_scr[1]


def _bidir_scans(families):
    B, S, _ = families[0][0].shape
    C = _tile(S, CHUNK)
    nc = S // C
    fwd = lambda c: c
    rev = lambda c: nc - 1 - c
    in_specs, out_specs, out_shape, scratch, args, dims_list = [], [], [], [], [], []
    for q, k, v, acol, arow, wrow, h0_f, h0_b, dims in families:
        G, R, P, N = dims["G"], dims["R"], dims["P"], dims["N"]
        RP = R * P
        bx = (lambda b: b) if acol.shape[0] > 1 else (lambda b: 0)
        seq = lambda n, ci: pl.BlockSpec((1, C, n), lambda b, c, ci=ci: (b, ci(c), 0))
        col = lambda d, ci, G=G, R=R, bx=bx: pl.BlockSpec((1, G, C, R), lambda b, c: (bx(b), d, ci(c), 0))
        rowt = lambda d, ci, G=G, R=R, bx=bx: pl.BlockSpec((1, G, R, C), lambda b, c: (bx(b), d, 0, ci(c)))
        state = pl.BlockSpec((1, G, N, RP), lambda b, c: (b, 0, 0, 0))
        in_specs += [seq(G * N, fwd), seq(G * N, fwd), seq(G * RP, fwd),
                     seq(G * N, rev), seq(G * N, rev), seq(G * RP, rev),
                     col(0, fwd), rowt(0, fwd), rowt(0, fwd), col(1, rev), rowt(1, rev), rowt(1, rev),
                     state, state]
        out_specs += [seq(G * RP, fwd), seq(G * RP, rev), state, state]
        out_shape += [jax.ShapeDtypeStruct((B, S, G * RP), BF16), jax.ShapeDtypeStruct((B, S, G * RP), BF16),
                      jax.ShapeDtypeStruct((B, G, N, RP), F32), jax.ShapeDtypeStruct((B, G, N, RP), F32)]
        scratch.append(pltpu.VMEM((2, G, N, RP), F32))
        args += [q, k, v, q, k, v, acol, arow, wrow, acol, arow, wrow, h0_f, h0_b]
        dims_list.append(dims)
    res = pl.pallas_call(
        functools.partial(_scan_kernel, families=tuple(dims_list)),
        grid=(B, nc),
        in_specs=in_specs,
        out_specs=out_specs,
        out_shape=out_shape,
        scratch_shapes=scratch,
        compiler_params=_params(("arbitrary", "arbitrary")),
        name="bidir_scans",
    )(*args)
    return [res[SCAN_OUTS * i:SCAN_OUTS * (i + 1)] for i in range(len(families))]


def _even_out_kernel(ysf_ref, ysb_ref, xs_ref, z_ref, yrf_ref, yrb_ref, rg_ref, x_ref, mod_ref, dvec_ref, sn_ref,
                     w_ref, o_ref):
    f32 = lambda ref: ref[0].astype(F32)
    y = f32(ysf_ref) + f32(ysb_ref) + f32(xs_ref) * dvec_ref[...]
    y = _rms(y * _silu(f32(z_ref))) * sn_ref[...]
    out = _dot(y.astype(BF16), w_ref[0:SSD_INNER, :])
    yr = f32(yrf_ref) + f32(yrb_ref)
    rg = _silu(f32(rg_ref))
    for j in range(RET_HEADS):
        lo = j * RET_V_DIM
        r = _rms(yr[:, lo:lo + RET_V_DIM]) * rg[:, lo:lo + RET_V_DIM]
        out = out + _dot(r.astype(BF16), w_ref[SSD_INNER + lo:SSD_INNER + lo + RET_V_DIM, :])
    o_ref[0] = x_ref[0] + mod_ref[0][2:3] * out


def _even_out(ysf, ysb, xs, z, yrf, yrb, rg, x, mod, dvec, ssd_norm, w, tm):
    B, S, D = x.shape
    tm = _tile(S, tm)
    bm = mod.shape[0]
    mod_map = (lambda b, i: (b, 0, 0)) if bm > 1 else (lambda b, i: (0, 0, 0))
    row = lambda n: pl.BlockSpec((1, tm, n), lambda b, i: (b, i, 0))
    const = lambda r, n: pl.BlockSpec((r, n), lambda b, i: (0, 0))
    return pl.pallas_call(
        _even_out_kernel,
        grid=(B, S // tm),
        in_specs=[row(SSD_INNER), row(SSD_INNER), row(SSD_INNER), row(SSD_INNER),
                  row(RET_INNER), row(RET_INNER), row(RET_INNER), row(D),
                  pl.BlockSpec((1, 6, D), mod_map),
                  const(1, SSD_INNER), const(1, SSD_INNER), const(SSD_INNER + RET_INNER, D)],
        out_specs=row(D),
        out_shape=jax.ShapeDtypeStruct((B, S, D), F32),
        compiler_params=_params(("arbitrary", "arbitrary")),
        name="even_out",
    )(ysf, ysb, xs, z, yrf, yrb, rg, x, mod, dvec, ssd_norm, w)


def _ffn_kernel(x_ref, mod_ref, gain_ref, wg_ref, wu_ref, wd_ref, o_ref, h_scr, acc_scr):
    f = pl.program_id(2)
    mod = mod_ref[0]

    @pl.when(f == 0)
    def _():
        h_scr[...] = _modulate(x_ref[0], gain_ref[...], mod[3:4], mod[4:5]).astype(BF16)
        acc_scr[...] = jnp.zeros_like(acc_scr)

    h = h_scr[...]
    a = (_silu(_dot(h, wg_ref[...])) * _dot(h, wu_ref[...])).astype(BF16)
    acc_scr[...] += _dot(a, wd_ref[...])

    @pl.when(f == pl.num_programs(2) - 1)
    def _():
        o_ref[0] = x_ref[0] + mod[5:6] * acc_scr[...]


def _ffn(x, mod, gain, wg, wu, wd, tm, tf):
    B, S, D = x.shape
    F = wg.shape[1]
    tm = _tile(S, tm)
    tf = _tile(F, tf)
    bm = mod.shape[0]
    mod_map = (lambda b, i, f: (b, 0, 0)) if bm > 1 else (lambda b, i, f: (0, 0, 0))
    resident = dict(pipeline_mode=pl.Buffered(1)) if tf == F else {}
    return pl.pallas_call(
        _ffn_kernel,
        grid=(B, S // tm, F // tf),
        in_specs=[pl.BlockSpec((1, tm, D), lambda b, i, f: (b, i, 0)),
                  pl.BlockSpec((1, 6, D), mod_map),
                  pl.BlockSpec((1, D), lambda b, i, f: (0, 0)),
                  pl.BlockSpec((D, tf), lambda b, i, f: (0, f), **resident),
                  pl.BlockSpec((D, tf), lambda b, i, f: (0, f), **resident),
                  pl.BlockSpec((tf, D), lambda b, i, f: (f, 0), **resident)],
        out_specs=pl.BlockSpec((1, tm, D), lambda b, i, f: (b, i, 0)),
        out_shape=jax.ShapeDtypeStruct((B, S, D), F32),
        scratch_shapes=[pltpu.VMEM((tm, D), BF16), pltpu.VMEM((tm, D), F32)],
        compiler_params=_params(("arbitrary", "arbitrary", "arbitrary")),
        name="ffn_dense",
    )(x, mod, gain, wg, wu, wd)


def _odd_in_kernel(x_ref, mod_ref, gain_ref, w_ref, qn_ref, kn_ref, bd_ref, cos_ref, sin_ref,
                   q_ref, k_ref, v_ref):
    mod = mod_ref[0]
    h = _modulate(x_ref[0], gain_ref[...], mod[0:1], mod[1:2]).astype(BF16)
    cos = cos_ref[...]
    sin = sin_ref[...]
    bd = bd_ref[...]
    lane = lax.broadcasted_iota(jnp.int32, (1, LANES), 1)
    quarter = DIFF_HEAD_DIM // 4
    first = (lane % (2 * quarter)) < quarter
    for base, n_ref, o_ref, scale in ((0, qn_ref, q_ref, DIFF_HEAD_DIM ** -0.5),
                                      (DIFF_INNER, kn_ref, k_ref, 1.0)):
        for j in range(DIFF_INNER // MXU_N):
            lo = j * MXU_N
            a = _dot(h, w_ref[:, base + lo:base + lo + MXU_N])
            sq = a * a
            hi = sq.astype(BF16)
            rest = (sq - hi.astype(F32)).astype(BF16)
            ms = (_dot(hi, bd) + _dot(rest, bd)) * (1.0 / DIFF_HEAD_DIM)
            a = a * lax.rsqrt(ms + NORM_EPS)
            for c in range(MXU_N // LANES):
                b = a[:, c * LANES:(c + 1) * LANES] * n_ref[...]
                partner = jnp.where(first, pltpu.roll(b, LANES - quarter, 1), pltpu.roll(b, quarter, 1))
                o_ref[0, :, lo + c * LANES:lo + (c + 1) * LANES] = ((b * cos + partner * sin) * scale).astype(BF16)
    v_ref[0] = _dot(h, w_ref[:, 2 * DIFF_INNER:3 * DIFF_INNER]).astype(BF16)


def _odd_in(x, mod, gain, w, qn, kn, bd, cosf, sinf, tm):
    B, S, D = x.shape
    tm = _tile(S, tm)
    bm = mod.shape[0]
    mod_map = (lambda b, i: (b, 0, 0)) if bm > 1 else (lambda b, i: (0, 0, 0))
    row = lambda n: pl.BlockSpec((1, tm, n), lambda b, i: (b, i, 0))
    const = lambda r, n: pl.BlockSpec((r, n), lambda b, i: (0, 0))
    return pl.pallas_call(
        _odd_in_kernel,
        grid=(B, S // tm),
        in_specs=[row(D), pl.BlockSpec((1, 6, D), mod_map), const(1, D), const(D, 3 * DIFF_INNER),
                  const(1, LANES), const(1, LANES), const(MXU_N, MXU_N),
                  pl.BlockSpec((tm, LANES), lambda b, i: (i, 0)),
                  pl.BlockSpec((tm, LANES), lambda b, i: (i, 0))],
        out_specs=[row(DIFF_INNER)] * 3,
        out_shape=[jax.ShapeDtypeStruct((B, S, DIFF_INNER), BF16)] * 3,
        compiler_params=_params(("arbitrary", "arbitrary")),
        name="odd_in",
    )(x, mod, gain, w, qn, kn, bd, cosf, sinf)


def _attn_kernel(sc_ref, q_ref, kl_ref, vl_ref, kc_ref, vc_ref, sub_ref, o_ref,
                 q2_scr, m_scr, l_scr, acc_scr, *, tq, tk, unroll, out_scale):
    q = q_ref[0]
    lane = lax.broadcasted_iota(jnp.int32, (1, LANES), 1)
    lo_half = lane < DIFF_HEAD_DIM
    zero = jnp.zeros_like(q)
    q2_scr[0:tq, :] = jnp.where(lo_half, q, zero)
    q2_scr[tq:2 * tq, :] = jnp.where(lo_half, zero, q)
    l_scr[...] = jnp.zeros_like(l_scr)
    acc_scr[...] = jnp.zeros_like(acc_scr)
    n_lat = kl_ref.shape[1] // tk
    use_shift = sc_ref[0, 2] > 0.0

    def over_keys(update, unroll):
        def body(j, carry):
            start = pl.multiple_of(j * tk, tk)
            update(kl_ref[0, pl.ds(start, tk), :], vl_ref[0, pl.ds(start, tk), :])
            return carry

        lax.fori_loop(0, n_lat, body, 0, unroll=unroll)
        update(kc_ref[0], vc_ref[0])

    @pl.when(use_shift)
    def _():
        shift = sc_ref[0, 1]

        def update(k, v):
            p = jnp.exp(_dot_nt(q2_scr[...], k) - shift)
            part = p[:, 0:LANES]
            for c in range(1, p.shape[1] // LANES):
                part = part + p[:, c * LANES:(c + 1) * LANES]
            l_scr[...] += part
            acc_scr[...] += _dot(p.astype(BF16), v)

        over_keys(update, unroll)
        acc_scr[...] = acc_scr[...] / jnp.sum(l_scr[...], axis=1, keepdims=True)

    @pl.when(jnp.logical_not(use_shift))
    def _():
        m_scr[...] = jnp.full_like(m_scr, -jnp.inf)

        def update(k, v):
            s = _dot_nt(q2_scr[...], k)
            m_prev = m_scr[...]
            m_new = jnp.maximum(m_prev, jnp.max(s, axis=1, keepdims=True))
            alpha = jnp.exp(m_prev - m_new)
            p = jnp.exp(s - m_new[:, 0:1])
            l_scr[...] = alpha * l_scr[...] + jnp.sum(p, axis=1, keepdims=True)
            acc_scr[...] = alpha * acc_scr[...] + _dot(p.astype(BF16), v)
            m_scr[...] = m_new

        over_keys(update, 1)
        acc_scr[...] = acc_scr[...] / l_scr[...]

    o = acc_scr[0:tq, :] - sc_ref[0, 0] * acc_scr[tq:2 * tq, :]
    o_ref[0] = (_rms(o) * sub_ref[...] * out_scale).astype(BF16)


def _attention(lam, q, k_lat, v_lat, k_ctx, v_ctx, subln, out_scale, tq, tk, unroll):
    B, L, _ = q.shape
    Lc = k_ctx.shape[1]
    tq = _tile(L, tq)
    tk = _tile(L, tk)
    unroll = min(unroll, L // tk)
    head = lambda n: pl.BlockSpec((1, n, LANES), lambda b, h, i: (b, 0, h))
    return pl.pallas_call(
        functools.partial(_attn_kernel, tq=tq, tk=tk, unroll=unroll, out_scale=out_scale),
        grid=(B, DIFF_HEADS, L // tq),
        in_specs=[pl.BlockSpec(memory_space=pltpu.SMEM),
                  pl.BlockSpec((1, tq, LANES), lambda b, h, i: (b, i, h)),
                  head(L), head(L), head(Lc), head(Lc),
                  pl.BlockSpec((1, LANES), lambda b, h, i: (0, 0))],
        out_specs=pl.BlockSpec((1, tq, LANES), lambda b, h, i: (b, i, h)),
        out_shape=jax.ShapeDtypeStruct((B, L, DIFF_INNER), BF16),
        scratch_shapes=[pltpu.VMEM((2 * tq, LANES), BF16), pltpu.VMEM((2 * tq, LANES), F32),
                        pltpu.VMEM((2 * tq, LANES), F32), pltpu.VMEM((2 * tq, LANES), F32)],
        compiler_params=_params(("arbitrary", "arbitrary", "arbitrary")),
        name="diff_attn",
    )(lam, q, k_lat, v_lat, k_ctx, v_ctx, subln)


def _odd_out_kernel(o_ref, x_ref, mod_ref, gain_ref, w_ref, wr_ref, xo_ref, h_ref, route_ref):
    mod = mod_ref[0]
    x1 = x_ref[0] + mod[2:3] * _dot(o_ref[0], w_ref[...])
    xo_ref[0] = x1
    h = _modulate(x1, gain_ref[...], mod[3:4], mod[4:5])
    h_ref[...] = h
    wr = wr_ref[...]
    h_hi, h_mid, h_lo = _split3(h)
    parts = _dot(h_lo, wr) + _dot(h_mid, wr) + _dot(h_hi, wr)
    logits = (pltpu.roll(parts, LANES - 2 * N_EXPERTS, 1) + pltpu.roll(parts, LANES - N_EXPERTS, 1)) + parts
    lane = lax.broadcasted_iota(jnp.int32, logits.shape, 1)
    lg = jnp.where(lane < N_EXPERTS, logits, -jnp.inf)
    v1 = jnp.max(lg, axis=1, keepdims=True)
    i1 = jnp.min(jnp.where(lg == v1, lane, LANES), axis=1, keepdims=True)
    lg2 = jnp.where(lane == i1, -jnp.inf, lg)
    v2 = jnp.max(lg2, axis=1, keepdims=True)
    i2 = jnp.min(jnp.where(lg2 == v2, lane, LANES), axis=1, keepdims=True)
    e = jnp.exp(v2 - v1)
    w1 = 1.0 / (1.0 + e)
    w2 = e * w1
    route = jnp.where(lane == 0, i1.astype(F32),
                      jnp.where(lane == 1, i2.astype(F32),
                                jnp.where(lane == 2, w1, jnp.where(lane == 3, w2, 0.0))))
    route_ref[...] = route


def _odd_out(o, x, mod, gain, w, w_router, tm):
    B, S, D = x.shape
    tm = _tile(S, tm)
    n = S // tm
    row = lambda n: pl.BlockSpec((1, tm, n), lambda b, i: (b, i, 0))
    flat = lambda w: pl.BlockSpec((tm, w), lambda b, i: (b * n + i, 0))
    const = lambda r, n: pl.BlockSpec((r, n), lambda b, i: (0, 0))
    return pl.pallas_call(
        _odd_out_kernel,
        grid=(B, n),
        in_specs=[row(DIFF_INNER), row(D), pl.BlockSpec((1, 6, D), lambda b, i: (b, 0, 0)),
                  const(1, D), const(DIFF_INNER, D), const(D, LANES)],
        out_specs=[row(D), flat(D), flat(LANES)],
        out_shape=[jax.ShapeDtypeStruct((B, S, D), F32), jax.ShapeDtypeStruct((B * S, D), F32),
                   jax.ShapeDtypeStruct((B * S, LANES), F32)],
        compiler_params=_params(("arbitrary", "arbitrary")),
        name="odd_out",
    )(o, x, mod, gain, w, w_router)


def _row_copy(src, src_row, dst, dst_row, sem):
    return pltpu.make_async_copy(src.at[pl.ds(src_row, 1), :], dst.at[pl.ds(dst_row, 1), :], sem)


def _dispatch_kernel(last_tile_ref, slot_ref, h_ref, hs_ref, zero_buf, sem, zero_sem, *, tt, tm):
    @pl.when(pl.program_id(0) == 0)
    def _():
        zero_buf[...] = jnp.zeros_like(zero_buf)
        for e in range(2 * N_EXPERTS):
            start = pl.multiple_of(last_tile_ref[e] * tm, tm)
            clear = pltpu.make_async_copy(zero_buf, hs_ref.at[pl.ds(start, tm), :], zero_sem)
            clear.start()
            clear.wait()

    def issue(r, carry):
        _row_copy(h_ref, r, hs_ref, slot_ref[0, 0, 2 * r], sem).start()
        _row_copy(h_ref, r, hs_ref, slot_ref[0, 0, 2 * r + 1], sem).start()
        return carry

    lax.fori_loop(0, tt, issue, 0, unroll=DMA_ISSUE_UNROLL)
    for _ in range(2):
        pltpu.make_async_copy(h_ref, hs_ref.at[pl.ds(0, tt), :], sem).wait()


def _dispatch(last_tile, slots, h, n_rows, tt, tm):
    T, D = h.shape
    tt = _tile(T, tt)
    grid_spec = pltpu.PrefetchScalarGridSpec(
        num_scalar_prefetch=1,
        grid=(T // tt,),
        in_specs=[pl.BlockSpec((1, 1, 2 * tt), lambda i, lt: (i, 0, 0), memory_space=pltpu.SMEM),
                  pl.BlockSpec((tt, D), lambda i, lt: (i, 0))],
        out_specs=pl.BlockSpec(memory_space=pl.ANY),
        scratch_shapes=[pltpu.VMEM((tm, D), F32), pltpu.SemaphoreType.DMA(()), pltpu.SemaphoreType.DMA(())])
    return pl.pallas_call(
        functools.partial(_dispatch_kernel, tt=tt, tm=tm),
        grid_spec=grid_spec,
        out_shape=jax.ShapeDtypeStruct((n_rows, D), F32),
        compiler_params=_params(("arbitrary",)),
        name="moe_dispatch",
    )(last_tile, slots.reshape(T // tt, 1, 2 * tt), h)


def _expert_kernel(te_ref, tv_ref, ts_ref, hs_ref, wg_ref, wu_ref, wd_ref, ys_ref, h_scr, acc_scr):
    del te_ref, ts_ref
    i = pl.program_id(0)
    f = pl.program_id(1)

    @pl.when(tv_ref[i] > 0)
    def _():
        @pl.when(f == 0)
        def _():
            h_scr[...] = hs_ref[...].astype(BF16)
            acc_scr[...] = jnp.zeros_like(acc_scr)

        h = h_scr[...]
        g = _dot(h, wg_ref[0].astype(BF16))
        u = _dot(h, wu_ref[0].astype(BF16))
        acc_scr[...] += _dot((_silu(g) * u).astype(BF16), wd_ref[0].astype(BF16))

        @pl.when(f == pl.num_programs(1) - 1)
        def _():
            ys_ref[...] = acc_scr[...]

    @pl.when((tv_ref[i] == 0) & (f == pl.num_programs(1) - 1))
    def _():
        ys_ref[...] = jnp.zeros_like(ys_ref)


def _experts(tile_expert, tile_valid, tile_src, hs, wg, wu, wd, tm, tf):
    n_rows, D = hs.shape
    F = wg.shape[2]
    tf = _tile(F, tf)
    nf = F // tf
    fi = lambda i, f, tv: jnp.where(tv[i] > 0, f, nf - 1)
    grid_spec = pltpu.PrefetchScalarGridSpec(
        num_scalar_prefetch=3,
        grid=(n_rows // tm, nf),
        in_specs=[pl.BlockSpec((tm, D), lambda i, f, te, tv, ts: (ts[i], 0)),
                  pl.BlockSpec((1, D, tf), lambda i, f, te, tv, ts: (te[i], 0, fi(i, f, tv))),
                  pl.BlockSpec((1, D, tf), lambda i, f, te, tv, ts: (te[i], 0, fi(i, f, tv))),
                  pl.BlockSpec((1, tf, D), lambda i, f, te, tv, ts: (te[i], fi(i, f, tv), 0))],
        out_specs=pl.BlockSpec((tm, D), lambda i, f, te, tv, ts: (i, 0)),
        scratch_shapes=[pltpu.VMEM((tm, D), BF16), pltpu.VMEM((tm, D), F32)])
    return pl.pallas_call(
        _expert_kernel,
        grid_spec=grid_spec,
        out_shape=jax.ShapeDtypeStruct((n_rows, D), F32),
        compiler_params=_params(("arbitrary", "arbitrary")),
        name="moe_experts",
    )(tile_expert, tile_valid, tile_src, hs, wg, wu, wd)


def _combine_kernel(slot_ref, ys_ref, x_ref, mod_ref, route_ref, o_ref, buf0, buf1, sem, *, tt):
    def issue(r, carry):
        _row_copy(ys_ref, slot_ref[0, 0, 2 * r], buf0, r, sem).start()
        _row_copy(ys_ref, slot_ref[0, 0, 2 * r + 1], buf1, r, sem).start()
        return carry

    lax.fori_loop(0, tt, issue, 0, unroll=DMA_ISSUE_UNROLL)
    pltpu.make_async_copy(ys_ref.at[pl.ds(0, tt), :], buf0, sem).wait()
    pltpu.make_async_copy(ys_ref.at[pl.ds(0, tt), :], buf1, sem).wait()
    route = route_ref[...]
    y = route[:, 2:3] * buf0[...] + route[:, 3:4] * buf1[...]
    o_ref[0] = x_ref[0] + mod_ref[0][5:6] * y


def _combine(slots, ys, x, mod, route, tt):
    B, S, D = x.shape
    tt = _tile(S, tt)
    n = S // tt
    return pl.pallas_call(
        functools.partial(_combine_kernel, tt=tt),
        grid=(B, n),
        in_specs=[pl.BlockSpec((1, 1, 2 * tt), lambda b, i: (b * n + i, 0, 0), memory_space=pltpu.SMEM),
                  pl.BlockSpec(memory_space=pl.ANY),
                  pl.BlockSpec((1, tt, D), lambda b, i: (b, i, 0)),
                  pl.BlockSpec((1, 6, D), lambda b, i: (b, 0, 0)),
                  pl.BlockSpec((tt, LANES), lambda b, i: (b * n + i, 0))],
        out_specs=pl.BlockSpec((1, tt, D), lambda b, i: (b, i, 0)),
        out_shape=jax.ShapeDtypeStruct((B, S, D), F32),
        scratch_shapes=[pltpu.VMEM((tt, D), F32), pltpu.VMEM((tt, D), F32), pltpu.SemaphoreType.DMA(())],
        compiler_params=_params(("arbitrary", "arbitrary")),
        name="moe_combine",
    )(slots.reshape(B * n, 1, 2 * tt), ys, x, mod, route)


def _routing_tables(route, tm):
    T = route.shape[0]
    experts = route[:, 0:2].astype(jnp.int32).reshape(2 * T)
    onehot = (experts[:, None] == jnp.arange(N_EXPERTS, dtype=jnp.int32)[None, :]).astype(jnp.int32)
    rank = jnp.sum((jnp.cumsum(onehot, axis=0) - onehot) * onehot, axis=1)
    counts = jnp.sum(onehot, axis=0)
    tiles = (counts + tm - 1) // tm
    tile_end = jnp.cumsum(tiles)
    offsets = (tile_end - tiles) * tm
    slots = jnp.sum(onehot * offsets[None, :], axis=1) + rank
    n_tiles = (2 * T) // tm + N_EXPERTS
    ids = jnp.arange(n_tiles, dtype=jnp.int32)
    tile_expert = jnp.minimum(jnp.sum((ids[:, None] >= tile_end[None, :]).astype(jnp.int32), axis=1),
                              N_EXPERTS - 1)
    last_used = jnp.minimum(jnp.sum((tile_end[-1] - 1 >= tile_end).astype(jnp.int32)), N_EXPERTS - 1)
    tile_valid = (ids < tile_end[-1]).astype(jnp.int32)
    tile_expert = jnp.where(tile_valid > 0, tile_expert, last_used)
    tile_src = jnp.minimum(ids, tile_end[-1] - 1)
    last_tile = jnp.maximum(tile_end - 1, 0)
    tail = ids[n_tiles - N_EXPERTS:]
    last_tile = jnp.concatenate([last_tile, jnp.where(tail >= tile_end[-1], tail, last_tile[0])])
    return (slots.astype(jnp.int32), tile_expert.astype(jnp.int32), tile_valid, tile_src.astype(jnp.int32),
            last_tile.astype(jnp.int32), n_tiles)


def _rope_angles(pos, dim):
    inv = ROPE_THETA ** (-jnp.arange(dim // 2, dtype=F32) / (dim // 2))
    return pos.astype(F32)[:, None] * inv[None, :]


def _ret_rope_tables(pos):
    ang = _rope_angles(pos, RET_QK_DIM)
    cos, sin = jnp.cos(ang), jnp.sin(ang)
    return jnp.concatenate([cos, cos], axis=1), jnp.concatenate([-sin, sin], axis=1)


def _axial_rope_tables(L):
    n_rows = L // GRID_W
    row = jnp.broadcast_to(jnp.arange(n_rows)[:, None], (n_rows, GRID_W)).reshape(-1)
    col = jnp.broadcast_to(jnp.arange(GRID_W)[None, :], (n_rows, GRID_W)).reshape(-1)
    ar = _rope_angles(row, DIFF_HEAD_DIM // 2)
    ac = _rope_angles(col, DIFF_HEAD_DIM // 2)
    cos = jnp.concatenate([jnp.cos(ar), jnp.cos(ar), jnp.cos(ac), jnp.cos(ac)], axis=1)
    sin = jnp.concatenate([-jnp.sin(ar), jnp.sin(ar), -jnp.sin(ac), jnp.sin(ac)], axis=1)
    return jnp.tile(cos, (1, LANES // DIFF_HEAD_DIM)), jnp.tile(sin, (1, LANES // DIFF_HEAD_DIM))


def _ret_decay_tables(ret_decay, S):
    C = min(S, CHUNK)
    lam = -jnp.exp(ret_decay.astype(F32))
    pos = jnp.arange(S) % C
    steps = jnp.stack([pos + 1, C - pos]).astype(F32)
    a = (lam[:, :, None] * steps[:, None, :]).reshape(1, 2 * RET_HEADS, S)
    return a[:, :, :, None], a[:, :, None, :], jnp.ones((1, 2 * RET_HEADS, 1, S), F32)


def kernel(x, c, ctx, c_ctx, even_w_mod, even_b_mod, even_norm1, even_norm2, even_w_in, even_conv_w, even_conv_b, even_dt_bias, even_a_log, even_d, even_ssd_norm, even_ret_decay, even_w_out, even_ffn_gate, even_ffn_up, even_ffn_down, odd_w_mod, odd_b_mod, odd_norm1, odd_norm2, odd_w_in, odd_q_norm, odd_k_norm, odd_lambda, odd_subln, odd_w_out, odd_router, odd_exp_gate, odd_exp_up, odd_exp_down):
    B, L, D = x.shape
    Lc = ctx.shape[1]
    T = B * L
    row = lambda v: v.reshape(1, -1)
    pad_lanes = lambda v: jnp.pad(v.reshape(1, -1), ((0, 0), (0, LANES - v.size)))

    cond = jnp.concatenate([c, c_ctx[None, :], jnp.zeros((SUBLANES - B - 1, D), F32)], axis=0)

    mod = _adaln(cond, even_w_mod[0], even_b_mod[0]).reshape(SUBLANES, 6, D)
    mod_l, mod_c = mod[:B], mod[B:B + 1]
    w_in = even_w_in[0]
    cut = SSD_INNER + SSD_CONV_CH
    w_in = jnp.concatenate([w_in[:, :cut], w_in[:, cut + 2 * SSD_HEADS:], w_in[:, cut:cut + 2 * SSD_HEADS],
                            jnp.zeros((D, LANES - 2 * SSD_HEADS), F32)], axis=1).astype(BF16)
    g1 = row(even_norm1[0])
    cos_c, sin_c = _ret_rope_tables(jnp.arange(Lc))
    cos_l, sin_l = _ret_rope_tables(Lc + jnp.arange(L))
    dtb = pad_lanes(even_dt_bias[0])
    alog = pad_lanes(even_a_log[0])
    conv_b = row(even_conv_b[0])
    ssd_kw = dict(G=SSD_GROUPS, R=SSD_HEADS // SSD_GROUPS, P=SSD_HEAD_DIM, N=SSD_STATE)
    ret_kw = dict(G=RET_HEADS, R=1, P=RET_V_DIM, N=RET_QK_DIM)

    def mix(xin, modv, cosf, sinf, hs, hr, tm):
        z, xbc, dt, rq, rk, rv, rg = _even_in(xin, modv, g1, w_in, cosf, sinf, tm)
        xs, bmat, cmat, acol, arow, wrow = _ssd_prep(xbc, dt, even_conv_w[0], conv_b, dtb, alog, tm)
        racol, rarow, rwrow = _ret_decay_tables(even_ret_decay[0], xin.shape[1])
        (ysf, ysb, hs_f, hs_b), (yrf, yrb, hr_f, hr_b) = _bidir_scans([
            (cmat, bmat, xs, acol, arow, wrow, hs[0], hs[1], ssd_kw),
            (rq, rk, rv, racol, rarow, rwrow, hr[0], hr[1], ret_kw)])
        return (ysf, ysb, xs, z, yrf, yrb, rg), (hs_f, hs_b), (hr_f, hr_b)

    zs = jnp.zeros((B, SSD_GROUPS, SSD_STATE, SSD_INNER // SSD_GROUPS), F32)
    zr = jnp.zeros((B, RET_HEADS, RET_QK_DIM, RET_V_DIM), F32)
    tm_c = 256
    tm_l = 512
    mixed_c, hs, hr = mix(ctx, mod_c, cos_c, sin_c, (zs, zs), (zr, zr), tm_c)
    mixed_l, _, _ = mix(x, mod_l, cos_l, sin_l, hs, hr, tm_l)

    dvec = row(jnp.repeat(even_d[0], SSD_HEAD_DIM))
    snorm = row(even_ssd_norm[0])
    w_out = even_w_out[0].astype(BF16)
    g2 = row(even_norm2[0])
    wg = even_ffn_gate[0].astype(BF16)
    wu = even_ffn_up[0].astype(BF16)
    wd = even_ffn_down[0].astype(BF16)
    ffn_tf = wg.shape[1]

    xl = _even_out(*mixed_l, x, mod_l, dvec, snorm, w_out, tm_l)
    xl = _ffn(xl, mod_l, g2, wg, wu, wd, tm_l, ffn_tf)
    xc = _even_out(*mixed_c, ctx, mod_c, dvec, snorm, w_out, tm_c)
    xc = _ffn(xc, mod_c, g2, wg, wu, wd, tm_c, ffn_tf)

    mod = _adaln(cond, odd_w_mod[0], odd_b_mod[0]).reshape(SUBLANES, 6, D)
    mod_l, mod_c = mod[:B], mod[B:B + 1]
    lam_init = 0.8 - 0.6 * math.exp(-0.3 * 1)
    lq1, lk1, lq2, lk2 = odd_lambda[0]
    lam = jnp.exp(jnp.sum(lq1 * lk1)) - jnp.exp(jnp.sum(lq2 * lk2)) + lam_init
    bound = 1.02 * DIFF_HEAD_DIM ** 0.5 * jnp.max(jnp.abs(odd_q_norm[0])) * jnp.max(jnp.abs(odd_k_norm[0]))
    lam = jnp.stack([lam, bound - SHIFT_HEADROOM, (bound <= SHIFT_MAX_BOUND).astype(F32),
                     jnp.zeros((), F32)]).reshape(1, 4)
    w_in = odd_w_in[0].astype(BF16)
    g1 = row(odd_norm1[0])
    qn = row(jnp.tile(odd_q_norm[0], LANES // DIFF_HEAD_DIM))
    kn = row(jnp.tile(odd_k_norm[0], LANES // DIFF_HEAD_DIM))
    gid = jnp.arange(MXU_N) // DIFF_HEAD_DIM
    bd = (gid[:, None] == gid[None, :]).astype(BF16)
    cos_l, sin_l = _axial_rope_tables(L)
    cos_c, sin_c = jnp.ones((Lc, LANES), F32), jnp.zeros((Lc, LANES), F32)
    q_l, k_l, v_l = _odd_in(xl, mod_l, g1, w_in, qn, kn, bd, cos_l, sin_l, tm_l)
    _, k_c, v_c = _odd_in(xc, mod_c, g1, w_in, qn, kn, bd, cos_c, sin_c, tm_c)
    o = _attention(lam, q_l, k_l, v_l, k_c, v_c, row(odd_subln[0]), 1.0 - lam_init, 2048, 512, 4)

    w_router = jnp.pad(jnp.concatenate(_split3(odd_router[0]), axis=1), ((0, 0), (0, LANES - 3 * N_EXPERTS)))
    xl, h, route = _odd_out(o, xl, mod_l, row(odd_norm2[0]), odd_w_out[0].astype(BF16), w_router, tm_l)
    moe_tm = min(1024, T)
    slots, tile_expert, tile_valid, tile_src, last_tile, n_tiles = _routing_tables(route, moe_tm)
    hs_sorted = _dispatch(last_tile, slots, h, n_tiles * moe_tm, 512, moe_tm)
    ys_sorted = _experts(tile_expert, tile_valid, tile_src, hs_sorted, odd_exp_gate[0], odd_exp_up[0],
                         odd_exp_down[0], moe_tm, 512)
    return _combine(slots, ys_sorted, xl, mod_l, route, 256)
```

```python
import functools
import math

import jax
import jax.numpy as jnp
from jax import lax
from jax.experimental import pallas as pl
from jax.experimental.pallas import tpu as pltpu

F32 = jnp.float32
BF16 = jnp.bfloat16
LANES = 128
SUBLANES = 8
MXU_N = 256
VMEM_LIMIT = 56 * 1024 * 1024

GRID_W = 64
CHUNK = 128
NORM_EPS = 1e-6
ROPE_THETA = 10000.0
SSD_HEADS = 16
SSD_HEAD_DIM = 64
SSD_INNER = SSD_HEADS * SSD_HEAD_DIM
SSD_GROUPS = 2
SSD_STATE = 128
SSD_BC = SSD_GROUPS * SSD_STATE
SSD_CONV_CH = SSD_INNER + 2 * SSD_BC
RET_HEADS = 4
RET_QK_DIM = 128
RET_V_DIM = 256
RET_QK = RET_HEADS * RET_QK_DIM
RET_INNER = RET_HEADS * RET_V_DIM
DIFF_HEADS = 8
DIFF_HEAD_DIM = 64
DIFF_V_DIM = 2 * DIFF_HEAD_DIM
DIFF_INNER = DIFF_HEADS * DIFF_V_DIM
N_EXPERTS = 8
NEG_BIG = -1e30
SHIFT_HEADROOM = 30.0
SHIFT_MAX_BOUND = 55.0
DMA_ISSUE_UNROLL = 8


def _params(sem):
    return pltpu.CompilerParams(dimension_semantics=sem, vmem_limit_bytes=VMEM_LIMIT)


def _tile(n, pref):
    t = min(n, pref)
    assert n % t == 0, (n, t)
    return t


def _silu(x):
    return x * jax.nn.sigmoid(x)


def _rms(x):
    return x * lax.rsqrt(jnp.mean(x * x, axis=-1, keepdims=True) + NORM_EPS)


def _modulate(x, gain, shift, scale):
    return _rms(x) * gain * (1.0 + scale) + shift


def _dot(a, b):
    return jnp.dot(a, b, preferred_element_type=F32)


def _dot_nt(a, b):
    return lax.dot_general(a, b, (((1,), (1,)), ((), ())), preferred_element_type=F32)


def _split3(a):
    hi = a.astype(BF16)
    r1 = a - hi.astype(F32)
    mid = r1.astype(BF16)
    lo = (r1 - mid.astype(F32)).astype(BF16)
    return hi, mid, lo


def _adaln_kernel(c_ref, w_ref, b_ref, o_ref):
    a = _silu(c_ref[...]).astype(BF16)
    o_ref[...] = _dot(a, w_ref[...].astype(BF16)) + b_ref[...]


def _adaln(cond, w, b):
    R, D = cond.shape
    N = w.shape[1]
    tn = _tile(N, 1024)
    return pl.pallas_call(
        _adaln_kernel,
        grid=(N // tn,),
        in_specs=[pl.BlockSpec((R, D), lambda j: (0, 0)),
                  pl.BlockSpec((D, tn), lambda j: (0, j)),
                  pl.BlockSpec((1, tn), lambda j: (0, j))],
        out_specs=pl.BlockSpec((R, tn), lambda j: (0, j)),
        out_shape=jax.ShapeDtypeStruct((R, N), F32),
        compiler_params=_params(("arbitrary",)),
        name="adaln",
    )(cond, w, b.reshape(1, N))


EV_Z = 0
EV_XBC = EV_Z + SSD_INNER
EV_RQ = EV_XBC + SSD_CONV_CH
EV_RK = EV_RQ + RET_QK
EV_RV = EV_RK + RET_QK
EV_RG = EV_RV + RET_INNER
EV_DT = EV_RG + RET_INNER
EV_END = EV_DT + LANES


def _even_in_kernel(x_ref, mod_ref, gain_ref, w_ref, cos_ref, sin_ref,
                    z_ref, xbc_ref, dt_ref, rq_ref, rk_ref, rv_ref, rg_ref):
    mod = mod_ref[0]
    h = _modulate(x_ref[0], gain_ref[...], mod[0:1], mod[1:2]).astype(BF16)

    def mm(lo, hi):
        return _dot(h, w_ref[:, lo:hi])

    z_ref[0] = mm(EV_Z, EV_XBC).astype(BF16)
    xbc_ref[0] = mm(EV_XBC, EV_RQ)
    cos = cos_ref[...]
    sin = sin_ref[...]
    k_scale = RET_QK_DIM ** -0.5
    qk = mm(EV_RQ, EV_RV)
    for j in range(RET_HEADS):
        lo = j * RET_QK_DIM
        a = qk[:, lo:lo + RET_QK_DIM]
        rq_ref[0, :, lo:lo + RET_QK_DIM] = (a * cos + pltpu.roll(a, RET_QK_DIM // 2, 1) * sin).astype(BF16)
        a = qk[:, RET_QK + lo:RET_QK + lo + RET_QK_DIM]
        a = (a * cos + pltpu.roll(a, RET_QK_DIM // 2, 1) * sin) * k_scale
        rk_ref[0, lo:lo + RET_QK_DIM, :] = a.T.astype(BF16)
    rv_ref[0] = mm(EV_RV, EV_RG).astype(BF16)
    rg_ref[0] = mm(EV_RG, EV_DT).astype(BF16)
    dt_ref[0] = mm(EV_DT, EV_END)


def _even_in(x, mod, gain, w, cosf, sinf, tm):
    B, S, D = x.shape
    tm = _tile(S, tm)
    bm = mod.shape[0]
    mod_map = (lambda b, i: (b, 0, 0)) if bm > 1 else (lambda b, i: (0, 0, 0))
    row = lambda n: pl.BlockSpec((1, tm, n), lambda b, i: (b, i, 0))
    outs = [(SSD_INNER, BF16), (SSD_CONV_CH, F32), (LANES, F32), (RET_QK, BF16), (RET_QK, BF16),
            (RET_INNER, BF16), (RET_INNER, BF16)]
    RK_OUT = 4
    return pl.pallas_call(
        _even_in_kernel,
        grid=(B, S // tm),
        in_specs=[row(D),
                  pl.BlockSpec((1, 6, D), mod_map),
                  pl.BlockSpec((1, D), lambda b, i: (0, 0)),
                  pl.BlockSpec((D, EV_END), lambda b, i: (0, 0)),
                  pl.BlockSpec((tm, LANES), lambda b, i: (i, 0)),
                  pl.BlockSpec((tm, LANES), lambda b, i: (i, 0))],
        out_specs=[pl.BlockSpec((1, n, tm), lambda b, i: (b, 0, i)) if j == RK_OUT else row(n)
                   for j, (n, _) in enumerate(outs)],
        out_shape=[jax.ShapeDtypeStruct((B, n, S) if j == RK_OUT else (B, S, n), dt)
                   for j, (n, dt) in enumerate(outs)],
        compiler_params=_params(("arbitrary", "arbitrary")),
        name="even_in",
    )(x, mod, gain, w, cosf, sinf)


def _ssd_prep_kernel(xbc_ref, prev_ref, next_ref, dt_ref, cw_ref, cb_ref, dtb_ref, alog_ref,
                     xs_ref, bm_ref, cm_ref, acol_ref, arow_ref, wrow_ref, *, tm, chunk):
    i = pl.program_id(1)
    n = pl.num_programs(1)
    brow = lax.broadcasted_iota(jnp.int32, (chunk, 1), 0)
    for cb in range(SSD_CONV_CH // LANES):
        cols = slice(cb * LANES, (cb + 1) * LANES)
        w0, w1, w2 = cw_ref[0:1, cols], cw_ref[1:2, cols], cw_ref[2:3, cols]
        bias = cb_ref[:, cols]
        zero_row = jnp.zeros((1, LANES), F32)
        for rb in range(tm // chunk):
            r0 = rb * chunk
            x = xbc_ref[0, r0:r0 + chunk, cols]
            if rb == 0:
                above = jnp.where(i > 0, prev_ref[0, SUBLANES - 1:SUBLANES, cols], zero_row)
            else:
                above = xbc_ref[0, r0 - 1:r0, cols]
            if rb == tm // chunk - 1:
                below = jnp.where(i < n - 1, next_ref[0, 0:1, cols], zero_row)
            else:
                below = xbc_ref[0, r0 + chunk:r0 + chunk + 1, cols]
            xp = jnp.where(brow == 0, above, pltpu.roll(x, 1, 0))
            xn = jnp.where(brow == chunk - 1, below, pltpu.roll(x, chunk - 1, 0))
            y = _silu(xp * w0 + x * w1 + xn * w2 + bias)
            if cb * LANES < SSD_INNER:
                xs_ref[0, r0:r0 + chunk, cols] = y.astype(BF16)
            elif cb * LANES < SSD_INNER + SSD_BC:
                lo = cb * LANES - SSD_INNER
                bm_ref[0, lo:lo + LANES, r0:r0 + chunk] = y.T.astype(BF16)
            else:
                lo = cb * LANES - SSD_INNER - SSD_BC
                cm_ref[0, r0:r0 + chunk, lo:lo + LANES] = y.astype(BF16)

    rows = lax.broadcasted_iota(jnp.int32, (tm, 1), 0)
    t = dt_ref[0] + dtb_ref[...]
    dt = jnp.maximum(t, 0.0) + jnp.log(1.0 + jnp.exp(-jnp.abs(t)))
    la = -dt * jnp.exp(alog_ref[...])
    rmod = rows % chunk
    fwd = la
    rev = la
    sh = 1
    while sh < chunk:
        fwd = fwd + jnp.where(rmod >= sh, pltpu.roll(fwd, sh, 0), 0.0)
        rev = rev + jnp.where(rmod < chunk - sh, pltpu.roll(rev, tm - sh, 0), 0.0)
        sh *= 2
    lane = lax.broadcasted_iota(jnp.int32, (1, LANES), 1)
    acc = jnp.where(lane < SSD_HEADS, fwd, rev)
    acc_t = acc.T
    dt_t = dt.T
    hp = SSD_HEADS // SSD_GROUPS
    for j in range(2 * SSD_GROUPS):
        acol_ref[0, j] = acc[:, j * hp:(j + 1) * hp]
        arow_ref[0, j] = acc_t[j * hp:(j + 1) * hp, :]
        wrow_ref[0, j] = dt_t[j * hp:(j + 1) * hp, :]


def _ssd_prep(xbc, dt, conv_w, conv_b, dtb, alog, tm):
    B, S, _ = xbc.shape
    tm = _tile(S, tm)
    assert tm % CHUNK == 0
    nh = S // SUBLANES
    hp = SSD_HEADS // SSD_GROUPS
    gd = 2 * SSD_GROUPS
    row = lambda n: pl.BlockSpec((1, tm, n), lambda b, i: (b, i, 0))
    const = lambda r, n: pl.BlockSpec((r, n), lambda b, i: (0, 0))
    return pl.pallas_call(
        functools.partial(_ssd_prep_kernel, tm=tm, chunk=CHUNK),
        grid=(B, S // tm),
        in_specs=[row(SSD_CONV_CH),
                  pl.BlockSpec((1, SUBLANES, SSD_CONV_CH),
                               lambda b, i: (b, jnp.maximum(i * (tm // SUBLANES) - 1, 0), 0)),
                  pl.BlockSpec((1, SUBLANES, SSD_CONV_CH),
                               lambda b, i: (b, jnp.minimum((i + 1) * (tm // SUBLANES), nh - 1), 0)),
                  row(LANES),
                  const(3, SSD_CONV_CH), const(1, SSD_CONV_CH), const(1, LANES), const(1, LANES)],
        out_specs=[row(SSD_INNER), pl.BlockSpec((1, SSD_BC, tm), lambda b, i: (b, 0, i)), row(SSD_BC),
                   pl.BlockSpec((1, gd, tm, hp), lambda b, i: (b, 0, i, 0)),
                   pl.BlockSpec((1, gd, hp, tm), lambda b, i: (b, 0, 0, i)),
                   pl.BlockSpec((1, gd, hp, tm), lambda b, i: (b, 0, 0, i))],
        out_shape=[jax.ShapeDtypeStruct((B, S, SSD_INNER), BF16),
                   jax.ShapeDtypeStruct((B, SSD_BC, S), BF16),
                   jax.ShapeDtypeStruct((B, S, SSD_BC), BF16),
                   jax.ShapeDtypeStruct((B, gd, S, hp), F32),
                   jax.ShapeDtypeStruct((B, gd, hp, S), F32),
                   jax.ShapeDtypeStruct((B, gd, hp, S), F32)],
        compiler_params=_params(("arbitrary", "arbitrary")),
        name="ssd_prep",
    )(xbc, xbc, xbc, dt, conv_w, conv_b, dtb, alog)


def _scan_direction(q_ref, k_ref, v_ref, ac_ref, ar_ref, wr_ref, y_ref, h_scr, d, *, G, R, P, N):
    reverse = d == 1
    C = q_ref.shape[1]
    li = lax.broadcasted_iota(jnp.int32, (C, C), 0)
    si = lax.broadcasted_iota(jnp.int32, (C, C), 1)
    mask = (si >= li) if reverse else (li >= si)
    far = 0 if reverse else C - 1
    width = max(P, LANES)
    heads_per = width // P
    lane_head = lax.broadcasted_iota(jnp.int32, (1, width), 1) // P
    RP = R * P
    for g in range(G):
        q = q_ref[0, :, g * N:(g + 1) * N]
        k = k_ref[0, g * N:(g + 1) * N, :]
        s = _dot(q, k)
        k_t = k.astype(F32)
        ac = ac_ref[0, g]
        ar = ar_ref[0, g]
        wr = wr_ref[0, g]
        for u in range(RP // width):
            lo = u * width
            v = v_ref[0, :, g * RP + lo:g * RP + lo + width].astype(BF16)
            h_old = h_scr[d, g, :, lo:lo + width]
            y_diag = None
            for j in range(heads_per):
                r = u * heads_per + j
                a_col = ac[:, r:r + 1]
                a_row = ar[r:r + 1, :]
                w_row = wr[r:r + 1, :]
                tot = a_row[:, far:far + 1]
                decay = jnp.exp(jnp.where(mask, a_col - a_row, NEG_BIG))
                m = (s * decay * w_row).astype(BF16)
                kw = (k_t * (jnp.exp(tot - a_row) * w_row)).astype(BF16)
                yd = _dot(m, v)
                st = _dot(kw, v)
                e_col = jnp.exp(a_col)
                e_tot = jnp.exp(tot)
                if y_diag is None:
                    y_diag, state, col_scale, tot_scale = yd, st, e_col, e_tot
                else:
                    sel = lane_head == j
                    y_diag = jnp.where(sel, yd, y_diag)
                    state = jnp.where(sel, st, state)
                    col_scale = jnp.where(sel, e_col, col_scale)
                    tot_scale = jnp.where(sel, e_tot, tot_scale)
            y = y_diag + col_scale * _dot(q, h_old.astype(BF16))
            y_ref[0, :, g * RP + lo:g * RP + lo + width] = y.astype(y_ref.dtype)
            h_scr[d, g, :, lo:lo + width] = tot_scale * h_old + state


SCAN_INS = 14
SCAN_OUTS = 4


def _scan_kernel(*refs, families):
    n = len(families)
    ins = [refs[SCAN_INS * i:SCAN_INS * (i + 1)] for i in range(n)]
    outs = [refs[SCAN_INS * n + SCAN_OUTS * i:SCAN_INS * n + SCAN_OUTS * (i + 1)] for i in range(n)]
    scrs = refs[(SCAN_INS + SCAN_OUTS) * n:]
    c = pl.program_id(1)

    @pl.when(c == 0)
    def _():
        for fam_in, h_scr in zip(ins, scrs):
            h_scr[0] = fam_in[12][0]
            h_scr[1] = fam_in[13][0]

    for dims, fam_in, fam_out, h_scr in zip(families, ins, outs, scrs):
        qf, kf, vf, qb, kb, vb, acf, arf, wrf, acb, arb, wrb = fam_in[:12]
        _scan_direction(qf, kf, vf, acf, arf, wrf, fam_out[0], h_scr, 0, **dims)
        _scan_direction(qb, kb, vb, acb, arb, wrb, fam_out[1], h_scr, 1, **dims)

    @pl.when(c == pl.num_programs(1) - 1)
    def _():
        for fam_out, h_scr in zip(outs, scrs):
            fam_out[2][0] = h_scr[0]
            fam_out[3][0] = h_scr[1]


def _bidir_scans(families):
    B, S, _ = families[0][0].shape
    C = _tile(S, CHUNK)
    nc = S // C
    fwd = lambda c: c
    rev = lambda c: nc - 1 - c
    in_specs, out_specs, out_shape, scratch, args, dims_list = [], [], [], [], [], []
    for q, k, v, acol, arow, wrow, h0_f, h0_b, dims in families:
        G, R, P, N = dims["G"], dims["R"], dims["P"], dims["N"]
        RP = R * P
        bx = (lambda b: b) if acol.shape[0] > 1 else (lambda b: 0)
        seq = lambda n, ci: pl.BlockSpec((1, C, n), lambda b, c, ci=ci: (b, ci(c), 0))
        col = lambda d, ci, G=G, R=R, bx=bx: pl.BlockSpec((1, G, C, R), lambda b, c: (bx(b), d, ci(c), 0))
        rowt = lambda d, ci, G=G, R=R, bx=bx: pl.BlockSpec((1, G, R, C), lambda b, c: (bx(b), d, 0, ci(c)))
        state = pl.BlockSpec((1, G, N, RP), lambda b, c: (b, 0, 0, 0))
        seq_t = lambda n, ci: pl.BlockSpec((1, n, C), lambda b, c, ci=ci: (b, 0, ci(c)))
        in_specs += [seq(G * N, fwd), seq_t(G * N, fwd), seq(G * RP, fwd),
                     seq(G * N, rev), seq_t(G * N, rev), seq(G * RP, rev),
                     col(0, fwd), rowt(0, fwd), rowt(0, fwd), col(1, rev), rowt(1, rev), rowt(1, rev),
                     state, state]
        out_specs += [seq(G * RP, fwd), seq(G * RP, rev), state, state]
        out_shape += [jax.ShapeDtypeStruct((B, S, G * RP), BF16), jax.ShapeDtypeStruct((B, S, G * RP), BF16),
                      jax.ShapeDtypeStruct((B, G, N, RP), F32), jax.ShapeDtypeStruct((B, G, N, RP), F32)]
        scratch.append(pltpu.VMEM((2, G, N, RP), F32))
        args += [q, k, v, q, k, v, acol, arow, wrow, acol, arow, wrow, h0_f, h0_b]
        dims_list.append(dims)
    res = pl.pallas_call(
        functools.partial(_scan_kernel, families=tuple(dims_list)),
        grid=(B, nc),
        in_specs=in_specs,
        out_specs=out_specs,
        out_shape=out_shape,
        scratch_shapes=scratch,
        compiler_params=_params(("arbitrary", "arbitrary")),
        name="bidir_scans",
    )(*args)
    return [res[SCAN_OUTS * i:SCAN_OUTS * (i + 1)] for i in range(len(families))]


def _even_out_kernel(ysf_ref, ysb_ref, xs_ref, z_ref, yrf_ref, yrb_ref, rg_ref, x_ref, mod_ref, dvec_ref, sn_ref,
                     w_ref, o_ref):
    f32 = lambda ref: ref[0].astype(F32)
    y = f32(ysf_ref) + f32(ysb_ref) + f32(xs_ref) * dvec_ref[...]
    y = _rms(y * _silu(f32(z_ref))) * sn_ref[...]
    out = _dot(y.astype(BF16), w_ref[0:SSD_INNER, :])
    yr = f32(yrf_ref) + f32(yrb_ref)
    rg = _silu(f32(rg_ref))
    for j in range(RET_HEADS):
        lo = j * RET_V_DIM
        r = _rms(yr[:, lo:lo + RET_V_DIM]) * rg[:, lo:lo + RET_V_DIM]
        out = out + _dot(r.astype(BF16), w_ref[SSD_INNER + lo:SSD_INNER + lo + RET_V_DIM, :])
    o_ref[0] = x_ref[0] + mod_ref[0][2:3] * out


def _even_out(ysf, ysb, xs, z, yrf, yrb, rg, x, mod, dvec, ssd_norm, w, tm):
    B, S, D = x.shape
    tm = _tile(S, tm)
    bm = mod.shape[0]
    mod_map = (lambda b, i: (b, 0, 0)) if bm > 1 else (lambda b, i: (0, 0, 0))
    row = lambda n: pl.BlockSpec((1, tm, n), lambda b, i: (b, i, 0))
    const = lambda r, n: pl.BlockSpec((r, n), lambda b, i: (0, 0))
    return pl.pallas_call(
        _even_out_kernel,
        grid=(B, S // tm),
        in_specs=[row(SSD_INNER), row(SSD_INNER), row(SSD_INNER), row(SSD_INNER),
                  row(RET_INNER), row(RET_INNER), row(RET_INNER), row(D),
                  pl.BlockSpec((1, 6, D), mod_map),
                  const(1, SSD_INNER), const(1, SSD_INNER), const(SSD_INNER + RET_INNER, D)],
        out_specs=row(D),
        out_shape=jax.ShapeDtypeStruct((B, S, D), F32),
        compiler_params=_params(("arbitrary", "arbitrary")),
        name="even_out",
    )(ysf, ysb, xs, z, yrf, yrb, rg, x, mod, dvec, ssd_norm, w)


def _ffn_kernel(x_ref, mod_ref, gain_ref, wg_ref, wu_ref, wd_ref, o_ref, h_scr, acc_scr):
    f = pl.program_id(2)
    mod = mod_ref[0]

    @pl.when(f == 0)
    def _():
        h_scr[...] = _modulate(x_ref[0], gain_ref[...], mod[3:4], mod[4:5]).astype(BF16)
        acc_scr[...] = jnp.zeros_like(acc_scr)

    h = h_scr[...]
    a = (_silu(_dot(h, wg_ref[...])) * _dot(h, wu_ref[...])).astype(BF16)
    acc_scr[...] += _dot(a, wd_ref[...])

    @pl.when(f == pl.num_programs(2) - 1)
    def _():
        o_ref[0] = x_ref[0] + mod[5:6] * acc_scr[...]


def _ffn(x, mod, gain, wg, wu, wd, tm, tf):
    B, S, D = x.shape
    F = wg.shape[1]
    tm = _tile(S, tm)
    tf = _tile(F, tf)
    bm = mod.shape[0]
    mod_map = (lambda b, i, f: (b, 0, 0)) if bm > 1 else (lambda b, i, f: (0, 0, 0))
    resident = dict(pipeline_mode=pl.Buffered(1)) if tf == F else {}
    return pl.pallas_call(
        _ffn_kernel,
        grid=(B, S // tm, F // tf),
        in_specs=[pl.BlockSpec((1, tm, D), lambda b, i, f: (b, i, 0)),
                  pl.BlockSpec((1, 6, D), mod_map),
                  pl.BlockSpec((1, D), lambda b, i, f: (0, 0)),
                  pl.BlockSpec((D, tf), lambda b, i, f: (0, f), **resident),
                  pl.BlockSpec((D, tf), lambda b, i, f: (0, f), **resident),
                  pl.BlockSpec((tf, D), lambda b, i, f: (f, 0), **resident)],
        out_specs=pl.BlockSpec((1, tm, D), lambda b, i, f: (b, i, 0)),
        out_shape=jax.ShapeDtypeStruct((B, S, D), F32),
        scratch_shapes=[pltpu.VMEM((tm, D), BF16), pltpu.VMEM((tm, D), F32)],
        compiler_params=_params(("arbitrary", "arbitrary", "arbitrary")),
        name="ffn_dense",
    )(x, mod, gain, wg, wu, wd)


def _odd_in_kernel(x_ref, mod_ref, gain_ref, w_ref, qn_ref, kn_ref, bd_ref, cos_ref, sin_ref,
                   q_ref, k_ref, v_ref):
    mod = mod_ref[0]
    h = _modulate(x_ref[0], gain_ref[...], mod[0:1], mod[1:2]).astype(BF16)
    cos = cos_ref[...]
    sin = sin_ref[...]
    bd = bd_ref[...]
    lane = lax.broadcasted_iota(jnp.int32, (1, LANES), 1)
    quarter = DIFF_HEAD_DIM // 4
    first = (lane % (2 * quarter)) < quarter
    for base, n_ref, o_ref, scale in ((0, qn_ref, q_ref, DIFF_HEAD_DIM ** -0.5),
                                      (DIFF_INNER, kn_ref, k_ref, 1.0)):
        for j in range(DIFF_INNER // MXU_N):
            lo = j * MXU_N
            a = _dot(h, w_ref[:, base + lo:base + lo + MXU_N])
            sq = a * a
            hi = sq.astype(BF16)
            rest = (sq - hi.astype(F32)).astype(BF16)
            ms = (_dot(hi, bd) + _dot(rest, bd)) * (1.0 / DIFF_HEAD_DIM)
            a = a * lax.rsqrt(ms + NORM_EPS)
            for c in range(MXU_N // LANES):
                b = a[:, c * LANES:(c + 1) * LANES] * n_ref[...]
                partner = jnp.where(first, pltpu.roll(b, LANES - quarter, 1), pltpu.roll(b, quarter, 1))
                o_ref[0, :, lo + c * LANES:lo + (c + 1) * LANES] = ((b * cos + partner * sin) * scale).astype(BF16)
    v_ref[0] = _dot(h, w_ref[:, 2 * DIFF_INNER:3 * DIFF_INNER]).astype(BF16)


def _odd_in(x, mod, gain, w, qn, kn, bd, cosf, sinf, tm):
    B, S, D = x.shape
    tm = _tile(S, tm)
    bm = mod.shape[0]
    mod_map = (lambda b, i: (b, 0, 0)) if bm > 1 else (lambda b, i: (0, 0, 0))
    row = lambda n: pl.BlockSpec((1, tm, n), lambda b, i: (b, i, 0))
    const = lambda r, n: pl.BlockSpec((r, n), lambda b, i: (0, 0))
    return pl.pallas_call(
        _odd_in_kernel,
        grid=(B, S // tm),
        in_specs=[row(D), pl.BlockSpec((1, 6, D), mod_map), const(1, D), const(D, 3 * DIFF_INNER),
                  const(1, LANES), const(1, LANES), const(MXU_N, MXU_N),
                  pl.BlockSpec((tm, LANES), lambda b, i: (i, 0)),
                  pl.BlockSpec((tm, LANES), lambda b, i: (i, 0))],
        out_specs=[row(DIFF_INNER)] * 3,
        out_shape=[jax.ShapeDtypeStruct((B, S, DIFF_INNER), BF16)] * 3,
        compiler_params=_params(("arbitrary", "arbitrary")),
        name="odd_in",
    )(x, mod, gain, w, qn, kn, bd, cosf, sinf)


def _attn_kernel(sc_ref, q_ref, kl_ref, vl_ref, kc_ref, vc_ref, sub_ref, o_ref,
                 q2_scr, m_scr, l_scr, acc_scr, *, tq, tk, unroll, out_scale):
    q = q_ref[0]
    lane = lax.broadcasted_iota(jnp.int32, (1, LANES), 1)
    lo_half = lane < DIFF_HEAD_DIM
    zero = jnp.zeros_like(q)
    q2_scr[0:tq, :] = jnp.where(lo_half, q, zero)
    q2_scr[tq:2 * tq, :] = jnp.where(lo_half, zero, q)
    l_scr[...] = jnp.zeros_like(l_scr)
    acc_scr[...] = jnp.zeros_like(acc_scr)
    n_lat = kl_ref.shape[1] // tk
    use_shift = sc_ref[0, 2] > 0.0

    def over_keys(update, unroll):
        def body(j, carry):
            start = pl.multiple_of(j * tk, tk)
            update(kl_ref[0, pl.ds(start, tk), :], vl_ref[0, pl.ds(start, tk), :])
            return carry

        lax.fori_loop(0, n_lat, body, 0, unroll=unroll)
        update(kc_ref[0], vc_ref[0])

    @pl.when(use_shift)
    def _():
        shift = sc_ref[0, 1]

        def update(k, v):
            p = jnp.exp(_dot_nt(q2_scr[...], k) - shift)
            part = p[:, 0:LANES]
            for c in range(1, p.shape[1] // LANES):
                part = part + p[:, c * LANES:(c + 1) * LANES]
            l_scr[...] += part
            acc_scr[...] += _dot(p.astype(BF16), v)

        over_keys(update, unroll)
        acc_scr[...] = acc_scr[...] / jnp.sum(l_scr[...], axis=1, keepdims=True)

    @pl.when(jnp.logical_not(use_shift))
    def _():
        m_scr[...] = jnp.full_like(m_scr, -jnp.inf)

        def update(k, v):
            s = _dot_nt(q2_scr[...], k)
            m_prev = m_scr[...]
            m_new = jnp.maximum(m_prev, jnp.max(s, axis=1, keepdims=True))
            alpha = jnp.exp(m_prev - m_new)
            p = jnp.exp(s - m_new[:, 0:1])
            l_scr[...] = alpha * l_scr[...] + jnp.sum(p, axis=1, keepdims=True)
            acc_scr[...] = alpha * acc_scr[...] + _dot(p.astype(BF16), v)
            m_scr[...] = m_new

        over_keys(update, 1)
        acc_scr[...] = acc_scr[...] / l_scr[...]

    o = acc_scr[0:tq, :] - sc_ref[0, 0] * acc_scr[tq:2 * tq, :]
    o_ref[0] = (_rms(o) * sub_ref[...] * out_scale).astype(BF16)


def _attention(lam, q, k_lat, v_lat, k_ctx, v_ctx, subln, out_scale, tq, tk, unroll):
    B, L, _ = q.shape
    Lc = k_ctx.shape[1]
    tq = _tile(L, tq)
    tk = _tile(L, tk)
    unroll = min(unroll, L // tk)
    head = lambda n: pl.BlockSpec((1, n, LANES), lambda b, h, i: (b, 0, h))
    return pl.pallas_call(
        functools.partial(_attn_kernel, tq=tq, tk=tk, unroll=unroll, out_scale=out_scale),
        grid=(B, DIFF_HEADS, L // tq),
        in_specs=[pl.BlockSpec(memory_space=pltpu.SMEM),
                  pl.BlockSpec((1, tq, LANES), lambda b, h, i: (b, i, h)),
                  head(L), head(L), head(Lc), head(Lc),
                  pl.BlockSpec((1, LANES), lambda b, h, i: (0, 0))],
        out_specs=pl.BlockSpec((1, tq, LANES), lambda b, h, i: (b, i, h)),
        out_shape=jax.ShapeDtypeStruct((B, L, DIFF_INNER), BF16),
        scratch_shapes=[pltpu.VMEM((2 * tq, LANES), BF16), pltpu.VMEM((2 * tq, LANES), F32),
                        pltpu.VMEM((2 * tq, LANES), F32), pltpu.VMEM((2 * tq, LANES), F32)],
        compiler_params=_params(("arbitrary", "arbitrary", "arbitrary")),
        name="diff_attn",
    )(lam, q, k_lat, v_lat, k_ctx, v_ctx, subln)


def _odd_out_kernel(o_ref, x_ref, mod_ref, gain_ref, w_ref, wr_ref, xo_ref, h_ref, route_ref, rt_ref):
    mod = mod_ref[0]
    x1 = x_ref[0] + mod[2:3] * _dot(o_ref[0], w_ref[...])
    xo_ref[0] = x1
    h = _modulate(x1, gain_ref[...], mod[3:4], mod[4:5])
    h_ref[...] = h
    wr = wr_ref[...]
    h_hi, h_mid, h_lo = _split3(h)
    parts = _dot(h_lo, wr) + _dot(h_mid, wr) + _dot(h_hi, wr)
    logits = (pltpu.roll(parts, LANES - 2 * N_EXPERTS, 1) + pltpu.roll(parts, LANES - N_EXPERTS, 1)) + parts
    lane = lax.broadcasted_iota(jnp.int32, logits.shape, 1)
    lg = jnp.where(lane < N_EXPERTS, logits, -jnp.inf)
    v1 = jnp.max(lg, axis=1, keepdims=True)
    i1 = jnp.min(jnp.where(lg == v1, lane, LANES), axis=1, keepdims=True)
    lg2 = jnp.where(lane == i1, -jnp.inf, lg)
    v2 = jnp.max(lg2, axis=1, keepdims=True)
    i2 = jnp.min(jnp.where(lg2 == v2, lane, LANES), axis=1, keepdims=True)
    e = jnp.exp(v2 - v1)
    w1 = 1.0 / (1.0 + e)
    w2 = e * w1
    route = jnp.where(lane == 0, i1.astype(F32),
                      jnp.where(lane == 1, i2.astype(F32),
                                jnp.where(lane == 2, w1, jnp.where(lane == 3, w2, 0.0))))
    route_ref[...] = route
    rt_ref[...] = route.T[0:SUBLANES, :]


def _odd_out(o, x, mod, gain, w, w_router, tm):
    B, S, D = x.shape
    tm = _tile(S, tm)
    n = S // tm
    row = lambda n: pl.BlockSpec((1, tm, n), lambda b, i: (b, i, 0))
    flat = lambda w: pl.BlockSpec((tm, w), lambda b, i: (b * n + i, 0))
    const = lambda r, n: pl.BlockSpec((r, n), lambda b, i: (0, 0))
    return pl.pallas_call(
        _odd_out_kernel,
        grid=(B, n),
        in_specs=[row(DIFF_INNER), row(D), pl.BlockSpec((1, 6, D), lambda b, i: (b, 0, 0)),
                  const(1, D), const(DIFF_INNER, D), const(D, LANES)],
        out_specs=[row(D), flat(D), flat(LANES), pl.BlockSpec((SUBLANES, tm), lambda b, i: (0, b * n + i))],
        out_shape=[jax.ShapeDtypeStruct((B, S, D), F32), jax.ShapeDtypeStruct((B * S, D), F32),
                   jax.ShapeDtypeStruct((B * S, LANES), F32), jax.ShapeDtypeStruct((SUBLANES, B * S), F32)],
        compiler_params=_params(("arbitrary", "arbitrary")),
        name="odd_out",
    )(o, x, mod, gain, w, w_router)


def _row_copy(src, src_row, dst, dst_row, sem):
    return pltpu.make_async_copy(src.at[pl.ds(src_row, 1), :], dst.at[pl.ds(dst_row, 1), :], sem)


def _dispatch_kernel(last_tile_ref, slot_ref, h_ref, hs_ref, zero_buf, sem, zero_sem, *, tt, tm):
    @pl.when(pl.program_id(0) == 0)
    def _():
        zero_buf[...] = jnp.zeros_like(zero_buf)
        for e in range(2 * N_EXPERTS):
            start = pl.multiple_of(last_tile_ref[e] * tm, tm)
            clear = pltpu.make_async_copy(zero_buf, hs_ref.at[pl.ds(start, tm), :], zero_sem)
            clear.start()
            clear.wait()

    def issue(r, carry):
        _row_copy(h_ref, r, hs_ref, slot_ref[0, 0, r], sem).start()
        _row_copy(h_ref, r, hs_ref, slot_ref[0, 1, r], sem).start()
        return carry

    lax.fori_loop(0, tt, issue, 0, unroll=DMA_ISSUE_UNROLL)
    for _ in range(2):
        pltpu.make_async_copy(h_ref, hs_ref.at[pl.ds(0, tt), :], sem).wait()


def _dispatch(last_tile, slots, h, n_rows, tt, tm):
    T, D = h.shape
    tt = _tile(T, tt)
    grid_spec = pltpu.PrefetchScalarGridSpec(
        num_scalar_prefetch=1,
        grid=(T // tt,),
        in_specs=[pl.BlockSpec((1, 2, tt), lambda i, lt: (i, 0, 0), memory_space=pltpu.SMEM),
                  pl.BlockSpec((tt, D), lambda i, lt: (i, 0))],
        out_specs=pl.BlockSpec(memory_space=pl.ANY),
        scratch_shapes=[pltpu.VMEM((tm, D), F32), pltpu.SemaphoreType.DMA(()), pltpu.SemaphoreType.DMA(())])
    return pl.pallas_call(
        functools.partial(_dispatch_kernel, tt=tt, tm=tm),
        grid_spec=grid_spec,
        out_shape=jax.ShapeDtypeStruct((n_rows, D), F32),
        compiler_params=_params(("arbitrary",)),
        name="moe_dispatch",
    )(last_tile, _slot_blocks(slots, tt), h)


def _expert_kernel(te_ref, tv_ref, ts_ref, hs_ref, wg_ref, wu_ref, wd_ref, ys_ref, h_scr, acc_scr):
    del te_ref, ts_ref
    i = pl.program_id(0)
    f = pl.program_id(1)

    @pl.when(tv_ref[i] > 0)
    def _():
        @pl.when(f == 0)
        def _():
            h_scr[...] = hs_ref[...].astype(BF16)
            acc_scr[...] = jnp.zeros_like(acc_scr)

        h = h_scr[...]
        g = _dot(h, wg_ref[0].astype(BF16))
        u = _dot(h, wu_ref[0].astype(BF16))
        acc_scr[...] += _dot((_silu(g) * u).astype(BF16), wd_ref[0].astype(BF16))

        @pl.when(f == pl.num_programs(1) - 1)
        def _():
            ys_ref[...] = acc_scr[...]

    @pl.when((tv_ref[i] == 0) & (f == pl.num_programs(1) - 1))
    def _():
        ys_ref[...] = jnp.zeros_like(ys_ref)


def _experts(tile_expert, tile_valid, tile_src, hs, wg, wu, wd, tm, tf):
    n_rows, D = hs.shape
    F = wg.shape[2]
    tf = _tile(F, tf)
    nf = F // tf
    fi = lambda i, f, tv: jnp.where(tv[i] > 0, f, nf - 1)
    grid_spec = pltpu.PrefetchScalarGridSpec(
        num_scalar_prefetch=3,
        grid=(n_rows // tm, nf),
        in_specs=[pl.BlockSpec((tm, D), lambda i, f, te, tv, ts: (ts[i], 0)),
                  pl.BlockSpec((1, D, tf), lambda i, f, te, tv, ts: (te[i], 0, fi(i, f, tv))),
                  pl.BlockSpec((1, D, tf), lambda i, f, te, tv, ts: (te[i], 0, fi(i, f, tv))),
                  pl.BlockSpec((1, tf, D), lambda i, f, te, tv, ts: (te[i], fi(i, f, tv), 0))],
        out_specs=pl.BlockSpec((tm, D), lambda i, f, te, tv, ts: (i, 0)),
        scratch_shapes=[pltpu.VMEM((tm, D), BF16), pltpu.VMEM((tm, D), F32)])
    return pl.pallas_call(
        _expert_kernel,
        grid_spec=grid_spec,
        out_shape=jax.ShapeDtypeStruct((n_rows, D), F32),
        compiler_params=_params(("arbitrary", "arbitrary")),
        name="moe_experts",
    )(tile_expert, tile_valid, tile_src, hs, wg, wu, wd)


def _combine_kernel(slot_ref, ys_ref, x_ref, mod_ref, route_ref, o_ref, buf0, buf1, sem, *, tt):
    def issue(r, carry):
        _row_copy(ys_ref, slot_ref[0, 0, r], buf0, r, sem).start()
        _row_copy(ys_ref, slot_ref[0, 1, r], buf1, r, sem).start()
        return carry

    lax.fori_loop(0, tt, issue, 0, unroll=DMA_ISSUE_UNROLL)
    pltpu.make_async_copy(ys_ref.at[pl.ds(0, tt), :], buf0, sem).wait()
    pltpu.make_async_copy(ys_ref.at[pl.ds(0, tt), :], buf1, sem).wait()
    route = route_ref[...]
    y = route[:, 2:3] * buf0[...] + route[:, 3:4] * buf1[...]
    o_ref[0] = x_ref[0] + mod_ref[0][5:6] * y


def _combine(slots, ys, x, mod, route, tt):
    B, S, D = x.shape
    tt = _tile(S, tt)
    n = S // tt
    return pl.pallas_call(
        functools.partial(_combine_kernel, tt=tt),
        grid=(B, n),
        in_specs=[pl.BlockSpec((1, 2, tt), lambda b, i: (b * n + i, 0, 0), memory_space=pltpu.SMEM),
                  pl.BlockSpec(memory_space=pl.ANY),
                  pl.BlockSpec((1, tt, D), lambda b, i: (b, i, 0)),
                  pl.BlockSpec((1, 6, D), lambda b, i: (b, 0, 0)),
                  pl.BlockSpec((tt, LANES), lambda b, i: (b * n + i, 0))],
        out_specs=pl.BlockSpec((1, tt, D), lambda b, i: (b, i, 0)),
        out_shape=jax.ShapeDtypeStruct((B, S, D), F32),
        scratch_shapes=[pltpu.VMEM((tt, D), F32), pltpu.VMEM((tt, D), F32), pltpu.SemaphoreType.DMA(())],
        compiler_params=_params(("arbitrary", "arbitrary")),
        name="moe_combine",
    )(_slot_blocks(slots, tt), ys, x, mod, route)


def _slot_blocks(slots, tt):
    T = slots.shape[1]
    return slots.reshape(2, T // tt, tt).transpose(1, 0, 2)


def _routing_tables(route, tm):
    T = route.shape[1]
    experts = route[0:2].astype(jnp.int32).reshape(2 * T)
    onehot = (experts[None, :] == jnp.arange(N_EXPERTS, dtype=jnp.int32)[:, None]).astype(jnp.int32)
    csum = jnp.cumsum(onehot, axis=1)
    rank = jnp.sum((csum - onehot) * onehot, axis=0)
    counts = csum[:, -1]
    tiles = (counts + tm - 1) // tm
    tile_end = jnp.cumsum(tiles)
    offsets = (tile_end - tiles) * tm
    slots = (jnp.sum(onehot * offsets[:, None], axis=0) + rank).reshape(2, T)
    n_tiles = (2 * T) // tm + N_EXPERTS
    ids = jnp.arange(n_tiles, dtype=jnp.int32)
    tile_expert = jnp.minimum(jnp.sum((ids[:, None] >= tile_end[None, :]).astype(jnp.int32), axis=1),
                              N_EXPERTS - 1)
    last_used = jnp.minimum(jnp.sum((tile_end[-1] - 1 >= tile_end).astype(jnp.int32)), N_EXPERTS - 1)
    tile_valid = (ids < tile_end[-1]).astype(jnp.int32)
    tile_expert = jnp.where(tile_valid > 0, tile_expert, last_used)
    tile_src = jnp.minimum(ids, tile_end[-1] - 1)
    last_tile = jnp.maximum(tile_end - 1, 0)
    tail = ids[n_tiles - N_EXPERTS:]
    last_tile = jnp.concatenate([last_tile, jnp.where(tail >= tile_end[-1], tail, last_tile[0])])
    return (slots.astype(jnp.int32), tile_expert.astype(jnp.int32), tile_valid, tile_src.astype(jnp.int32),
            last_tile.astype(jnp.int32), n_tiles)


def _rope_angles(pos, dim):
    inv = ROPE_THETA ** (-jnp.arange(dim // 2, dtype=F32) / (dim // 2))
    return pos.astype(F32)[:, None] * inv[None, :]


def _ret_rope_tables(pos):
    ang = _rope_angles(pos, RET_QK_DIM)
    cos, sin = jnp.cos(ang), jnp.sin(ang)
    return jnp.concatenate([cos, cos], axis=1), jnp.concatenate([-sin, sin], axis=1)


def _axial_rope_tables(L):
    n_rows = L // GRID_W
    row = jnp.broadcast_to(jnp.arange(n_rows)[:, None], (n_rows, GRID_W)).reshape(-1)
    col = jnp.broadcast_to(jnp.arange(GRID_W)[None, :], (n_rows, GRID_W)).reshape(-1)
    ar = _rope_angles(row, DIFF_HEAD_DIM // 2)
    ac = _rope_angles(col, DIFF_HEAD_DIM // 2)
    cos = jnp.concatenate([jnp.cos(ar), jnp.cos(ar), jnp.cos(ac), jnp.cos(ac)], axis=1)
    sin = jnp.concatenate([-jnp.sin(ar), jnp.sin(ar), -jnp.sin(ac), jnp.sin(ac)], axis=1)
    return jnp.tile(cos, (1, LANES // DIFF_HEAD_DIM)), jnp.tile(sin, (1, LANES // DIFF_HEAD_DIM))


def _ret_decay_tables(ret_decay, S):
    C = min(S, CHUNK)
    lam = -jnp.exp(ret_decay.astype(F32))
    pos = jnp.arange(S) % C
    steps = jnp.stack([pos + 1, C - pos]).astype(F32)
    a = (lam[:, :, None] * steps[:, None, :]).reshape(1, 2 * RET_HEADS, S)
    return a[:, :, :, None], a[:, :, None, :], jnp.ones((1, 2 * RET_HEADS, 1, S), F32)


def kernel(x, c, ctx, c_ctx, even_w_mod, even_b_mod, even_norm1, even_norm2, even_w_in, even_conv_w, even_conv_b, even_dt_bias, even_a_log, even_d, even_ssd_norm, even_ret_decay, even_w_out, even_ffn_gate, even_ffn_up, even_ffn_down, odd_w_mod, odd_b_mod, odd_norm1, odd_norm2, odd_w_in, odd_q_norm, odd_k_norm, odd_lambda, odd_subln, odd_w_out, odd_router, odd_exp_gate, odd_exp_up, odd_exp_down):
    B, L, D = x.shape
    Lc = ctx.shape[1]
    T = B * L
    row = lambda v: v.reshape(1, -1)
    pad_lanes = lambda v: jnp.pad(v.reshape(1, -1), ((0, 0), (0, LANES - v.size)))

    cond = jnp.concatenate([c, c_ctx[None, :], jnp.zeros((SUBLANES - B - 1, D), F32)], axis=0)

    mod = _adaln(cond, even_w_mod[0], even_b_mod[0]).reshape(SUBLANES, 6, D)
    mod_l, mod_c = mod[:B], mod[B:B + 1]
    w_in = even_w_in[0]
    cut = SSD_INNER + SSD_CONV_CH
    w_in = jnp.concatenate([w_in[:, :cut], w_in[:, cut + 2 * SSD_HEADS:], w_in[:, cut:cut + 2 * SSD_HEADS],
                            jnp.zeros((D, LANES - 2 * SSD_HEADS), F32)], axis=1).astype(BF16)
    g1 = row(even_norm1[0])
    cos_c, sin_c = _ret_rope_tables(jnp.arange(Lc))
    cos_l, sin_l = _ret_rope_tables(Lc + jnp.arange(L))
    dtb = pad_lanes(even_dt_bias[0])
    alog = pad_lanes(even_a_log[0])
    conv_b = row(even_conv_b[0])
    ssd_kw = dict(G=SSD_GROUPS, R=SSD_HEADS // SSD_GROUPS, P=SSD_HEAD_DIM, N=SSD_STATE)
    ret_kw = dict(G=RET_HEADS, R=1, P=RET_V_DIM, N=RET_QK_DIM)

    def mix(xin, modv, cosf, sinf, hs, hr, tm):
        z, xbc, dt, rq, rk, rv, rg = _even_in(xin, modv, g1, w_in, cosf, sinf, tm)
        xs, bmat, cmat, acol, arow, wrow = _ssd_prep(xbc, dt, even_conv_w[0], conv_b, dtb, alog, tm)
        racol, rarow, rwrow = _ret_decay_tables(even_ret_decay[0], xin.shape[1])
        (ysf, ysb, hs_f, hs_b), (yrf, yrb, hr_f, hr_b) = _bidir_scans([
            (cmat, bmat, xs, acol, arow, wrow, hs[0], hs[1], ssd_kw),
            (rq, rk, rv, racol, rarow, rwrow, hr[0], hr[1], ret_kw)])
        return (ysf, ysb, xs, z, yrf, yrb, rg), (hs_f, hs_b), (hr_f, hr_b)

    zs = jnp.zeros((B, SSD_GROUPS, SSD_STATE, SSD_INNER // SSD_GROUPS), F32)
    zr = jnp.zeros((B, RET_HEADS, RET_QK_DIM, RET_V_DIM), F32)
    tm_c = 256
    tm_l = 512
    mixed_c, hs, hr = mix(ctx, mod_c, cos_c, sin_c, (zs, zs), (zr, zr), tm_c)
    mixed_l, _, _ = mix(x, mod_l, cos_l, sin_l, hs, hr, tm_l)

    dvec = row(jnp.repeat(even_d[0], SSD_HEAD_DIM))
    snorm = row(even_ssd_norm[0])
    w_out = even_w_out[0].astype(BF16)
    g2 = row(even_norm2[0])
    wg = even_ffn_gate[0].astype(BF16)
    wu = even_ffn_up[0].astype(BF16)
    wd = even_ffn_down[0].astype(BF16)
    ffn_tf = wg.shape[1]

    xl = _even_out(*mixed_l, x, mod_l, dvec, snorm, w_out, tm_l)
    xl = _ffn(xl, mod_l, g2, wg, wu, wd, tm_l, ffn_tf)
    xc = _even_out(*mixed_c, ctx, mod_c, dvec, snorm, w_out, tm_c)
    xc = _ffn(xc, mod_c, g2, wg, wu, wd, tm_c, ffn_tf)

    mod = _adaln(cond, odd_w_mod[0], odd_b_mod[0]).reshape(SUBLANES, 6, D)
    mod_l, mod_c = mod[:B], mod[B:B + 1]
    lam_init = 0.8 - 0.6 * math.exp(-0.3 * 1)
    lq1, lk1, lq2, lk2 = odd_lambda[0]
    lam = jnp.exp(jnp.sum(lq1 * lk1)) - jnp.exp(jnp.sum(lq2 * lk2)) + lam_init
    bound = 1.02 * DIFF_HEAD_DIM ** 0.5 * jnp.max(jnp.abs(odd_q_norm[0])) * jnp.max(jnp.abs(odd_k_norm[0]))
    lam = jnp.stack([lam, bound - SHIFT_HEADROOM, (bound <= SHIFT_MAX_BOUND).astype(F32),
                     jnp.zeros((), F32)]).reshape(1, 4)
    w_in = odd_w_in[0].astype(BF16)
    g1 = row(odd_norm1[0])
    qn = row(jnp.tile(odd_q_norm[0], LANES // DIFF_HEAD_DIM))
    kn = row(jnp.tile(odd_k_norm[0], LANES // DIFF_HEAD_DIM))
    gid = jnp.arange(MXU_N) // DIFF_HEAD_DIM
    bd = (gid[:, None] == gid[None, :]).astype(BF16)
    cos_l, sin_l = _axial_rope_tables(L)
    cos_c, sin_c = jnp.ones((Lc, LANES), F32), jnp.zeros((Lc, LANES), F32)
    q_l, k_l, v_l = _odd_in(xl, mod_l, g1, w_in, qn, kn, bd, cos_l, sin_l, tm_l)
    _, k_c, v_c = _odd_in(xc, mod_c, g1, w_in, qn, kn, bd, cos_c, sin_c, tm_c)
    o = _attention(lam, q_l, k_l, v_l, k_c, v_c, row(odd_subln[0]), 1.0 - lam_init, 2048, 512, 4)

    w_router = jnp.pad(jnp.concatenate(_split3(odd_router[0]), axis=1), ((0, 0), (0, LANES - 3 * N_EXPERTS)))
    xl, h, route, route_t = _odd_out(o, xl, mod_l, row(odd_norm2[0]), odd_w_out[0].astype(BF16), w_router, tm_l)
    moe_tm = min(1024, T)
    slots, tile_expert, tile_valid, tile_src, last_tile, n_tiles = _routing_tables(route_t, moe_tm)
    hs_sorted = _dispatch(last_tile, slots, h, n_tiles * moe_tm, 512, moe_tm)
    ys_sorted = _experts(tile_expert, tile_valid, tile_src, hs_sorted, odd_exp_gate[0], odd_exp_up[0],
                         odd_exp_down[0], moe_tm, 512)
    return _combine(slots, ys_sorted, xl, mod_l, route, 256)
```

```python
import functools
import math

import jax
import jax.numpy as jnp
from jax import lax
from jax.experimental import pallas as pl
from jax.experimental.pallas import tpu as pltpu

F32 = jnp.float32
BF16 = jnp.bfloat16
LANES = 128
SUBLANES = 8
MXU_N = 256
VMEM_LIMIT = 56 * 1024 * 1024

GRID_W = 64
CHUNK = 128
NORM_EPS = 1e-6
ROPE_THETA = 10000.0
SSD_HEADS = 16
SSD_HEAD_DIM = 64
SSD_INNER = SSD_HEADS * SSD_HEAD_DIM
SSD_GROUPS = 2
SSD_STATE = 128
SSD_BC = SSD_GROUPS * SSD_STATE
SSD_CONV_CH = SSD_INNER + 2 * SSD_BC
RET_HEADS = 4
RET_QK_DIM = 128
RET_V_DIM = 256
RET_QK = RET_HEADS * RET_QK_DIM
RET_INNER = RET_HEADS * RET_V_DIM
DIFF_HEADS = 8
DIFF_HEAD_DIM = 64
DIFF_V_DIM = 2 * DIFF_HEAD_DIM
DIFF_INNER = DIFF_HEADS * DIFF_V_DIM
N_EXPERTS = 8
NEG_BIG = -1e30
SHIFT_HEADROOM = 30.0
SHIFT_MAX_BOUND = 55.0
DMA_ISSUE_UNROLL = 8


def _params(sem):
    return pltpu.CompilerParams(dimension_semantics=sem, vmem_limit_bytes=VMEM_LIMIT)


def _tile(n, pref):
    t = min(n, pref)
    assert n % t == 0, (n, t)
    return t


def _silu(x):
    return x * jax.nn.sigmoid(x)


def _rms(x):
    return x * lax.rsqrt(jnp.mean(x * x, axis=-1, keepdims=True) + NORM_EPS)


def _modulate(x, gain, shift, scale):
    return _rms(x) * gain * (1.0 + scale) + shift


def _dot(a, b):
    return jnp.dot(a, b, preferred_element_type=F32)


def _dot_nt(a, b):
    return lax.dot_general(a, b, (((1,), (1,)), ((), ())), preferred_element_type=F32)


def _split3(a):
    hi = a.astype(BF16)
    r1 = a - hi.astype(F32)
    mid = r1.astype(BF16)
    lo = (r1 - mid.astype(F32)).astype(BF16)
    return hi, mid, lo


def _adaln_kernel(c_ref, w_ref, b_ref, o_ref):
    a = _silu(c_ref[...]).astype(BF16)
    o_ref[...] = _dot(a, w_ref[...].astype(BF16)) + b_ref[...]


def _adaln(cond, w, b):
    R, D = cond.shape
    N = w.shape[1]
    tn = _tile(N, 1024)
    return pl.pallas_call(
        _adaln_kernel,
        grid=(N // tn,),
        in_specs=[pl.BlockSpec((R, D), lambda j: (0, 0)),
                  pl.BlockSpec((D, tn), lambda j: (0, j)),
                  pl.BlockSpec((1, tn), lambda j: (0, j))],
        out_specs=pl.BlockSpec((R, tn), lambda j: (0, j)),
        out_shape=jax.ShapeDtypeStruct((R, N), F32),
        compiler_params=_params(("arbitrary",)),
        name="adaln",
    )(cond, w, b.reshape(1, N))


EV_Z = 0
EV_XBC = EV_Z + SSD_INNER
EV_RQ = EV_XBC + SSD_CONV_CH
EV_RK = EV_RQ + RET_QK
EV_RV = EV_RK + RET_QK
EV_RG = EV_RV + RET_INNER
EV_DT = EV_RG + RET_INNER
EV_END = EV_DT + LANES


def _even_in_kernel(x_ref, mod_ref, gain_ref, w_ref, cos_ref, sin_ref,
                    z_ref, xbc_ref, dt_ref, rq_ref, rk_ref, rv_ref, rg_ref):
    mod = mod_ref[0]
    h = _modulate(x_ref[0], gain_ref[...], mod[0:1], mod[1:2]).astype(BF16)

    def mm(lo, hi):
        return _dot(h, w_ref[:, lo:hi])

    z_ref[0] = mm(EV_Z, EV_XBC).astype(BF16)
    xbc_ref[0] = mm(EV_XBC, EV_RQ)
    cos = cos_ref[...]
    sin = sin_ref[...]
    k_scale = RET_QK_DIM ** -0.5
    qk = mm(EV_RQ, EV_RV)
    for j in range(RET_HEADS):
        lo = j * RET_QK_DIM
        a = qk[:, lo:lo + RET_QK_DIM]
        rq_ref[0, :, lo:lo + RET_QK_DIM] = (a * cos + pltpu.roll(a, RET_QK_DIM // 2, 1) * sin).astype(BF16)
        a = qk[:, RET_QK + lo:RET_QK + lo + RET_QK_DIM]
        a = (a * cos + pltpu.roll(a, RET_QK_DIM // 2, 1) * sin) * k_scale
        rk_ref[0, lo:lo + RET_QK_DIM, :] = a.T.astype(BF16)
    rv_ref[0] = mm(EV_RV, EV_RG).astype(BF16)
    rg_ref[0] = mm(EV_RG, EV_DT).astype(BF16)
    dt_ref[0] = mm(EV_DT, EV_END)


def _even_in(x, mod, gain, w, cosf, sinf, tm):
    B, S, D = x.shape
    tm = _tile(S, tm)
    bm = mod.shape[0]
    mod_map = (lambda b, i: (b, 0, 0)) if bm > 1 else (lambda b, i: (0, 0, 0))
    row = lambda n: pl.BlockSpec((1, tm, n), lambda b, i: (b, i, 0))
    outs = [(SSD_INNER, BF16), (SSD_CONV_CH, F32), (LANES, F32), (RET_QK, BF16), (RET_QK, BF16),
            (RET_INNER, BF16), (RET_INNER, BF16)]
    RK_OUT = 4
    return pl.pallas_call(
        _even_in_kernel,
        grid=(B, S // tm),
        in_specs=[row(D),
                  pl.BlockSpec((1, 6, D), mod_map),
                  pl.BlockSpec((1, D), lambda b, i: (0, 0)),
                  pl.BlockSpec((D, EV_END), lambda b, i: (0, 0)),
                  pl.BlockSpec((tm, LANES), lambda b, i: (i, 0)),
                  pl.BlockSpec((tm, LANES), lambda b, i: (i, 0))],
        out_specs=[pl.BlockSpec((1, n, tm), lambda b, i: (b, 0, i)) if j == RK_OUT else row(n)
                   for j, (n, _) in enumerate(outs)],
        out_shape=[jax.ShapeDtypeStruct((B, n, S) if j == RK_OUT else (B, S, n), dt)
                   for j, (n, dt) in enumerate(outs)],
        compiler_params=_params(("arbitrary", "arbitrary")),
        name="even_in",
    )(x, mod, gain, w, cosf, sinf)


def _ssd_prep_kernel(xbc_ref, prev_ref, next_ref, dt_ref, cw_ref, cb_ref, dtb_ref, alog_ref,
                     xs_ref, bm_ref, cm_ref, acol_ref, arow_ref, wrow_ref, *, tm, chunk):
    i = pl.program_id(1)
    n = pl.num_programs(1)
    brow = lax.broadcasted_iota(jnp.int32, (chunk, 1), 0)
    for cb in range(SSD_CONV_CH // LANES):
        cols = slice(cb * LANES, (cb + 1) * LANES)
        w0, w1, w2 = cw_ref[0:1, cols], cw_ref[1:2, cols], cw_ref[2:3, cols]
        bias = cb_ref[:, cols]
        zero_row = jnp.zeros((1, LANES), F32)
        for rb in range(tm // chunk):
            r0 = rb * chunk
            x = xbc_ref[0, r0:r0 + chunk, cols]
            if rb == 0:
                above = jnp.where(i > 0, prev_ref[0, SUBLANES - 1:SUBLANES, cols], zero_row)
            else:
                above = xbc_ref[0, r0 - 1:r0, cols]
            if rb == tm // chunk - 1:
                below = jnp.where(i < n - 1, next_ref[0, 0:1, cols], zero_row)
            else:
                below = xbc_ref[0, r0 + chunk:r0 + chunk + 1, cols]
            xp = jnp.where(brow == 0, above, pltpu.roll(x, 1, 0))
            xn = jnp.where(brow == chunk - 1, below, pltpu.roll(x, chunk - 1, 0))
            y = _silu(xp * w0 + x * w1 + xn * w2 + bias)
            if cb * LANES < SSD_INNER:
                xs_ref[0, r0:r0 + chunk, cols] = y.astype(BF16)
            elif cb * LANES < SSD_INNER + SSD_BC:
                lo = cb * LANES - SSD_INNER
                bm_ref[0, lo:lo + LANES, r0:r0 + chunk] = y.T.astype(BF16)
            else:
                lo = cb * LANES - SSD_INNER - SSD_BC
                cm_ref[0, r0:r0 + chunk, lo:lo + LANES] = y.astype(BF16)

    rows = lax.broadcasted_iota(jnp.int32, (tm, 1), 0)
    t = dt_ref[0] + dtb_ref[...]
    dt = jnp.maximum(t, 0.0) + jnp.log(1.0 + jnp.exp(-jnp.abs(t)))
    la = -dt * jnp.exp(alog_ref[...])
    rmod = rows % chunk
    fwd = la
    rev = la
    sh = 1
    while sh < chunk:
        fwd = fwd + jnp.where(rmod >= sh, pltpu.roll(fwd, sh, 0), 0.0)
        rev = rev + jnp.where(rmod < chunk - sh, pltpu.roll(rev, tm - sh, 0), 0.0)
        sh *= 2
    lane = lax.broadcasted_iota(jnp.int32, (1, LANES), 1)
    acc = jnp.where(lane < SSD_HEADS, fwd, rev)
    acc_t = acc.T
    dt_t = dt.T
    hp = SSD_HEADS // SSD_GROUPS
    for j in range(2 * SSD_GROUPS):
        acol_ref[0, j] = acc[:, j * hp:(j + 1) * hp]
        arow_ref[0, j] = acc_t[j * hp:(j + 1) * hp, :]
        wrow_ref[0, j] = dt_t[j * hp:(j + 1) * hp, :]


def _ssd_prep(xbc, dt, conv_w, conv_b, dtb, alog, tm):
    B, S, _ = xbc.shape
    tm = _tile(S, tm)
    assert tm % CHUNK == 0
    nh = S // SUBLANES
    hp = SSD_HEADS // SSD_GROUPS
    gd = 2 * SSD_GROUPS
    row = lambda n: pl.BlockSpec((1, tm, n), lambda b, i: (b, i, 0))
    const = lambda r, n: pl.BlockSpec((r, n), lambda b, i: (0, 0))
    return pl.pallas_call(
        functools.partial(_ssd_prep_kernel, tm=tm, chunk=CHUNK),
        grid=(B, S // tm),
        in_specs=[row(SSD_CONV_CH),
                  pl.BlockSpec((1, SUBLANES, SSD_CONV_CH),
                               lambda b, i: (b, jnp.maximum(i * (tm // SUBLANES) - 1, 0), 0)),
                  pl.BlockSpec((1, SUBLANES, SSD_CONV_CH),
                               lambda b, i: (b, jnp.minimum((i + 1) * (tm // SUBLANES), nh - 1), 0)),
                  row(LANES),
                  const(3, SSD_CONV_CH), const(1, SSD_CONV_CH), const(1, LANES), const(1, LANES)],
        out_specs=[row(SSD_INNER), pl.BlockSpec((1, SSD_BC, tm), lambda b, i: (b, 0, i)), row(SSD_BC),
                   pl.BlockSpec((1, gd, tm, hp), lambda b, i: (b, 0, i, 0)),
                   pl.BlockSpec((1, gd, hp, tm), lambda b, i: (b, 0, 0, i)),
                   pl.BlockSpec((1, gd, hp, tm), lambda b, i: (b, 0, 0, i))],
        out_shape=[jax.ShapeDtypeStruct((B, S, SSD_INNER), BF16),
                   jax.ShapeDtypeStruct((B, SSD_BC, S), BF16),
                   jax.ShapeDtypeStruct((B, S, SSD_BC), BF16),
                   jax.ShapeDtypeStruct((B, gd, S, hp), F32),
                   jax.ShapeDtypeStruct((B, gd, hp, S), F32),
                   jax.ShapeDtypeStruct((B, gd, hp, S), F32)],
        compiler_params=_params(("arbitrary", "arbitrary")),
        name="ssd_prep",
    )(xbc, xbc, xbc, dt, conv_w, conv_b, dtb, alog)


def _scan_direction(q_ref, k_ref, v_ref, ac_ref, ar_ref, wr_ref, y_ref, h_scr, d, *, G, R, P, N):
    reverse = d == 1
    C = q_ref.shape[1]
    li = lax.broadcasted_iota(jnp.int32, (C, C), 0)
    si = lax.broadcasted_iota(jnp.int32, (C, C), 1)
    mask = (si >= li) if reverse else (li >= si)
    far = 0 if reverse else C - 1
    width = max(P, LANES)
    heads_per = width // P
    lane_head = lax.broadcasted_iota(jnp.int32, (1, width), 1) // P
    RP = R * P
    for g in range(G):
        q = q_ref[0, :, g * N:(g + 1) * N]
        k = k_ref[0, g * N:(g + 1) * N, :]
        s = _dot(q, k)
        k_t = k.astype(F32)
        ac = ac_ref[0, g]
        ar = ar_ref[0, g]
        wr = wr_ref[0, g]
        for u in range(RP // width):
            lo = u * width
            v = v_ref[0, :, g * RP + lo:g * RP + lo + width].astype(BF16)
            h_old = h_scr[d, g, :, lo:lo + width]
            y_diag = None
            for j in range(heads_per):
                r = u * heads_per + j
                a_col = ac[:, r:r + 1]
                a_row = ar[r:r + 1, :]
                w_row = wr[r:r + 1, :]
                tot = a_row[:, far:far + 1]
                decay = jnp.exp(jnp.where(mask, a_col - a_row, NEG_BIG))
                m = (s * decay * w_row).astype(BF16)
                kw = (k_t * (jnp.exp(tot - a_row) * w_row)).astype(BF16)
                yd = _dot(m, v)
                st = _dot(kw, v)
                e_col = jnp.exp(a_col)
                e_tot = jnp.exp(tot)
                if y_diag is None:
                    y_diag, state, col_scale, tot_scale = yd, st, e_col, e_tot
                else:
                    sel = lane_head == j
                    y_diag = jnp.where(sel, yd, y_diag)
                    state = jnp.where(sel, st, state)
                    col_scale = jnp.where(sel, e_col, col_scale)
                    tot_scale = jnp.where(sel, e_tot, tot_scale)
            y = y_diag + col_scale * _dot(q, h_old.astype(BF16))
            y_ref[0, :, g * RP + lo:g * RP + lo + width] = y.astype(y_ref.dtype)
            h_scr[d, g, :, lo:lo + width] = tot_scale * h_old + state


SCAN_INS = 14
SCAN_OUTS = 4


def _scan_kernel(*refs, families):
    n = len(families)
    ins = [refs[SCAN_INS * i:SCAN_INS * (i + 1)] for i in range(n)]
    outs = [refs[SCAN_INS * n + SCAN_OUTS * i:SCAN_INS * n + SCAN_OUTS * (i + 1)] for i in range(n)]
    scrs = refs[(SCAN_INS + SCAN_OUTS) * n:]
    c = pl.program_id(1)

    @pl.when(c == 0)
    def _():
        for fam_in, h_scr in zip(ins, scrs):
            h_scr[0] = fam_in[12][0]
            h_scr[1] = fam_in[13][0]

    for dims, fam_in, fam_out, h_scr in zip(families, ins, outs, scrs):
        qf, kf, vf, qb, kb, vb, acf, arf, wrf, acb, arb, wrb = fam_in[:12]
        _scan_direction(qf, kf, vf, acf, arf, wrf, fam_out[0], h_scr, 0, **dims)
        _scan_direction(qb, kb, vb, acb, arb, wrb, fam_out[1], h_scr, 1, **dims)

    @pl.when(c == pl.num_programs(1) - 1)
    def _():
        for fam_out, h_scr in zip(outs, scrs):
            fam_out[2][0] = h_scr[0]
            fam_out[3][0] = h_scr[1]


def _bidir_scans(families):
    B, S, _ = families[0][0].shape
    C = _tile(S, CHUNK)
    nc = S // C
    fwd = lambda c: c
    rev = lambda c: nc - 1 - c
    in_specs, out_specs, out_shape, scratch, args, dims_list = [], [], [], [], [], []
    for q, k, v, acol, arow, wrow, h0_f, h0_b, dims in families:
        G, R, P, N = dims["G"], dims["R"], dims["P"], dims["N"]
        RP = R * P
        bx = (lambda b: b) if acol.shape[0] > 1 else (lambda b: 0)
        seq = lambda n, ci: pl.BlockSpec((1, C, n), lambda b, c, ci=ci: (b, ci(c), 0))
        tc = (lambda ci: ci) if acol.shape[2] > C else (lambda ci: (lambda c: 0))
        col = lambda d, ci, G=G, R=R, bx=bx, tc=tc: pl.BlockSpec(
            (1, G, C, R), lambda b, c: (bx(b), d, tc(ci)(c), 0))
        rowt = lambda d, ci, G=G, R=R, bx=bx, tc=tc: pl.BlockSpec(
            (1, G, R, C), lambda b, c: (bx(b), d, 0, tc(ci)(c)))
        state = pl.BlockSpec((1, G, N, RP), lambda b, c: (b, 0, 0, 0))
        seq_t = lambda n, ci: pl.BlockSpec((1, n, C), lambda b, c, ci=ci: (b, 0, ci(c)))
        in_specs += [seq(G * N, fwd), seq_t(G * N, fwd), seq(G * RP, fwd),
                     seq(G * N, rev), seq_t(G * N, rev), seq(G * RP, rev),
                     col(0, fwd), rowt(0, fwd), rowt(0, fwd), col(1, rev), rowt(1, rev), rowt(1, rev),
                     state, state]
        out_specs += [seq(G * RP, fwd), seq(G * RP, rev), state, state]
        out_shape += [jax.ShapeDtypeStruct((B, S, G * RP), BF16), jax.ShapeDtypeStruct((B, S, G * RP), BF16),
                      jax.ShapeDtypeStruct((B, G, N, RP), F32), jax.ShapeDtypeStruct((B, G, N, RP), F32)]
        scratch.append(pltpu.VMEM((2, G, N, RP), F32))
        args += [q, k, v, q, k, v, acol, arow, wrow, acol, arow, wrow, h0_f, h0_b]
        dims_list.append(dims)
    res = pl.pallas_call(
        functools.partial(_scan_kernel, families=tuple(dims_list)),
        grid=(B, nc),
        in_specs=in_specs,
        out_specs=out_specs,
        out_shape=out_shape,
        scratch_shapes=scratch,
        compiler_params=_params(("arbitrary", "arbitrary")),
        name="bidir_scans",
    )(*args)
    return [res[SCAN_OUTS * i:SCAN_OUTS * (i + 1)] for i in range(len(families))]


def _even_out_kernel(ysf_ref, ysb_ref, xs_ref, z_ref, yrf_ref, yrb_ref, rg_ref, x_ref, mod_ref, dvec_ref, sn_ref,
                     w_ref, o_ref):
    f32 = lambda ref: ref[0].astype(F32)
    y = f32(ysf_ref) + f32(ysb_ref) + f32(xs_ref) * dvec_ref[...]
    y = _rms(y * _silu(f32(z_ref))) * sn_ref[...]
    out = _dot(y.astype(BF16), w_ref[0:SSD_INNER, :])
    yr = f32(yrf_ref) + f32(yrb_ref)
    rg = _silu(f32(rg_ref))
    for j in range(RET_HEADS):
        lo = j * RET_V_DIM
        r = _rms(yr[:, lo:lo + RET_V_DIM]) * rg[:, lo:lo + RET_V_DIM]
        out = out + _dot(r.astype(BF16), w_ref[SSD_INNER + lo:SSD_INNER + lo + RET_V_DIM, :])
    o_ref[0] = x_ref[0] + mod_ref[0][2:3] * out


def _even_out(ysf, ysb, xs, z, yrf, yrb, rg, x, mod, dvec, ssd_norm, w, tm):
    B, S, D = x.shape
    tm = _tile(S, tm)
    bm = mod.shape[0]
    mod_map = (lambda b, i: (b, 0, 0)) if bm > 1 else (lambda b, i: (0, 0, 0))
    row = lambda n: pl.BlockSpec((1, tm, n), lambda b, i: (b, i, 0))
    const = lambda r, n: pl.BlockSpec((r, n), lambda b, i: (0, 0))
    return pl.pallas_call(
        _even_out_kernel,
        grid=(B, S // tm),
        in_specs=[row(SSD_INNER), row(SSD_INNER), row(SSD_INNER), row(SSD_INNER),
                  row(RET_INNER), row(RET_INNER), row(RET_INNER), row(D),
                  pl.BlockSpec((1, 6, D), mod_map),
                  const(1, SSD_INNER), const(1, SSD_INNER), const(SSD_INNER + RET_INNER, D)],
        out_specs=row(D),
        out_shape=jax.ShapeDtypeStruct((B, S, D), F32),
        compiler_params=_params(("arbitrary", "arbitrary")),
        name="even_out",
    )(ysf, ysb, xs, z, yrf, yrb, rg, x, mod, dvec, ssd_norm, w)


def _ffn_kernel(x_ref, mod_ref, gain_ref, wg_ref, wu_ref, wd_ref, o_ref, h_scr, acc_scr):
    f = pl.program_id(2)
    mod = mod_ref[0]

    @pl.when(f == 0)
    def _():
        h_scr[...] = _modulate(x_ref[0], gain_ref[...], mod[3:4], mod[4:5]).astype(BF16)
        acc_scr[...] = jnp.zeros_like(acc_scr)

    h = h_scr[...]
    a = (_silu(_dot(h, wg_ref[...])) * _dot(h, wu_ref[...])).astype(BF16)
    acc_scr[...] += _dot(a, wd_ref[...])

    @pl.when(f == pl.num_programs(2) - 1)
    def _():
        o_ref[0] = x_ref[0] + mod[5:6] * acc_scr[...]


def _ffn(x, mod, gain, wg, wu, wd, tm, tf):
    B, S, D = x.shape
    F = wg.shape[1]
    tm = _tile(S, tm)
    tf = _tile(F, tf)
    bm = mod.shape[0]
    mod_map = (lambda b, i, f: (b, 0, 0)) if bm > 1 else (lambda b, i, f: (0, 0, 0))
    resident = dict(pipeline_mode=pl.Buffered(1)) if tf == F else {}
    return pl.pallas_call(
        _ffn_kernel,
        grid=(B, S // tm, F // tf),
        in_specs=[pl.BlockSpec((1, tm, D), lambda b, i, f: (b, i, 0)),
                  pl.BlockSpec((1, 6, D), mod_map),
                  pl.BlockSpec((1, D), lambda b, i, f: (0, 0)),
                  pl.BlockSpec((D, tf), lambda b, i, f: (0, f), **resident),
                  pl.BlockSpec((D, tf), lambda b, i, f: (0, f), **resident),
                  pl.BlockSpec((tf, D), lambda b, i, f: (f, 0), **resident)],
        out_specs=pl.BlockSpec((1, tm, D), lambda b, i, f: (b, i, 0)),
        out_shape=jax.ShapeDtypeStruct((B, S, D), F32),
        scratch_shapes=[pltpu.VMEM((tm, D), BF16), pltpu.VMEM((tm, D), F32)],
        compiler_params=_params(("arbitrary", "arbitrary", "arbitrary")),
        name="ffn_dense",
    )(x, mod, gain, wg, wu, wd)


def _odd_in_kernel(x_ref, mod_ref, gain_ref, w_ref, qn_ref, kn_ref, bd_ref, cos_ref, sin_ref,
                   q_ref, k_ref, v_ref):
    mod = mod_ref[0]
    h = _modulate(x_ref[0], gain_ref[...], mod[0:1], mod[1:2]).astype(BF16)
    cos = cos_ref[...]
    sin = sin_ref[...]
    bd = bd_ref[...]
    lane = lax.broadcasted_iota(jnp.int32, (1, LANES), 1)
    quarter = DIFF_HEAD_DIM // 4
    first = (lane % (2 * quarter)) < quarter
    for base, n_ref, o_ref, scale in ((0, qn_ref, q_ref, DIFF_HEAD_DIM ** -0.5),
                                      (DIFF_INNER, kn_ref, k_ref, 1.0)):
        for j in range(DIFF_INNER // MXU_N):
            lo = j * MXU_N
            a = _dot(h, w_ref[:, base + lo:base + lo + MXU_N])
            sq = a * a
            hi = sq.astype(BF16)
            rest = (sq - hi.astype(F32)).astype(BF16)
            ms = (_dot(hi, bd) + _dot(rest, bd)) * (1.0 / DIFF_HEAD_DIM)
            a = a * lax.rsqrt(ms + NORM_EPS)
            for c in range(MXU_N // LANES):
                b = a[:, c * LANES:(c + 1) * LANES] * n_ref[...]
                partner = jnp.where(first, pltpu.roll(b, LANES - quarter, 1), pltpu.roll(b, quarter, 1))
                o_ref[0, :, lo + c * LANES:lo + (c + 1) * LANES] = ((b * cos + partner * sin) * scale).astype(BF16)
    v_ref[0] = _dot(h, w_ref[:, 2 * DIFF_INNER:3 * DIFF_INNER]).astype(BF16)


def _odd_in(x, mod, gain, w, qn, kn, bd, cosf, sinf, tm):
    B, S, D = x.shape
    tm = _tile(S, tm)
    bm = mod.shape[0]
    mod_map = (lambda b, i: (b, 0, 0)) if bm > 1 else (lambda b, i: (0, 0, 0))
    row = lambda n: pl.BlockSpec((1, tm, n), lambda b, i: (b, i, 0))
    const = lambda r, n: pl.BlockSpec((r, n), lambda b, i: (0, 0))
    return pl.pallas_call(
        _odd_in_kernel,
        grid=(B, S // tm),
        in_specs=[row(D), pl.BlockSpec((1, 6, D), mod_map), const(1, D), const(D, 3 * DIFF_INNER),
                  const(1, LANES), const(1, LANES), const(MXU_N, MXU_N),
                  pl.BlockSpec((tm, LANES), lambda b, i: (i, 0)),
                  pl.BlockSpec((tm, LANES), lambda b, i: (i, 0))],
        out_specs=[row(DIFF_INNER)] * 3,
        out_shape=[jax.ShapeDtypeStruct((B, S, DIFF_INNER), BF16)] * 3,
        compiler_params=_params(("arbitrary", "arbitrary")),
        name="odd_in",
    )(x, mod, gain, w, qn, kn, bd, cosf, sinf)


def _attn_kernel(sc_ref, q_ref, kl_ref, vl_ref, kc_ref, vc_ref, sub_ref, o_ref,
                 q2_scr, m_scr, l_scr, acc_scr, *, tq, tk, unroll, out_scale):
    q = q_ref[0]
    lane = lax.broadcasted_iota(jnp.int32, (1, LANES), 1)
    lo_half = lane < DIFF_HEAD_DIM
    zero = jnp.zeros_like(q)
    q2_scr[0:tq, :] = jnp.where(lo_half, q, zero)
    q2_scr[tq:2 * tq, :] = jnp.where(lo_half, zero, q)
    l_scr[...] = jnp.zeros_like(l_scr)
    acc_scr[...] = jnp.zeros_like(acc_scr)
    n_lat = kl_ref.shape[1] // tk
    use_shift = sc_ref[0, 2] > 0.0

    def over_keys(update, unroll):
        def body(j, carry):
            start = pl.multiple_of(j * tk, tk)
            update(kl_ref[0, pl.ds(start, tk), :], vl_ref[0, pl.ds(start, tk), :])
            return carry

        lax.fori_loop(0, n_lat, body, 0, unroll=unroll)
        update(kc_ref[0], vc_ref[0])

    @pl.when(use_shift)
    def _():
        shift = sc_ref[0, 1]

        def update(k, v):
            p = jnp.exp(_dot_nt(q2_scr[...], k) - shift)
            part = p[:, 0:LANES]
            for c in range(1, p.shape[1] // LANES):
                part = part + p[:, c * LANES:(c + 1) * LANES]
            l_scr[...] += part
            acc_scr[...] += _dot(p.astype(BF16), v)

        over_keys(update, unroll)
        acc_scr[...] = acc_scr[...] / jnp.sum(l_scr[...], axis=1, keepdims=True)

    @pl.when(jnp.logical_not(use_shift))
    def _():
        m_scr[...] = jnp.full_like(m_scr, -jnp.inf)

        def update(k, v):
            s = _dot_nt(q2_scr[...], k)
            m_prev = m_scr[...]
            m_new = jnp.maximum(m_prev, jnp.max(s, axis=1, keepdims=True))
            alpha = jnp.exp(m_prev - m_new)
            p = jnp.exp(s - m_new[:, 0:1])
            l_scr[...] = alpha * l_scr[...] + jnp.sum(p, axis=1, keepdims=True)
            acc_scr[...] = alpha * acc_scr[...] + _dot(p.astype(BF16), v)
            m_scr[...] = m_new

        over_keys(update, 1)
        acc_scr[...] = acc_scr[...] / l_scr[...]

    o = acc_scr[0:tq, :] - sc_ref[0, 0] * acc_scr[tq:2 * tq, :]
    o_ref[0] = (_rms(o) * sub_ref[...] * out_scale).astype(BF16)


def _attention(lam, q, k_lat, v_lat, k_ctx, v_ctx, subln, out_scale, tq, tk, unroll):
    B, L, _ = q.shape
    Lc = k_ctx.shape[1]
    tq = _tile(L, tq)
    tk = _tile(L, tk)
    unroll = min(unroll, L // tk)
    head = lambda n: pl.BlockSpec((1, n, LANES), lambda b, h, i: (b, 0, h))
    return pl.pallas_call(
        functools.partial(_attn_kernel, tq=tq, tk=tk, unroll=unroll, out_scale=out_scale),
        grid=(B, DIFF_HEADS, L // tq),
        in_specs=[pl.BlockSpec(memory_space=pltpu.SMEM),
                  pl.BlockSpec((1, tq, LANES), lambda b, h, i: (b, i, h)),
                  head(L), head(L), head(Lc), head(Lc),
                  pl.BlockSpec((1, LANES), lambda b, h, i: (0, 0))],
        out_specs=pl.BlockSpec((1, tq, LANES), lambda b, h, i: (b, i, h)),
        out_shape=jax.ShapeDtypeStruct((B, L, DIFF_INNER), BF16),
        scratch_shapes=[pltpu.VMEM((2 * tq, LANES), BF16), pltpu.VMEM((2 * tq, LANES), F32),
                        pltpu.VMEM((2 * tq, LANES), F32), pltpu.VMEM((2 * tq, LANES), F32)],
        compiler_params=_params(("arbitrary", "arbitrary", "arbitrary")),
        name="diff_attn",
    )(lam, q, k_lat, v_lat, k_ctx, v_ctx, subln)


def _odd_out_kernel(o_ref, x_ref, mod_ref, gain_ref, w_ref, wr_ref, xo_ref, h_ref, route_ref, rt_ref):
    mod = mod_ref[0]
    x1 = x_ref[0] + mod[2:3] * _dot(o_ref[0], w_ref[...])
    xo_ref[0] = x1
    h = _modulate(x1, gain_ref[...], mod[3:4], mod[4:5])
    h_ref[...] = h
    wr = wr_ref[...]
    h_hi, h_mid, h_lo = _split3(h)
    parts = _dot(h_lo, wr) + _dot(h_mid, wr) + _dot(h_hi, wr)
    logits = (pltpu.roll(parts, LANES - 2 * N_EXPERTS, 1) + pltpu.roll(parts, LANES - N_EXPERTS, 1)) + parts
    lane = lax.broadcasted_iota(jnp.int32, logits.shape, 1)
    lg = jnp.where(lane < N_EXPERTS, logits, -jnp.inf)
    v1 = jnp.max(lg, axis=1, keepdims=True)
    i1 = jnp.min(jnp.where(lg == v1, lane, LANES), axis=1, keepdims=True)
    lg2 = jnp.where(lane == i1, -jnp.inf, lg)
    v2 = jnp.max(lg2, axis=1, keepdims=True)
    i2 = jnp.min(jnp.where(lg2 == v2, lane, LANES), axis=1, keepdims=True)
    e = jnp.exp(v2 - v1)
    w1 = 1.0 / (1.0 + e)
    w2 = e * w1
    route = jnp.where(lane == 0, i1.astype(F32),
                      jnp.where(lane == 1, i2.astype(F32),
                                jnp.where(lane == 2, w1, jnp.where(lane == 3, w2, 0.0))))
    route_ref[...] = route
    rt_ref[...] = route.T[0:SUBLANES, :]


def _odd_out(o, x, mod, gain, w, w_router, tm):
    B, S, D = x.shape
    tm = _tile(S, tm)
    n = S // tm
    row = lambda n: pl.BlockSpec((1, tm, n), lambda b, i: (b, i, 0))
    flat = lambda w: pl.BlockSpec((tm, w), lambda b, i: (b * n + i, 0))
    const = lambda r, n: pl.BlockSpec((r, n), lambda b, i: (0, 0))
    return pl.pallas_call(
        _odd_out_kernel,
        grid=(B, n),
        in_specs=[row(DIFF_INNER), row(D), pl.BlockSpec((1, 6, D), lambda b, i: (b, 0, 0)),
                  const(1, D), const(DIFF_INNER, D), const(D, LANES)],
        out_specs=[row(D), flat(D), flat(LANES), pl.BlockSpec((SUBLANES, tm), lambda b, i: (0, b * n + i))],
        out_shape=[jax.ShapeDtypeStruct((B, S, D), F32), jax.ShapeDtypeStruct((B * S, D), F32),
                   jax.ShapeDtypeStruct((B * S, LANES), F32), jax.ShapeDtypeStruct((SUBLANES, B * S), F32)],
        compiler_params=_params(("arbitrary", "arbitrary")),
        name="odd_out",
    )(o, x, mod, gain, w, w_router)


def _row_copy(src, src_row, dst, dst_row, sem):
    return pltpu.make_async_copy(src.at[pl.ds(src_row, 1), :], dst.at[pl.ds(dst_row, 1), :], sem)


def _dispatch_kernel(last_tile_ref, slot_ref, h_ref, hs_ref, zero_buf, sem, zero_sem, *, tt, tm):
    @pl.when(pl.program_id(0) == 0)
    def _():
        zero_buf[...] = jnp.zeros_like(zero_buf)
        for e in range(2 * N_EXPERTS):
            start = pl.multiple_of(last_tile_ref[e] * tm, tm)
            clear = pltpu.make_async_copy(zero_buf, hs_ref.at[pl.ds(start, tm), :], zero_sem)
            clear.start()
            clear.wait()

    def issue(r, carry):
        _row_copy(h_ref, r, hs_ref, slot_ref[0, 0, r], sem).start()
        _row_copy(h_ref, r, hs_ref, slot_ref[0, 0, tt + r], sem).start()
        return carry

    lax.fori_loop(0, tt, issue, 0, unroll=DMA_ISSUE_UNROLL)
    for _ in range(2):
        pltpu.make_async_copy(h_ref, hs_ref.at[pl.ds(0, tt), :], sem).wait()


def _dispatch(last_tile, slots, h, n_rows, tt, tm):
    T, D = h.shape
    tt = _tile(T, tt)
    grid_spec = pltpu.PrefetchScalarGridSpec(
        num_scalar_prefetch=1,
        grid=(T // tt,),
        in_specs=[pl.BlockSpec((1, 1, 2 * tt), lambda i, lt: (i, 0, 0), memory_space=pltpu.SMEM),
                  pl.BlockSpec((tt, D), lambda i, lt: (i, 0))],
        out_specs=pl.BlockSpec(memory_space=pl.ANY),
        scratch_shapes=[pltpu.VMEM((tm, D), F32), pltpu.SemaphoreType.DMA(()), pltpu.SemaphoreType.DMA(())])
    return pl.pallas_call(
        functools.partial(_dispatch_kernel, tt=tt, tm=tm),
        grid_spec=grid_spec,
        out_shape=jax.ShapeDtypeStruct((n_rows, D), F32),
        compiler_params=_params(("arbitrary",)),
        name="moe_dispatch",
    )(last_tile, _slot_blocks(slots, tt), h)


def _expert_kernel(te_ref, tv_ref, ts_ref, hs_ref, wg_ref, wu_ref, wd_ref, ys_ref, h_scr, acc_scr):
    del te_ref, ts_ref
    i = pl.program_id(0)
    f = pl.program_id(1)

    @pl.when(tv_ref[i] > 0)
    def _():
        @pl.when(f == 0)
        def _():
            h_scr[...] = hs_ref[...].astype(BF16)
            acc_scr[...] = jnp.zeros_like(acc_scr)

        h = h_scr[...]
        g = _dot(h, wg_ref[0].astype(BF16))
        u = _dot(h, wu_ref[0].astype(BF16))
        acc_scr[...] += _dot((_silu(g) * u).astype(BF16), wd_ref[0].astype(BF16))

        @pl.when(f == pl.num_programs(1) - 1)
        def _():
            ys_ref[...] = acc_scr[...]

    @pl.when((tv_ref[i] == 0) & (f == pl.num_programs(1) - 1))
    def _():
        ys_ref[...] = jnp.zeros_like(ys_ref)


def _experts(tile_expert, tile_valid, tile_src, hs, wg, wu, wd, tm, tf):
    n_rows, D = hs.shape
    F = wg.shape[2]
    tf = _tile(F, tf)
    nf = F // tf
    fi = lambda i, f, tv: jnp.where(tv[i] > 0, f, nf - 1)
    grid_spec = pltpu.PrefetchScalarGridSpec(
        num_scalar_prefetch=3,
        grid=(n_rows // tm, nf),
        in_specs=[pl.BlockSpec((tm, D), lambda i, f, te, tv, ts: (ts[i], 0)),
                  pl.BlockSpec((1, D, tf), lambda i, f, te, tv, ts: (te[i], 0, fi(i, f, tv))),
                  pl.BlockSpec((1, D, tf), lambda i, f, te, tv, ts: (te[i], 0, fi(i, f, tv))),
                  pl.BlockSpec((1, tf, D), lambda i, f, te, tv, ts: (te[i], fi(i, f, tv), 0))],
        out_specs=pl.BlockSpec((tm, D), lambda i, f, te, tv, ts: (i, 0)),
        scratch_shapes=[pltpu.VMEM((tm, D), BF16), pltpu.VMEM((tm, D), F32)])
    return pl.pallas_call(
        _expert_kernel,
        grid_spec=grid_spec,
        out_shape=jax.ShapeDtypeStruct((n_rows, D), F32),
        compiler_params=_params(("arbitrary", "arbitrary")),
        name="moe_experts",
    )(tile_expert, tile_valid, tile_src, hs, wg, wu, wd)


def _combine_kernel(slot_ref, next_slot_ref, ys_ref, x_ref, mod_ref, route_ref, o_ref, buf, sem, *, tt):
    step = pl.program_id(0) * pl.num_programs(1) + pl.program_id(1)
    n_steps = pl.num_programs(0) * pl.num_programs(1)
    parity = step % 2

    def gather(slots, p):
        def issue(r, carry):
            _row_copy(ys_ref, slots[0, 0, r], buf.at[p, 0], r, sem.at[p]).start()
            _row_copy(ys_ref, slots[0, 0, tt + r], buf.at[p, 1], r, sem.at[p]).start()
            return carry

        lax.fori_loop(0, tt, issue, 0, unroll=DMA_ISSUE_UNROLL)

    @pl.when(step == 0)
    def _():
        gather(slot_ref, 0)

    @pl.when(step + 1 < n_steps)
    def _():
        gather(next_slot_ref, 1 - parity)

    for choice in range(2):
        pltpu.make_async_copy(ys_ref.at[pl.ds(0, tt), :], buf.at[parity, choice], sem.at[parity]).wait()
    route = route_ref[...]
    y = route[:, 2:3] * buf[parity, 0] + route[:, 3:4] * buf[parity, 1]
    o_ref[0] = x_ref[0] + mod_ref[0][5:6] * y


def _combine(slots, ys, x, mod, route, tt):
    B, S, D = x.shape
    tt = _tile(S, tt)
    n = S // tt
    blocks = _slot_blocks(slots, tt)
    last = B * n - 1
    return pl.pallas_call(
        functools.partial(_combine_kernel, tt=tt),
        grid=(B, n),
        in_specs=[pl.BlockSpec((1, 1, 2 * tt), lambda b, i: (b * n + i, 0, 0), memory_space=pltpu.SMEM),
                  pl.BlockSpec((1, 1, 2 * tt), lambda b, i: (jnp.minimum(b * n + i + 1, last), 0, 0),
                               memory_space=pltpu.SMEM),
                  pl.BlockSpec(memory_space=pl.ANY),
                  pl.BlockSpec((1, tt, D), lambda b, i: (b, i, 0)),
                  pl.BlockSpec((1, 6, D), lambda b, i: (b, 0, 0)),
                  pl.BlockSpec((tt, LANES), lambda b, i: (b * n + i, 0))],
        out_specs=pl.BlockSpec((1, tt, D), lambda b, i: (b, i, 0)),
        out_shape=jax.ShapeDtypeStruct((B, S, D), F32),
        scratch_shapes=[pltpu.VMEM((2, 2, tt, D), F32), pltpu.SemaphoreType.DMA((2,))],
        compiler_params=_params(("arbitrary", "arbitrary")),
        name="moe_combine",
    )(blocks, blocks, ys, x, mod, route)


def _slot_blocks(slots, tt):
    T = slots.shape[1]
    return slots.reshape(2, T // tt, tt).transpose(1, 0, 2).reshape(T // tt, 1, 2 * tt)


def _routing_tables(route, tm):
    T = route.shape[1]
    experts = route[0:2].astype(jnp.int32).reshape(2 * T)
    onehot = (experts[None, :] == jnp.arange(N_EXPERTS, dtype=jnp.int32)[:, None]).astype(jnp.int32)
    csum = jnp.cumsum(onehot, axis=1)
    rank = jnp.sum((csum - onehot) * onehot, axis=0)
    counts = csum[:, -1]
    tiles = (counts + tm - 1) // tm
    tile_end = jnp.cumsum(tiles)
    offsets = (tile_end - tiles) * tm
    slots = (jnp.sum(onehot * offsets[:, None], axis=0) + rank).reshape(2, T)
    n_tiles = (2 * T) // tm + N_EXPERTS
    ids = jnp.arange(n_tiles, dtype=jnp.int32)
    tile_expert = jnp.minimum(jnp.sum((ids[:, None] >= tile_end[None, :]).astype(jnp.int32), axis=1),
                              N_EXPERTS - 1)
    last_used = jnp.minimum(jnp.sum((tile_end[-1] - 1 >= tile_end).astype(jnp.int32)), N_EXPERTS - 1)
    tile_valid = (ids < tile_end[-1]).astype(jnp.int32)
    tile_expert = jnp.where(tile_valid > 0, tile_expert, last_used)
    tile_src = jnp.minimum(ids, tile_end[-1] - 1)
    last_tile = jnp.maximum(tile_end - 1, 0)
    tail = ids[n_tiles - N_EXPERTS:]
    last_tile = jnp.concatenate([last_tile, jnp.where(tail >= tile_end[-1], tail, last_tile[0])])
    return (slots.astype(jnp.int32), tile_expert.astype(jnp.int32), tile_valid, tile_src.astype(jnp.int32),
            last_tile.astype(jnp.int32), n_tiles)


def _rope_angles(pos, dim):
    inv = ROPE_THETA ** (-jnp.arange(dim // 2, dtype=F32) / (dim // 2))
    return pos.astype(F32)[:, None] * inv[None, :]


def _ret_rope_tables(pos):
    ang = _rope_angles(pos, RET_QK_DIM)
    cos, sin = jnp.cos(ang), jnp.sin(ang)
    return jnp.concatenate([cos, cos], axis=1), jnp.concatenate([-sin, sin], axis=1)


def _axial_rope_tables(L):
    n_rows = L // GRID_W
    row = jnp.broadcast_to(jnp.arange(n_rows)[:, None], (n_rows, GRID_W)).reshape(-1)
    col = jnp.broadcast_to(jnp.arange(GRID_W)[None, :], (n_rows, GRID_W)).reshape(-1)
    ar = _rope_angles(row, DIFF_HEAD_DIM // 2)
    ac = _rope_angles(col, DIFF_HEAD_DIM // 2)
    cos = jnp.concatenate([jnp.cos(ar), jnp.cos(ar), jnp.cos(ac), jnp.cos(ac)], axis=1)
    sin = jnp.concatenate([-jnp.sin(ar), jnp.sin(ar), -jnp.sin(ac), jnp.sin(ac)], axis=1)
    return jnp.tile(cos, (1, LANES // DIFF_HEAD_DIM)), jnp.tile(sin, (1, LANES // DIFF_HEAD_DIM))


def _ret_decay_tables(ret_decay, S):
    C = min(S, CHUNK)
    lam = -jnp.exp(ret_decay.astype(F32))
    pos = jnp.arange(C)
    steps = jnp.stack([pos + 1, C - pos]).astype(F32)
    a = (lam[:, :, None] * steps[:, None, :]).reshape(1, 2 * RET_HEADS, C)
    return a[:, :, :, None], a[:, :, None, :], jnp.ones((1, 2 * RET_HEADS, 1, C), F32)


def kernel(x, c, ctx, c_ctx, even_w_mod, even_b_mod, even_norm1, even_norm2, even_w_in, even_conv_w, even_conv_b, even_dt_bias, even_a_log, even_d, even_ssd_norm, even_ret_decay, even_w_out, even_ffn_gate, even_ffn_up, even_ffn_down, odd_w_mod, odd_b_mod, odd_norm1, odd_norm2, odd_w_in, odd_q_norm, odd_k_norm, odd_lambda, odd_subln, odd_w_out, odd_router, odd_exp_gate, odd_exp_up, odd_exp_down):
    B, L, D = x.shape
    Lc = ctx.shape[1]
    T = B * L
    row = lambda v: v.reshape(1, -1)
    pad_lanes = lambda v: jnp.pad(v.reshape(1, -1), ((0, 0), (0, LANES - v.size)))

    cond = jnp.concatenate([c, c_ctx[None, :], jnp.zeros((SUBLANES - B - 1, D), F32)], axis=0)

    mod = _adaln(cond, even_w_mod[0], even_b_mod[0]).reshape(SUBLANES, 6, D)
    mod_l, mod_c = mod[:B], mod[B:B + 1]
    w_in = even_w_in[0]
    cut = SSD_INNER + SSD_CONV_CH
    w_in = jnp.concatenate([w_in[:, :cut], w_in[:, cut + 2 * SSD_HEADS:], w_in[:, cut:cut + 2 * SSD_HEADS],
                            jnp.zeros((D, LANES - 2 * SSD_HEADS), F32)], axis=1).astype(BF16)
    g1 = row(even_norm1[0])
    cos_c, sin_c = _ret_rope_tables(jnp.arange(Lc))
    cos_l, sin_l = _ret_rope_tables(Lc + jnp.arange(L))
    dtb = pad_lanes(even_dt_bias[0])
    alog = pad_lanes(even_a_log[0])
    conv_b = row(even_conv_b[0])
    ssd_kw = dict(G=SSD_GROUPS, R=SSD_HEADS // SSD_GROUPS, P=SSD_HEAD_DIM, N=SSD_STATE)
    ret_kw = dict(G=RET_HEADS, R=1, P=RET_V_DIM, N=RET_QK_DIM)

    def mix(xin, modv, cosf, sinf, hs, hr, tm):
        z, xbc, dt, rq, rk, rv, rg = _even_in(xin, modv, g1, w_in, cosf, sinf, tm)
        xs, bmat, cmat, acol, arow, wrow = _ssd_prep(xbc, dt, even_conv_w[0], conv_b, dtb, alog, tm)
        racol, rarow, rwrow = _ret_decay_tables(even_ret_decay[0], xin.shape[1])
        (ysf, ysb, hs_f, hs_b), (yrf, yrb, hr_f, hr_b) = _bidir_scans([
            (cmat, bmat, xs, acol, arow, wrow, hs[0], hs[1], ssd_kw),
            (rq, rk, rv, racol, rarow, rwrow, hr[0], hr[1], ret_kw)])
        return (ysf, ysb, xs, z, yrf, yrb, rg), (hs_f, hs_b), (hr_f, hr_b)

    zs = jnp.zeros((B, SSD_GROUPS, SSD_STATE, SSD_INNER // SSD_GROUPS), F32)
    zr = jnp.zeros((B, RET_HEADS, RET_QK_DIM, RET_V_DIM), F32)
    tm_c = 256
    tm_l = 512
    mixed_c, hs, hr = mix(ctx, mod_c, cos_c, sin_c, (zs, zs), (zr, zr), tm_c)
    mixed_l, _, _ = mix(x, mod_l, cos_l, sin_l, hs, hr, tm_l)

    dvec = row(jnp.repeat(even_d[0], SSD_HEAD_DIM))
    snorm = row(even_ssd_norm[0])
    w_out = even_w_out[0].astype(BF16)
    g2 = row(even_norm2[0])
    wg = even_ffn_gate[0].astype(BF16)
    wu = even_ffn_up[0].astype(BF16)
    wd = even_ffn_down[0].astype(BF16)
    ffn_tf = wg.shape[1]

    xl = _even_out(*mixed_l, x, mod_l, dvec, snorm, w_out, tm_l)
    xl = _ffn(xl, mod_l, g2, wg, wu, wd, tm_l, ffn_tf)
    xc = _even_out(*mixed_c, ctx, mod_c, dvec, snorm, w_out, tm_c)
    xc = _ffn(xc, mod_c, g2, wg, wu, wd, tm_c, ffn_tf)

    mod = _adaln(cond, odd_w_mod[0], odd_b_mod[0]).reshape(SUBLANES, 6, D)
    mod_l, mod_c = mod[:B], mod[B:B + 1]
    lam_init = 0.8 - 0.6 * math.exp(-0.3 * 1)
    lq1, lk1, lq2, lk2 = odd_lambda[0]
    lam = jnp.exp(jnp.sum(lq1 * lk1)) - jnp.exp(jnp.sum(lq2 * lk2)) + lam_init
    bound = 1.02 * DIFF_HEAD_DIM ** 0.5 * jnp.max(jnp.abs(odd_q_norm[0])) * jnp.max(jnp.abs(odd_k_norm[0]))
    lam = jnp.stack([lam, bound - SHIFT_HEADROOM, (bound <= SHIFT_MAX_BOUND).astype(F32),
                     jnp.zeros((), F32)]).reshape(1, 4)
    w_in = odd_w_in[0].astype(BF16)
    g1 = row(odd_norm1[0])
    qn = row(jnp.tile(odd_q_norm[0], LANES // DIFF_HEAD_DIM))
    kn = row(jnp.tile(odd_k_norm[0], LANES // DIFF_HEAD_DIM))
    gid = jnp.arange(MXU_N) // DIFF_HEAD_DIM
    bd = (gid[:, None] == gid[None, :]).astype(BF16)
    cos_l, sin_l = _axial_rope_tables(L)
    cos_c, sin_c = jnp.ones((Lc, LANES), F32), jnp.zeros((Lc, LANES), F32)
    q_l, k_l, v_l = _odd_in(xl, mod_l, g1, w_in, qn, kn, bd, cos_l, sin_l, tm_l)
    _, k_c, v_c = _odd_in(xc, mod_c, g1, w_in, qn, kn, bd, cos_c, sin_c, tm_c)
    o = _attention(lam, q_l, k_l, v_l, k_c, v_c, row(odd_subln[0]), 1.0 - lam_init, 2048, 512, 4)

    w_router = jnp.pad(jnp.concatenate(_split3(odd_router[0]), axis=1), ((0, 0), (0, LANES - 3 * N_EXPERTS)))
    xl, h, route, route_t = _odd_out(o, xl, mod_l, row(odd_norm2[0]), odd_w_out[0].astype(BF16), w_router, tm_l)
    moe_tm = min(1024, T)
    slots, tile_expert, tile_valid, tile_src, last_tile, n_tiles = _routing_tables(route_t, moe_tm)
    hs_sorted = _dispatch(last_tile, slots, h, n_tiles * moe_tm, 512, moe_tm)
    ys_sorted = _experts(tile_expert, tile_valid, tile_src, hs_sorted, odd_exp_gate[0], odd_exp_up[0],
                         odd_exp_down[0], moe_tm, 512)
    return _combine(slots, ys_sorted, xl, mod_l, route, 256)
```

```python
import functools
import math

import jax
import jax.numpy as jnp
from jax import lax
from jax.experimental import pallas as pl
from jax.experimental.pallas import tpu as pltpu

F32 = jnp.float32
BF16 = jnp.bfloat16
LANES = 128
SUBLANES = 8
MXU_N = 256
VMEM_LIMIT = 56 * 1024 * 1024

GRID_W = 64
CHUNK = 128
NORM_EPS = 1e-6
ROPE_THETA = 10000.0
SSD_HEADS = 16
SSD_HEAD_DIM = 64
SSD_INNER = SSD_HEADS * SSD_HEAD_DIM
SSD_GROUPS = 2
SSD_STATE = 128
SSD_BC = SSD_GROUPS * SSD_STATE
SSD_CONV_CH = SSD_INNER + 2 * SSD_BC
RET_HEADS = 4
RET_QK_DIM = 128
RET_V_DIM = 256
RET_QK = RET_HEADS * RET_QK_DIM
RET_INNER = RET_HEADS * RET_V_DIM
DIFF_HEADS = 8
DIFF_HEAD_DIM = 64
DIFF_V_DIM = 2 * DIFF_HEAD_DIM
DIFF_INNER = DIFF_HEADS * DIFF_V_DIM
N_EXPERTS = 8
NEG_BIG = -1e30
SHIFT_HEADROOM = 30.0
SHIFT_MAX_BOUND = 55.0
DMA_ISSUE_UNROLL = 8


def _params(sem):
    return pltpu.CompilerParams(dimension_semantics=sem, vmem_limit_bytes=VMEM_LIMIT)


def _tile(n, pref):
    t = min(n, pref)
    assert n % t == 0, (n, t)
    return t


def _silu(x):
    return x * jax.nn.sigmoid(x)


def _rms(x):
    return x * lax.rsqrt(jnp.mean(x * x, axis=-1, keepdims=True) + NORM_EPS)


def _modulate(x, gain, shift, scale):
    return _rms(x) * gain * (1.0 + scale) + shift


def _dot(a, b):
    return jnp.dot(a, b, preferred_element_type=F32)


def _dot_nt(a, b):
    return lax.dot_general(a, b, (((1,), (1,)), ((), ())), preferred_element_type=F32)


def _split3(a):
    hi = a.astype(BF16)
    r1 = a - hi.astype(F32)
    mid = r1.astype(BF16)
    lo = (r1 - mid.astype(F32)).astype(BF16)
    return hi, mid, lo


def _adaln_kernel(c_ref, w_ref, b_ref, o_ref):
    a = _silu(c_ref[...]).astype(BF16)
    o_ref[...] = _dot(a, w_ref[...].astype(BF16)) + b_ref[...]


def _adaln(cond, w, b):
    R, D = cond.shape
    N = w.shape[1]
    tn = _tile(N, 1024)
    return pl.pallas_call(
        _adaln_kernel,
        grid=(N // tn,),
        in_specs=[pl.BlockSpec((R, D), lambda j: (0, 0)),
                  pl.BlockSpec((D, tn), lambda j: (0, j)),
                  pl.BlockSpec((1, tn), lambda j: (0, j))],
        out_specs=pl.BlockSpec((R, tn), lambda j: (0, j)),
        out_shape=jax.ShapeDtypeStruct((R, N), F32),
        compiler_params=_params(("arbitrary",)),
        name="adaln",
    )(cond, w, b.reshape(1, N))


EV_Z = 0
EV_XBC = EV_Z + SSD_INNER
EV_RQ = EV_XBC + SSD_CONV_CH
EV_RK = EV_RQ + RET_QK
EV_RV = EV_RK + RET_QK
EV_RG = EV_RV + RET_INNER
EV_DT = EV_RG + RET_INNER
EV_END = EV_DT + LANES


def _even_in_kernel(x_ref, mod_ref, gain_ref, w_ref, cos_ref, sin_ref,
                    z_ref, xbc_ref, dt_ref, rq_ref, rk_ref, rv_ref, rg_ref):
    mod = mod_ref[0]
    h = _modulate(x_ref[0], gain_ref[...], mod[0:1], mod[1:2]).astype(BF16)

    def mm(lo, hi):
        return _dot(h, w_ref[:, lo:hi])

    z_ref[0] = mm(EV_Z, EV_XBC).astype(BF16)
    xbc_ref[0] = mm(EV_XBC, EV_RQ)
    cos = cos_ref[...]
    sin = sin_ref[...]
    k_scale = RET_QK_DIM ** -0.5
    qk = mm(EV_RQ, EV_RV)
    for j in range(RET_HEADS):
        lo = j * RET_QK_DIM
        a = qk[:, lo:lo + RET_QK_DIM]
        rq_ref[0, :, lo:lo + RET_QK_DIM] = (a * cos + pltpu.roll(a, RET_QK_DIM // 2, 1) * sin).astype(BF16)
        a = qk[:, RET_QK + lo:RET_QK + lo + RET_QK_DIM]
        a = (a * cos + pltpu.roll(a, RET_QK_DIM // 2, 1) * sin) * k_scale
        rk_ref[0, lo:lo + RET_QK_DIM, :] = a.T.astype(BF16)
    rv_ref[0] = mm(EV_RV, EV_RG).astype(BF16)
    rg_ref[0] = mm(EV_RG, EV_DT).astype(BF16)
    dt_ref[0] = mm(EV_DT, EV_END)


def _even_in(x, mod, gain, w, cosf, sinf, tm):
    B, S, D = x.shape
    tm = _tile(S, tm)
    bm = mod.shape[0]
    mod_map = (lambda b, i: (b, 0, 0)) if bm > 1 else (lambda b, i: (0, 0, 0))
    row = lambda n: pl.BlockSpec((1, tm, n), lambda b, i: (b, i, 0))
    outs = [(SSD_INNER, BF16), (SSD_CONV_CH, F32), (LANES, F32), (RET_QK, BF16), (RET_QK, BF16),
            (RET_INNER, BF16), (RET_INNER, BF16)]
    RK_OUT = 4
    return pl.pallas_call(
        _even_in_kernel,
        grid=(B, S // tm),
        in_specs=[row(D),
                  pl.BlockSpec((1, 6, D), mod_map),
                  pl.BlockSpec((1, D), lambda b, i: (0, 0)),
                  pl.BlockSpec((D, EV_END), lambda b, i: (0, 0)),
                  pl.BlockSpec((tm, LANES), lambda b, i: (i, 0)),
                  pl.BlockSpec((tm, LANES), lambda b, i: (i, 0))],
        out_specs=[pl.BlockSpec((1, n, tm), lambda b, i: (b, 0, i)) if j == RK_OUT else row(n)
                   for j, (n, _) in enumerate(outs)],
        out_shape=[jax.ShapeDtypeStruct((B, n, S) if j == RK_OUT else (B, S, n), dt)
                   for j, (n, dt) in enumerate(outs)],
        compiler_params=_params(("arbitrary", "arbitrary")),
        name="even_in",
    )(x, mod, gain, w, cosf, sinf)


def _ssd_prep_kernel(xbc_ref, prev_ref, next_ref, dt_ref, cw_ref, cb_ref, dtb_ref, alog_ref,
                     xs_ref, bm_ref, cm_ref, acol_ref, arow_ref, wrow_ref, *, tm, chunk):
    i = pl.program_id(1)
    n = pl.num_programs(1)
    brow = lax.broadcasted_iota(jnp.int32, (chunk, 1), 0)
    for cb in range(SSD_CONV_CH // LANES):
        cols = slice(cb * LANES, (cb + 1) * LANES)
        w0, w1, w2 = cw_ref[0:1, cols], cw_ref[1:2, cols], cw_ref[2:3, cols]
        bias = cb_ref[:, cols]
        zero_row = jnp.zeros((1, LANES), F32)
        for rb in range(tm // chunk):
            r0 = rb * chunk
            x = xbc_ref[0, r0:r0 + chunk, cols]
            if rb == 0:
                above = jnp.where(i > 0, prev_ref[0, SUBLANES - 1:SUBLANES, cols], zero_row)
            else:
                above = xbc_ref[0, r0 - 1:r0, cols]
            if rb == tm // chunk - 1:
                below = jnp.where(i < n - 1, next_ref[0, 0:1, cols], zero_row)
            else:
                below = xbc_ref[0, r0 + chunk:r0 + chunk + 1, cols]
            xp = jnp.where(brow == 0, above, pltpu.roll(x, 1, 0))
            xn = jnp.where(brow == chunk - 1, below, pltpu.roll(x, chunk - 1, 0))
            y = _silu(xp * w0 + x * w1 + xn * w2 + bias)
            if cb * LANES < SSD_INNER:
                xs_ref[0, r0:r0 + chunk, cols] = y.astype(BF16)
            elif cb * LANES < SSD_INNER + SSD_BC:
                lo = cb * LANES - SSD_INNER
                bm_ref[0, lo:lo + LANES, r0:r0 + chunk] = y.T.astype(BF16)
            else:
                lo = cb * LANES - SSD_INNER - SSD_BC
                cm_ref[0, r0:r0 + chunk, lo:lo + LANES] = y.astype(BF16)

    rows = lax.broadcasted_iota(jnp.int32, (tm, 1), 0)
    t = dt_ref[0] + dtb_ref[...]
    dt = jnp.maximum(t, 0.0) + jnp.log(1.0 + jnp.exp(-jnp.abs(t)))
    la = -dt * jnp.exp(alog_ref[...])
    rmod = rows % chunk
    fwd = la
    rev = la
    sh = 1
    while sh < chunk:
        fwd = fwd + jnp.where(rmod >= sh, pltpu.roll(fwd, sh, 0), 0.0)
        rev = rev + jnp.where(rmod < chunk - sh, pltpu.roll(rev, tm - sh, 0), 0.0)
        sh *= 2
    lane = lax.broadcasted_iota(jnp.int32, (1, LANES), 1)
    acc = jnp.where(lane < SSD_HEADS, fwd, rev)
    acc_t = acc.T
    dt_t = dt.T
    hp = SSD_HEADS // SSD_GROUPS
    for j in range(2 * SSD_GROUPS):
        acol_ref[0, j] = acc[:, j * hp:(j + 1) * hp]
        arow_ref[0, j] = acc_t[j * hp:(j + 1) * hp, :]
        wrow_ref[0, j] = dt_t[j * hp:(j + 1) * hp, :]


def _ssd_prep(xbc, dt, conv_w, conv_b, dtb, alog, tm):
    B, S, _ = xbc.shape
    tm = _tile(S, tm)
    assert tm % CHUNK == 0
    nh = S // SUBLANES
    hp = SSD_HEADS // SSD_GROUPS
    gd = 2 * SSD_GROUPS
    row = lambda n: pl.BlockSpec((1, tm, n), lambda b, i: (b, i, 0))
    const = lambda r, n: pl.BlockSpec((r, n), lambda b, i: (0, 0))
    return pl.pallas_call(
        functools.partial(_ssd_prep_kernel, tm=tm, chunk=CHUNK),
        grid=(B, S // tm),
        in_specs=[row(SSD_CONV_CH),
                  pl.BlockSpec((1, SUBLANES, SSD_CONV_CH),
                               lambda b, i: (b, jnp.maximum(i * (tm // SUBLANES) - 1, 0), 0)),
                  pl.BlockSpec((1, SUBLANES, SSD_CONV_CH),
                               lambda b, i: (b, jnp.minimum((i + 1) * (tm // SUBLANES), nh - 1), 0)),
                  row(LANES),
                  const(3, SSD_CONV_CH), const(1, SSD_CONV_CH), const(1, LANES), const(1, LANES)],
        out_specs=[row(SSD_INNER), pl.BlockSpec((1, SSD_BC, tm), lambda b, i: (b, 0, i)), row(SSD_BC),
                   pl.BlockSpec((1, gd, tm, hp), lambda b, i: (b, 0, i, 0)),
                   pl.BlockSpec((1, gd, hp, tm), lambda b, i: (b, 0, 0, i)),
                   pl.BlockSpec((1, gd, hp, tm), lambda b, i: (b, 0, 0, i))],
        out_shape=[jax.ShapeDtypeStruct((B, S, SSD_INNER), BF16),
                   jax.ShapeDtypeStruct((B, SSD_BC, S), BF16),
                   jax.ShapeDtypeStruct((B, S, SSD_BC), BF16),
                   jax.ShapeDtypeStruct((B, gd, S, hp), F32),
                   jax.ShapeDtypeStruct((B, gd, hp, S), F32),
                   jax.ShapeDtypeStruct((B, gd, hp, S), F32)],
        compiler_params=_params(("arbitrary", "arbitrary")),
        name="ssd_prep",
    )(xbc, xbc, xbc, dt, conv_w, conv_b, dtb, alog)


def _scan_direction(q_ref, k_ref, v_ref, ac_ref, ar_ref, wr_ref, y_ref, h_scr, d, *, G, R, P, N, C):
    reverse = d == 1
    li = lax.broadcasted_iota(jnp.int32, (C, C), 0)
    si = lax.broadcasted_iota(jnp.int32, (C, C), 1)
    mask = (si >= li) if reverse else (li >= si)
    far = 0 if reverse else C - 1
    width = max(P, LANES)
    heads_per = width // P
    lane_head = lax.broadcasted_iota(jnp.int32, (1, width), 1) // P
    RP = R * P
    n_sub = q_ref.shape[1] // C
    per_chunk_tables = ac_ref.shape[2] > C
    for sub, g in [(sub, g) for sub in (range(n_sub - 1, -1, -1) if reverse else range(n_sub)) for g in range(G)]:
        t0 = sub * C
        tb = t0 if per_chunk_tables else 0
        q = q_ref[0, t0:t0 + C, g * N:(g + 1) * N]
        k = k_ref[0, g * N:(g + 1) * N, t0:t0 + C]
        s = _dot(q, k)
        k_t = k.astype(F32)
        ac = ac_ref[0, g, tb:tb + C, :]
        ar = ar_ref[0, g, :, tb:tb + C]
        wr = wr_ref[0, g, :, tb:tb + C]
        for u in range(RP // width):
            lo = u * width
            v = v_ref[0, t0:t0 + C, g * RP + lo:g * RP + lo + width].astype(BF16)
            h_old = h_scr[d, g, :, lo:lo + width]
            y_diag = None
            for j in range(heads_per):
                r = u * heads_per + j
                a_col = ac[:, r:r + 1]
                a_row = ar[r:r + 1, :]
                w_row = wr[r:r + 1, :]
                tot = a_row[:, far:far + 1]
                decay = jnp.exp(jnp.where(mask, a_col - a_row, NEG_BIG))
                m = (s * decay * w_row).astype(BF16)
                kw = (k_t * (jnp.exp(tot - a_row) * w_row)).astype(BF16)
                yd = _dot(m, v)
                st = _dot(kw, v)
                e_col = jnp.exp(a_col)
                e_tot = jnp.exp(tot)
                if y_diag is None:
                    y_diag, state, col_scale, tot_scale = yd, st, e_col, e_tot
                else:
                    sel = lane_head == j
                    y_diag = jnp.where(sel, yd, y_diag)
                    state = jnp.where(sel, st, state)
                    col_scale = jnp.where(sel, e_col, col_scale)
                    tot_scale = jnp.where(sel, e_tot, tot_scale)
            y = y_diag + col_scale * _dot(q, h_old.astype(BF16))
            y_ref[0, t0:t0 + C, g * RP + lo:g * RP + lo + width] = y.astype(y_ref.dtype)
            h_scr[d, g, :, lo:lo + width] = tot_scale * h_old + state


SCAN_CHUNKS_PER_STEP = 4
SCAN_INS = 14
SCAN_OUTS = 4


def _scan_kernel(*refs, families):
    n = len(families)
    ins = [refs[SCAN_INS * i:SCAN_INS * (i + 1)] for i in range(n)]
    outs = [refs[SCAN_INS * n + SCAN_OUTS * i:SCAN_INS * n + SCAN_OUTS * (i + 1)] for i in range(n)]
    scrs = refs[(SCAN_INS + SCAN_OUTS) * n:]
    c = pl.program_id(1)

    @pl.when(c == 0)
    def _():
        for fam_in, h_scr in zip(ins, scrs):
            h_scr[0] = fam_in[12][0]
            h_scr[1] = fam_in[13][0]

    for dims, fam_in, fam_out, h_scr in zip(families, ins, outs, scrs):
        qf, kf, vf, qb, kb, vb, acf, arf, wrf, acb, arb, wrb = fam_in[:12]
        _scan_direction(qf, kf, vf, acf, arf, wrf, fam_out[0], h_scr, 0, **dims)
        _scan_direction(qb, kb, vb, acb, arb, wrb, fam_out[1], h_scr, 1, **dims)

    @pl.when(c == pl.num_programs(1) - 1)
    def _():
        for fam_out, h_scr in zip(outs, scrs):
            fam_out[2][0] = h_scr[0]
            fam_out[3][0] = h_scr[1]


def _bidir_scans(families):
    B, S, _ = families[0][0].shape
    C = _tile(S, CHUNK)
    blk = _tile(S, SCAN_CHUNKS_PER_STEP * C)
    nc = S // blk
    fwd = lambda c: c
    rev = lambda c: nc - 1 - c
    in_specs, out_specs, out_shape, scratch, args, dims_list = [], [], [], [], [], []
    for q, k, v, acol, arow, wrow, h0_f, h0_b, dims in families:
        G, R, P, N = dims["G"], dims["R"], dims["P"], dims["N"]
        RP = R * P
        bx = (lambda b: b) if acol.shape[0] > 1 else (lambda b: 0)
        seq = lambda n, ci: pl.BlockSpec((1, blk, n), lambda b, c, ci=ci: (b, ci(c), 0))
        per_chunk = acol.shape[2] > C
        tc = (lambda ci: ci) if per_chunk else (lambda ci: (lambda c: 0))
        tl = blk if per_chunk else C
        col = lambda d, ci, G=G, R=R, bx=bx, tc=tc, tl=tl: pl.BlockSpec(
            (1, G, tl, R), lambda b, c: (bx(b), d, tc(ci)(c), 0))
        rowt = lambda d, ci, G=G, R=R, bx=bx, tc=tc, tl=tl: pl.BlockSpec(
            (1, G, R, tl), lambda b, c: (bx(b), d, 0, tc(ci)(c)))
        state = pl.BlockSpec((1, G, N, RP), lambda b, c: (b, 0, 0, 0))
        seq_t = lambda n, ci: pl.BlockSpec((1, n, blk), lambda b, c, ci=ci: (b, 0, ci(c)))
        in_specs += [seq(G * N, fwd), seq_t(G * N, fwd), seq(G * RP, fwd),
                     seq(G * N, rev), seq_t(G * N, rev), seq(G * RP, rev),
                     col(0, fwd), rowt(0, fwd), rowt(0, fwd), col(1, rev), rowt(1, rev), rowt(1, rev),
                     state, state]
        out_specs += [seq(G * RP, fwd), seq(G * RP, rev), state, state]
        out_shape += [jax.ShapeDtypeStruct((B, S, G * RP), BF16), jax.ShapeDtypeStruct((B, S, G * RP), BF16),
                      jax.ShapeDtypeStruct((B, G, N, RP), F32), jax.ShapeDtypeStruct((B, G, N, RP), F32)]
        scratch.append(pltpu.VMEM((2, G, N, RP), F32))
        args += [q, k, v, q, k, v, acol, arow, wrow, acol, arow, wrow, h0_f, h0_b]
        dims_list.append(dict(dims, C=C))
    res = pl.pallas_call(
        functools.partial(_scan_kernel, families=tuple(dims_list)),
        grid=(B, nc),
        in_specs=in_specs,
        out_specs=out_specs,
        out_shape=out_shape,
        scratch_shapes=scratch,
        compiler_params=_params(("arbitrary", "arbitrary")),
        name="bidir_scans",
    )(*args)
    return [res[SCAN_OUTS * i:SCAN_OUTS * (i + 1)] for i in range(len(families))]


def _even_out_kernel(ysf_ref, ysb_ref, xs_ref, z_ref, yrf_ref, yrb_ref, rg_ref, x_ref, mod_ref, dvec_ref, sn_ref,
                     w_ref, o_ref):
    f32 = lambda ref: ref[0].astype(F32)
    y = f32(ysf_ref) + f32(ysb_ref) + f32(xs_ref) * dvec_ref[...]
    y = _rms(y * _silu(f32(z_ref))) * sn_ref[...]
    out = _dot(y.astype(BF16), w_ref[0:SSD_INNER, :])
    yr = f32(yrf_ref) + f32(yrb_ref)
    rg = _silu(f32(rg_ref))
    for j in range(RET_HEADS):
        lo = j * RET_V_DIM
        r = _rms(yr[:, lo:lo + RET_V_DIM]) * rg[:, lo:lo + RET_V_DIM]
        out = out + _dot(r.astype(BF16), w_ref[SSD_INNER + lo:SSD_INNER + lo + RET_V_DIM, :])
    o_ref[0] = x_ref[0] + mod_ref[0][2:3] * out


def _even_out(ysf, ysb, xs, z, yrf, yrb, rg, x, mod, dvec, ssd_norm, w, tm):
    B, S, D = x.shape
    tm = _tile(S, tm)
    bm = mod.shape[0]
    mod_map = (lambda b, i: (b, 0, 0)) if bm > 1 else (lambda b, i: (0, 0, 0))
    row = lambda n: pl.BlockSpec((1, tm, n), lambda b, i: (b, i, 0))
    const = lambda r, n: pl.BlockSpec((r, n), lambda b, i: (0, 0))
    return pl.pallas_call(
        _even_out_kernel,
        grid=(B, S // tm),
        in_specs=[row(SSD_INNER), row(SSD_INNER), row(SSD_INNER), row(SSD_INNER),
                  row(RET_INNER), row(RET_INNER), row(RET_INNER), row(D),
                  pl.BlockSpec((1, 6, D), mod_map),
                  const(1, SSD_INNER), const(1, SSD_INNER), const(SSD_INNER + RET_INNER, D)],
        out_specs=row(D),
        out_shape=jax.ShapeDtypeStruct((B, S, D), F32),
        compiler_params=_params(("arbitrary", "arbitrary")),
        name="even_out",
    )(ysf, ysb, xs, z, yrf, yrb, rg, x, mod, dvec, ssd_norm, w)


def _ffn_kernel(x_ref, mod_ref, gain_ref, wg_ref, wu_ref, wd_ref, o_ref, h_scr, acc_scr):
    f = pl.program_id(2)
    mod = mod_ref[0]

    @pl.when(f == 0)
    def _():
        h_scr[...] = _modulate(x_ref[0], gain_ref[...], mod[3:4], mod[4:5]).astype(BF16)
        acc_scr[...] = jnp.zeros_like(acc_scr)

    h = h_scr[...]
    a = (_silu(_dot(h, wg_ref[...])) * _dot(h, wu_ref[...])).astype(BF16)
    acc_scr[...] += _dot(a, wd_ref[...])

    @pl.when(f == pl.num_programs(2) - 1)
    def _():
        o_ref[0] = x_ref[0] + mod[5:6] * acc_scr[...]


def _ffn(x, mod, gain, wg, wu, wd, tm, tf):
    B, S, D = x.shape
    F = wg.shape[1]
    tm = _tile(S, tm)
    tf = _tile(F, tf)
    bm = mod.shape[0]
    mod_map = (lambda b, i, f: (b, 0, 0)) if bm > 1 else (lambda b, i, f: (0, 0, 0))
    resident = dict(pipeline_mode=pl.Buffered(1)) if tf == F else {}
    return pl.pallas_call(
        _ffn_kernel,
        grid=(B, S // tm, F // tf),
        in_specs=[pl.BlockSpec((1, tm, D), lambda b, i, f: (b, i, 0)),
                  pl.BlockSpec((1, 6, D), mod_map),
                  pl.BlockSpec((1, D), lambda b, i, f: (0, 0)),
                  pl.BlockSpec((D, tf), lambda b, i, f: (0, f), **resident),
                  pl.BlockSpec((D, tf), lambda b, i, f: (0, f), **resident),
                  pl.BlockSpec((tf, D), lambda b, i, f: (f, 0), **resident)],
        out_specs=pl.BlockSpec((1, tm, D), lambda b, i, f: (b, i, 0)),
        out_shape=jax.ShapeDtypeStruct((B, S, D), F32),
        scratch_shapes=[pltpu.VMEM((tm, D), BF16), pltpu.VMEM((tm, D), F32)],
        compiler_params=_params(("arbitrary", "arbitrary", "arbitrary")),
        name="ffn_dense",
    )(x, mod, gain, wg, wu, wd)


def _odd_in_kernel(x_ref, mod_ref, gain_ref, w_ref, qn_ref, kn_ref, bd_ref, cos_ref, sin_ref,
                   q_ref, k_ref, v_ref):
    mod = mod_ref[0]
    h = _modulate(x_ref[0], gain_ref[...], mod[0:1], mod[1:2]).astype(BF16)
    cos = cos_ref[...]
    sin = sin_ref[...]
    bd = bd_ref[...]
    lane = lax.broadcasted_iota(jnp.int32, (1, LANES), 1)
    quarter = DIFF_HEAD_DIM // 4
    first = (lane % (2 * quarter)) < quarter
    for base, n_ref, o_ref, scale in ((0, qn_ref, q_ref, DIFF_HEAD_DIM ** -0.5),
                                      (DIFF_INNER, kn_ref, k_ref, 1.0)):
        for j in range(DIFF_INNER // MXU_N):
            lo = j * MXU_N
            a = _dot(h, w_ref[:, base + lo:base + lo + MXU_N])
            sq = a * a
            hi = sq.astype(BF16)
            rest = (sq - hi.astype(F32)).astype(BF16)
            ms = (_dot(hi, bd) + _dot(rest, bd)) * (1.0 / DIFF_HEAD_DIM)
            a = a * lax.rsqrt(ms + NORM_EPS)
            for c in range(MXU_N // LANES):
                b = a[:, c * LANES:(c + 1) * LANES] * n_ref[...]
                partner = jnp.where(first, pltpu.roll(b, LANES - quarter, 1), pltpu.roll(b, quarter, 1))
                o_ref[0, :, lo + c * LANES:lo + (c + 1) * LANES] = ((b * cos + partner * sin) * scale).astype(BF16)
    v_ref[0] = _dot(h, w_ref[:, 2 * DIFF_INNER:3 * DIFF_INNER]).astype(BF16)


def _odd_in(x, mod, gain, w, qn, kn, bd, cosf, sinf, tm):
    B, S, D = x.shape
    tm = _tile(S, tm)
    bm = mod.shape[0]
    mod_map = (lambda b, i: (b, 0, 0)) if bm > 1 else (lambda b, i: (0, 0, 0))
    row = lambda n: pl.BlockSpec((1, tm, n), lambda b, i: (b, i, 0))
    const = lambda r, n: pl.BlockSpec((r, n), lambda b, i: (0, 0))
    return pl.pallas_call(
        _odd_in_kernel,
        grid=(B, S // tm),
        in_specs=[row(D), pl.BlockSpec((1, 6, D), mod_map), const(1, D), const(D, 3 * DIFF_INNER),
                  const(1, LANES), const(1, LANES), const(MXU_N, MXU_N),
                  pl.BlockSpec((tm, LANES), lambda b, i: (i, 0)),
                  pl.BlockSpec((tm, LANES), lambda b, i: (i, 0))],
        out_specs=[row(DIFF_INNER)] * 3,
        out_shape=[jax.ShapeDtypeStruct((B, S, DIFF_INNER), BF16)] * 3,
        compiler_params=_params(("arbitrary", "arbitrary")),
        name="odd_in",
    )(x, mod, gain, w, qn, kn, bd, cosf, sinf)


def _attn_kernel(sc_ref, q_ref, kl_ref, vl_ref, kc_ref, vc_ref, sub_ref, o_ref,
                 q2_scr, m_scr, l_scr, acc_scr, *, tq, tk, unroll, out_scale):
    q = q_ref[0]
    lane = lax.broadcasted_iota(jnp.int32, (1, LANES), 1)
    lo_half = lane < DIFF_HEAD_DIM
    zero = jnp.zeros_like(q)
    q2_scr[0:tq, :] = jnp.where(lo_half, q, zero)
    q2_scr[tq:2 * tq, :] = jnp.where(lo_half, zero, q)
    l_scr[...] = jnp.zeros_like(l_scr)
    acc_scr[...] = jnp.zeros_like(acc_scr)
    n_lat = kl_ref.shape[1] // tk
    use_shift = sc_ref[0, 2] > 0.0

    def over_keys(update, unroll):
        def body(j, carry):
            start = pl.multiple_of(j * tk, tk)
            update(kl_ref[0, pl.ds(start, tk), :], vl_ref[0, pl.ds(start, tk), :])
            return carry

        lax.fori_loop(0, n_lat, body, 0, unroll=unroll)
        update(kc_ref[0], vc_ref[0])

    @pl.when(use_shift)
    def _():
        shift = sc_ref[0, 1]

        def update(k, v):
            p = jnp.exp(_dot_nt(q2_scr[...], k) - shift)
            part = p[:, 0:LANES]
            for c in range(1, p.shape[1] // LANES):
                part = part + p[:, c * LANES:(c + 1) * LANES]
            l_scr[...] += part
            acc_scr[...] += _dot(p.astype(BF16), v)

        over_keys(update, unroll)
        acc_scr[...] = acc_scr[...] / jnp.sum(l_scr[...], axis=1, keepdims=True)

    @pl.when(jnp.logical_not(use_shift))
    def _():
        m_scr[...] = jnp.full_like(m_scr, -jnp.inf)

        def update(k, v):
            s = _dot_nt(q2_scr[...], k)
            m_prev = m_scr[...]
            m_new = jnp.maximum(m_prev, jnp.max(s, axis=1, keepdims=True))
            alpha = jnp.exp(m_prev - m_new)
            p = jnp.exp(s - m_new[:, 0:1])
            l_scr[...] = alpha * l_scr[...] + jnp.sum(p, axis=1, keepdims=True)
            acc_scr[...] = alpha * acc_scr[...] + _dot(p.astype(BF16), v)
            m_scr[...] = m_new

        over_keys(update, 1)
        acc_scr[...] = acc_scr[...] / l_scr[...]

    o = acc_scr[0:tq, :] - sc_ref[0, 0] * acc_scr[tq:2 * tq, :]
    o_ref[0] = (_rms(o) * sub_ref[...] * out_scale).astype(BF16)


def _attention(lam, q, k_lat, v_lat, k_ctx, v_ctx, subln, out_scale, tq, tk, unroll):
    B, L, _ = q.shape
    Lc = k_ctx.shape[1]
    tq = _tile(L, tq)
    tk = _tile(L, tk)
    unroll = min(unroll, L // tk)
    head = lambda n: pl.BlockSpec((1, n, LANES), lambda b, h, i: (b, 0, h))
    return pl.pallas_call(
        functools.partial(_attn_kernel, tq=tq, tk=tk, unroll=unroll, out_scale=out_scale),
        grid=(B, DIFF_HEADS, L // tq),
        in_specs=[pl.BlockSpec(memory_space=pltpu.SMEM),
                  pl.BlockSpec((1, tq, LANES), lambda b, h, i: (b, i, h)),
                  head(L), head(L), head(Lc), head(Lc),
                  pl.BlockSpec((1, LANES), lambda b, h, i: (0, 0))],
        out_specs=pl.BlockSpec((1, tq, LANES), lambda b, h, i: (b, i, h)),
        out_shape=jax.ShapeDtypeStruct((B, L, DIFF_INNER), BF16),
        scratch_shapes=[pltpu.VMEM((2 * tq, LANES), BF16), pltpu.VMEM((2 * tq, LANES), F32),
                        pltpu.VMEM((2 * tq, LANES), F32), pltpu.VMEM((2 * tq, LANES), F32)],
        compiler_params=_params(("arbitrary", "arbitrary", "arbitrary")),
        name="diff_attn",
    )(lam, q, k_lat, v_lat, k_ctx, v_ctx, subln)


def _odd_out_kernel(o_ref, x_ref, mod_ref, gain_ref, w_ref, wr_ref, xo_ref, h_ref, route_ref, rt_ref):
    mod = mod_ref[0]
    x1 = x_ref[0] + mod[2:3] * _dot(o_ref[0], w_ref[...])
    xo_ref[0] = x1
    h = _modulate(x1, gain_ref[...], mod[3:4], mod[4:5])
    h_ref[...] = h
    wr = wr_ref[...]
    h_hi, h_mid, h_lo = _split3(h)
    parts = _dot(h_lo, wr) + _dot(h_mid, wr) + _dot(h_hi, wr)
    logits = (pltpu.roll(parts, LANES - 2 * N_EXPERTS, 1) + pltpu.roll(parts, LANES - N_EXPERTS, 1)) + parts
    lane = lax.broadcasted_iota(jnp.int32, logits.shape, 1)
    lg = jnp.where(lane < N_EXPERTS, logits, -jnp.inf)
    v1 = jnp.max(lg, axis=1, keepdims=True)
    i1 = jnp.min(jnp.where(lg == v1, lane, LANES), axis=1, keepdims=True)
    lg2 = jnp.where(lane == i1, -jnp.inf, lg)
    v2 = jnp.max(lg2, axis=1, keepdims=True)
    i2 = jnp.min(jnp.where(lg2 == v2, lane, LANES), axis=1, keepdims=True)
    e = jnp.exp(v2 - v1)
    w1 = 1.0 / (1.0 + e)
    w2 = e * w1
    route = jnp.where(lane == 0, i1.astype(F32),
                      jnp.where(lane == 1, i2.astype(F32),
                                jnp.where(lane == 2, w1, jnp.where(lane == 3, w2, 0.0))))
    route_ref[...] = route
    rt_ref[...] = route.T[0:SUBLANES, :]


def _odd_out(o, x, mod, gain, w, w_router, tm):
    B, S, D = x.shape
    tm = _tile(S, tm)
    n = S // tm
    row = lambda n: pl.BlockSpec((1, tm, n), lambda b, i: (b, i, 0))
    flat = lambda w: pl.BlockSpec((tm, w), lambda b, i: (b * n + i, 0))
    const = lambda r, n: pl.BlockSpec((r, n), lambda b, i: (0, 0))
    return pl.pallas_call(
        _odd_out_kernel,
        grid=(B, n),
        in_specs=[row(DIFF_INNER), row(D), pl.BlockSpec((1, 6, D), lambda b, i: (b, 0, 0)),
                  const(1, D), const(DIFF_INNER, D), const(D, LANES)],
        out_specs=[row(D), flat(D), flat(LANES), pl.BlockSpec((SUBLANES, tm), lambda b, i: (0, b * n + i))],
        out_shape=[jax.ShapeDtypeStruct((B, S, D), F32), jax.ShapeDtypeStruct((B * S, D), F32),
                   jax.ShapeDtypeStruct((B * S, LANES), F32), jax.ShapeDtypeStruct((SUBLANES, B * S), F32)],
        compiler_params=_params(("arbitrary", "arbitrary")),
        name="odd_out",
    )(o, x, mod, gain, w, w_router)


def _row_copy(src, src_row, dst, dst_row, sem):
    return pltpu.make_async_copy(src.at[pl.ds(src_row, 1), :], dst.at[pl.ds(dst_row, 1), :], sem)


def _dispatch_kernel(last_tile_ref, slot_ref, h_ref, hs_ref, zero_buf, sem, zero_sem, *, tt, tm):
    @pl.when(pl.program_id(0) == 0)
    def _():
        zero_buf[...] = jnp.zeros_like(zero_buf)
        for e in range(2 * N_EXPERTS):
            start = pl.multiple_of(last_tile_ref[e] * tm, tm)
            clear = pltpu.make_async_copy(zero_buf, hs_ref.at[pl.ds(start, tm), :], zero_sem)
            clear.start()
            clear.wait()

    def issue(r, carry):
        _row_copy(h_ref, r, hs_ref, slot_ref[0, 0, r], sem).start()
        _row_copy(h_ref, r, hs_ref, slot_ref[0, 0, tt + r], sem).start()
        return carry

    lax.fori_loop(0, tt, issue, 0, unroll=DMA_ISSUE_UNROLL)
    for _ in range(2):
        pltpu.make_async_copy(h_ref, hs_ref.at[pl.ds(0, tt), :], sem).wait()


def _dispatch(last_tile, slots, h, n_rows, tt, tm):
    T, D = h.shape
    tt = _tile(T, tt)
    grid_spec = pltpu.PrefetchScalarGridSpec(
        num_scalar_prefetch=1,
        grid=(T // tt,),
        in_specs=[pl.BlockSpec((1, 1, 2 * tt), lambda i, lt: (i, 0, 0), memory_space=pltpu.SMEM),
                  pl.BlockSpec((tt, D), lambda i, lt: (i, 0))],
        out_specs=pl.BlockSpec(memory_space=pl.ANY),
        scratch_shapes=[pltpu.VMEM((tm, D), F32), pltpu.SemaphoreType.DMA(()), pltpu.SemaphoreType.DMA(())])
    return pl.pallas_call(
        functools.partial(_dispatch_kernel, tt=tt, tm=tm),
        grid_spec=grid_spec,
        out_shape=jax.ShapeDtypeStruct((n_rows, D), F32),
        compiler_params=_params(("arbitrary",)),
        name="moe_dispatch",
    )(last_tile, _slot_blocks(slots, tt), h)


def _expert_kernel(te_ref, tv_ref, ts_ref, hs_ref, wg_ref, wu_ref, wd_ref, ys_ref, h_scr, acc_scr):
    del te_ref, ts_ref
    i = pl.program_id(0)
    f = pl.program_id(1)

    @pl.when(tv_ref[i] > 0)
    def _():
        @pl.when(f == 0)
        def _():
            h_scr[...] = hs_ref[...].astype(BF16)
            acc_scr[...] = jnp.zeros_like(acc_scr)

        h = h_scr[...]
        g = _dot(h, wg_ref[0].astype(BF16))
        u = _dot(h, wu_ref[0].astype(BF16))
        acc_scr[...] += _dot((_silu(g) * u).astype(BF16), wd_ref[0].astype(BF16))

        @pl.when(f == pl.num_programs(1) - 1)
        def _():
            ys_ref[...] = acc_scr[...]

    @pl.when((tv_ref[i] == 0) & (f == pl.num_programs(1) - 1))
    def _():
        ys_ref[...] = jnp.zeros_like(ys_ref)


def _experts(tile_expert, tile_valid, tile_src, hs, wg, wu, wd, tm, tf):
    n_rows, D = hs.shape
    F = wg.shape[2]
    tf = _tile(F, tf)
    nf = F // tf
    fi = lambda i, f, tv: jnp.where(tv[i] > 0, f, nf - 1)
    grid_spec = pltpu.PrefetchScalarGridSpec(
        num_scalar_prefetch=3,
        grid=(n_rows // tm, nf),
        in_specs=[pl.BlockSpec((tm, D), lambda i, f, te, tv, ts: (ts[i], 0)),
                  pl.BlockSpec((1, D, tf), lambda i, f, te, tv, ts: (te[i], 0, fi(i, f, tv))),
                  pl.BlockSpec((1, D, tf), lambda i, f, te, tv, ts: (te[i], 0, fi(i, f, tv))),
                  pl.BlockSpec((1, tf, D), lambda i, f, te, tv, ts: (te[i], fi(i, f, tv), 0))],
        out_specs=pl.BlockSpec((tm, D), lambda i, f, te, tv, ts: (i, 0)),
        scratch_shapes=[pltpu.VMEM((tm, D), BF16), pltpu.VMEM((tm, D), F32)])
    return pl.pallas_call(
        _expert_kernel,
        grid_spec=grid_spec,
        out_shape=jax.ShapeDtypeStruct((n_rows, D), F32),
        compiler_params=_params(("arbitrary", "arbitrary")),
        name="moe_experts",
    )(tile_expert, tile_valid, tile_src, hs, wg, wu, wd)


def _combine_kernel(slot_ref, next_slot_ref, ys_ref, x_ref, mod_ref, route_ref, o_ref, buf, sem, *, tt):
    step = pl.program_id(0) * pl.num_programs(1) + pl.program_id(1)
    n_steps = pl.num_programs(0) * pl.num_programs(1)
    parity = step % 2

    def gather(slots, p):
        def issue(r, carry):
            _row_copy(ys_ref, slots[0, 0, r], buf.at[p, 0], r, sem.at[p]).start()
            _row_copy(ys_ref, slots[0, 0, tt + r], buf.at[p, 1], r, sem.at[p]).start()
            return carry

        lax.fori_loop(0, tt, issue, 0, unroll=DMA_ISSUE_UNROLL)

    @pl.when(step == 0)
    def _():
        gather(slot_ref, 0)

    @pl.when(step + 1 < n_steps)
    def _():
        gather(next_slot_ref, 1 - parity)

    for choice in range(2):
        pltpu.make_async_copy(ys_ref.at[pl.ds(0, tt), :], buf.at[parity, choice], sem.at[parity]).wait()
    route = route_ref[...]
    y = route[:, 2:3] * buf[parity, 0] + route[:, 3:4] * buf[parity, 1]
    o_ref[0] = x_ref[0] + mod_ref[0][5:6] * y


def _combine(slots, ys, x, mod, route, tt):
    B, S, D = x.shape
    tt = _tile(S, tt)
    n = S // tt
    blocks = _slot_blocks(slots, tt)
    last = B * n - 1
    return pl.pallas_call(
        functools.partial(_combine_kernel, tt=tt),
        grid=(B, n),
        in_specs=[pl.BlockSpec((1, 1, 2 * tt), lambda b, i: (b * n + i, 0, 0), memory_space=pltpu.SMEM),
                  pl.BlockSpec((1, 1, 2 * tt), lambda b, i: (jnp.minimum(b * n + i + 1, last), 0, 0),
                               memory_space=pltpu.SMEM),
                  pl.BlockSpec(memory_space=pl.ANY),
                  pl.BlockSpec((1, tt, D), lambda b, i: (b, i, 0)),
                  pl.BlockSpec((1, 6, D), lambda b, i: (b, 0, 0)),
                  pl.BlockSpec((tt, LANES), lambda b, i: (b * n + i, 0))],
        out_specs=pl.BlockSpec((1, tt, D), lambda b, i: (b, i, 0)),
        out_shape=jax.ShapeDtypeStruct((B, S, D), F32),
        scratch_shapes=[pltpu.VMEM((2, 2, tt, D), F32), pltpu.SemaphoreType.DMA((2,))],
        compiler_params=_params(("arbitrary", "arbitrary")),
        name="moe_combine",
    )(blocks, blocks, ys, x, mod, route)


def _slot_blocks(slots, tt):
    T = slots.shape[1]
    return slots.reshape(2, T // tt, tt).transpose(1, 0, 2).reshape(T // tt, 1, 2 * tt)


def _routing_tables(route, tm):
    T = route.shape[1]
    experts = route[0:2].astype(jnp.int32).reshape(2 * T)
    onehot = (experts[None, :] == jnp.arange(N_EXPERTS, dtype=jnp.int32)[:, None]).astype(jnp.int32)
    csum = jnp.cumsum(onehot, axis=1)
    rank = jnp.sum((csum - onehot) * onehot, axis=0)
    counts = csum[:, -1]
    tiles = (counts + tm - 1) // tm
    tile_end = jnp.cumsum(tiles)
    offsets = (tile_end - tiles) * tm
    slots = (jnp.sum(onehot * offsets[:, None], axis=0) + rank).reshape(2, T)
    n_tiles = (2 * T) // tm + N_EXPERTS
    ids = jnp.arange(n_tiles, dtype=jnp.int32)
    tile_expert = jnp.minimum(jnp.sum((ids[:, None] >= tile_end[None, :]).astype(jnp.int32), axis=1),
                              N_EXPERTS - 1)
    last_used = jnp.minimum(jnp.sum((tile_end[-1] - 1 >= tile_end).astype(jnp.int32)), N_EXPERTS - 1)
    tile_valid = (ids < tile_end[-1]).astype(jnp.int32)
    tile_expert = jnp.where(tile_valid > 0, tile_expert, last_used)
    tile_src = jnp.minimum(ids, tile_end[-1] - 1)
    last_tile = jnp.maximum(tile_end - 1, 0)
    tail = ids[n_tiles - N_EXPERTS:]
    last_tile = jnp.concatenate([last_tile, jnp.where(tail >= tile_end[-1], tail, last_tile[0])])
    return (slots.astype(jnp.int32), tile_expert.astype(jnp.int32), tile_valid, tile_src.astype(jnp.int32),
            last_tile.astype(jnp.int32), n_tiles)


def _rope_angles(pos, dim):
    inv = ROPE_THETA ** (-jnp.arange(dim // 2, dtype=F32) / (dim // 2))
    return pos.astype(F32)[:, None] * inv[None, :]


def _ret_rope_tables(pos):
    ang = _rope_angles(pos, RET_QK_DIM)
    cos, sin = jnp.cos(ang), jnp.sin(ang)
    return jnp.concatenate([cos, cos], axis=1), jnp.concatenate([-sin, sin], axis=1)


def _axial_rope_tables(L):
    n_rows = L // GRID_W
    row = jnp.broadcast_to(jnp.arange(n_rows)[:, None], (n_rows, GRID_W)).reshape(-1)
    col = jnp.broadcast_to(jnp.arange(GRID_W)[None, :], (n_rows, GRID_W)).reshape(-1)
    ar = _rope_angles(row, DIFF_HEAD_DIM // 2)
    ac = _rope_angles(col, DIFF_HEAD_DIM // 2)
    cos = jnp.concatenate([jnp.cos(ar), jnp.cos(ar), jnp.cos(ac), jnp.cos(ac)], axis=1)
    sin = jnp.concatenate([-jnp.sin(ar), jnp.sin(ar), -jnp.sin(ac), jnp.sin(ac)], axis=1)
    return jnp.tile(cos, (1, LANES // DIFF_HEAD_DIM)), jnp.tile(sin, (1, LANES // DIFF_HEAD_DIM))


def _ret_decay_tables(ret_decay, S):
    C = min(S, CHUNK)
    lam = -jnp.exp(ret_decay.astype(F32))
    pos = jnp.arange(C)
    steps = jnp.stack([pos + 1, C - pos]).astype(F32)
    a = (lam[:, :, None] * steps[:, None, :]).reshape(1, 2 * RET_HEADS, C)
    return a[:, :, :, None], a[:, :, None, :], jnp.ones((1, 2 * RET_HEADS, 1, C), F32)


def kernel(x, c, ctx, c_ctx, even_w_mod, even_b_mod, even_norm1, even_norm2, even_w_in, even_conv_w, even_conv_b, even_dt_bias, even_a_log, even_d, even_ssd_norm, even_ret_decay, even_w_out, even_ffn_gate, even_ffn_up, even_ffn_down, odd_w_mod, odd_b_mod, odd_norm1, odd_norm2, odd_w_in, odd_q_norm, odd_k_norm, odd_lambda, odd_subln, odd_w_out, odd_router, odd_exp_gate, odd_exp_up, odd_exp_down):
    B, L, D = x.shape
    Lc = ctx.shape[1]
    T = B * L
    row = lambda v: v.reshape(1, -1)
    pad_lanes = lambda v: jnp.pad(v.reshape(1, -1), ((0, 0), (0, LANES - v.size)))

    cond = jnp.concatenate([c, c_ctx[None, :], jnp.zeros((SUBLANES - B - 1, D), F32)], axis=0)

    mod = _adaln(cond, even_w_mod[0], even_b_mod[0]).reshape(SUBLANES, 6, D)
    mod_l, mod_c = mod[:B], mod[B:B + 1]
    w_in = even_w_in[0]
    cut = SSD_INNER + SSD_CONV_CH
    w_in = jnp.concatenate([w_in[:, :cut], w_in[:, cut + 2 * SSD_HEADS:], w_in[:, cut:cut + 2 * SSD_HEADS],
                            jnp.zeros((D, LANES - 2 * SSD_HEADS), F32)], axis=1).astype(BF16)
    g1 = row(even_norm1[0])
    cos_c, sin_c = _ret_rope_tables(jnp.arange(Lc))
    cos_l, sin_l = _ret_rope_tables(Lc + jnp.arange(L))
    dtb = pad_lanes(even_dt_bias[0])
    alog = pad_lanes(even_a_log[0])
    conv_b = row(even_conv_b[0])
    ssd_kw = dict(G=SSD_GROUPS, R=SSD_HEADS // SSD_GROUPS, P=SSD_HEAD_DIM, N=SSD_STATE)
    ret_kw = dict(G=RET_HEADS, R=1, P=RET_V_DIM, N=RET_QK_DIM)

    def mix(xin, modv, cosf, sinf, hs, hr, tm):
        z, xbc, dt, rq, rk, rv, rg = _even_in(xin, modv, g1, w_in, cosf, sinf, tm)
        xs, bmat, cmat, acol, arow, wrow = _ssd_prep(xbc, dt, even_conv_w[0], conv_b, dtb, alog, tm)
        racol, rarow, rwrow = _ret_decay_tables(even_ret_decay[0], xin.shape[1])
        (ysf, ysb, hs_f, hs_b), (yrf, yrb, hr_f, hr_b) = _bidir_scans([
            (cmat, bmat, xs, acol, arow, wrow, hs[0], hs[1], ssd_kw),
            (rq, rk, rv, racol, rarow, rwrow, hr[0], hr[1], ret_kw)])
        return (ysf, ysb, xs, z, yrf, yrb, rg), (hs_f, hs_b), (hr_f, hr_b)

    zs = jnp.zeros((B, SSD_GROUPS, SSD_STATE, SSD_INNER // SSD_GROUPS), F32)
    zr = jnp.zeros((B, RET_HEADS, RET_QK_DIM, RET_V_DIM), F32)
    tm_c = 256
    tm_l = 512
    mixed_c, hs, hr = mix(ctx, mod_c, cos_c, sin_c, (zs, zs), (zr, zr), tm_c)
    mixed_l, _, _ = mix(x, mod_l, cos_l, sin_l, hs, hr, tm_l)

    dvec = row(jnp.repeat(even_d[0], SSD_HEAD_DIM))
    snorm = row(even_ssd_norm[0])
    w_out = even_w_out[0].astype(BF16)
    g2 = row(even_norm2[0])
    wg = even_ffn_gate[0].astype(BF16)
    wu = even_ffn_up[0].astype(BF16)
    wd = even_ffn_down[0].astype(BF16)
    ffn_tf = wg.shape[1]

    xl = _even_out(*mixed_l, x, mod_l, dvec, snorm, w_out, tm_l)
    xl = _ffn(xl, mod_l, g2, wg, wu, wd, tm_l, ffn_tf)
    xc = _even_out(*mixed_c, ctx, mod_c, dvec, snorm, w_out, tm_c)
    xc = _ffn(xc, mod_c, g2, wg, wu, wd, tm_c, ffn_tf)

    mod = _adaln(cond, odd_w_mod[0], odd_b_mod[0]).reshape(SUBLANES, 6, D)
    mod_l, mod_c = mod[:B], mod[B:B + 1]
    lam_init = 0.8 - 0.6 * math.exp(-0.3 * 1)
    lq1, lk1, lq2, lk2 = odd_lambda[0]
    lam = jnp.exp(jnp.sum(lq1 * lk1)) - jnp.exp(jnp.sum(lq2 * lk2)) + lam_init
    bound = 1.02 * DIFF_HEAD_DIM ** 0.5 * jnp.max(jnp.abs(odd_q_norm[0])) * jnp.max(jnp.abs(odd_k_norm[0]))
    lam = jnp.stack([lam, bound - SHIFT_HEADROOM, (bound <= SHIFT_MAX_BOUND).astype(F32),
                     jnp.zeros((), F32)]).reshape(1, 4)
    w_in = odd_w_in[0].astype(BF16)
    g1 = row(odd_norm1[0])
    qn = row(jnp.tile(odd_q_norm[0], LANES // DIFF_HEAD_DIM))
    kn = row(jnp.tile(odd_k_norm[0], LANES // DIFF_HEAD_DIM))
    gid = jnp.arange(MXU_N) // DIFF_HEAD_DIM
    bd = (gid[:, None] == gid[None, :]).astype(BF16)
    cos_l, sin_l = _axial_rope_tables(L)
    cos_c, sin_c = jnp.ones((Lc, LANES), F32), jnp.zeros((Lc, LANES), F32)
    q_l, k_l, v_l = _odd_in(xl, mod_l, g1, w_in, qn, kn, bd, cos_l, sin_l, tm_l)
    _, k_c, v_c = _odd_in(xc, mod_c, g1, w_in, qn, kn, bd, cos_c, sin_c, tm_c)
    o = _attention(lam, q_l, k_l, v_l, k_c, v_c, row(odd_subln[0]), 1.0 - lam_init, 2048, 512, 4)

    w_router = jnp.pad(jnp.concatenate(_split3(odd_router[0]), axis=1), ((0, 0), (0, LANES - 3 * N_EXPERTS)))
    xl, h, route, route_t = _odd_out(o, xl, mod_l, row(odd_norm2[0]), odd_w_out[0].astype(BF16), w_router, tm_l)
    moe_tm = min(1024, T)
    slots, tile_expert, tile_valid, tile_src, last_tile, n_tiles = _routing_tables(route_t, moe_tm)
    hs_sorted = _dispatch(last_tile, slots, h, n_tiles * moe_tm, 512, moe_tm)
    ys_sorted = _experts(tile_expert, tile_valid, tile_src, hs_sorted, odd_exp_gate[0], odd_exp_up[0],
                         odd_exp_down[0], moe_tm, 512)
    return _combine(slots, ys_sorted, xl, mod_l, route, 256)
```

```python
import functools
import math

import jax
import jax.numpy as jnp
from jax import lax
from jax.experimental import pallas as pl
from jax.experimental.pallas import tpu as pltpu

F32 = jnp.float32
BF16 = jnp.bfloat16
LANES = 128
SUBLANES = 8
MXU_N = 256
VMEM_LIMIT = 56 * 1024 * 1024

GRID_W = 64
CHUNK = 128
NORM_EPS = 1e-6
ROPE_THETA = 10000.0
SSD_HEADS = 16
SSD_HEAD_DIM = 64
SSD_INNER = SSD_HEADS * SSD_HEAD_DIM
SSD_GROUPS = 2
SSD_STATE = 128
SSD_BC = SSD_GROUPS * SSD_STATE
SSD_CONV_CH = SSD_INNER + 2 * SSD_BC
RET_HEADS = 4
RET_QK_DIM = 128
RET_V_DIM = 256
RET_QK = RET_HEADS * RET_QK_DIM
RET_INNER = RET_HEADS * RET_V_DIM
DIFF_HEADS = 8
DIFF_HEAD_DIM = 64
DIFF_V_DIM = 2 * DIFF_HEAD_DIM
DIFF_INNER = DIFF_HEADS * DIFF_V_DIM
N_EXPERTS = 8
NEG_BIG = -1e30
SHIFT_HEADROOM = 30.0
SHIFT_MAX_BOUND = 55.0
DMA_ISSUE_UNROLL = 8


def _params(sem):
    return pltpu.CompilerParams(dimension_semantics=sem, vmem_limit_bytes=VMEM_LIMIT)


def _tile(n, pref):
    t = min(n, pref)
    assert n % t == 0, (n, t)
    return t


def _silu(x):
    return x * jax.nn.sigmoid(x)


def _rms(x):
    return x * lax.rsqrt(jnp.mean(x * x, axis=-1, keepdims=True) + NORM_EPS)


def _modulate(x, gain, shift, scale):
    return _rms(x) * gain * (1.0 + scale) + shift


def _dot(a, b):
    return jnp.dot(a, b, preferred_element_type=F32)


def _dot_nt(a, b):
    return lax.dot_general(a, b, (((1,), (1,)), ((), ())), preferred_element_type=F32)


def _split3(a):
    hi = a.astype(BF16)
    r1 = a - hi.astype(F32)
    mid = r1.astype(BF16)
    lo = (r1 - mid.astype(F32)).astype(BF16)
    return hi, mid, lo


def _adaln_kernel(c_ref, w_ref, b_ref, o_ref):
    a = _silu(c_ref[...]).astype(BF16)
    o_ref[...] = _dot(a, w_ref[...].astype(BF16)) + b_ref[...]


def _adaln(cond, w, b):
    R, D = cond.shape
    N = w.shape[1]
    tn = _tile(N, 1024)
    return pl.pallas_call(
        _adaln_kernel,
        grid=(N // tn,),
        in_specs=[pl.BlockSpec((R, D), lambda j: (0, 0)),
                  pl.BlockSpec((D, tn), lambda j: (0, j)),
                  pl.BlockSpec((1, tn), lambda j: (0, j))],
        out_specs=pl.BlockSpec((R, tn), lambda j: (0, j)),
        out_shape=jax.ShapeDtypeStruct((R, N), F32),
        compiler_params=_params(("arbitrary",)),
        name="adaln",
    )(cond, w, b.reshape(1, N))


EV_Z = 0
EV_XBC = EV_Z + SSD_INNER
EV_RQ = EV_XBC + SSD_CONV_CH
EV_RK = EV_RQ + RET_QK
EV_RV = EV_RK + RET_QK
EV_RG = EV_RV + RET_INNER
EV_DT = EV_RG + RET_INNER
EV_END = EV_DT + LANES


def _even_in_kernel(x_ref, mod_ref, gain_ref, w_ref, cos_ref, sin_ref,
                    z_ref, xbc_ref, dt_ref, rq_ref, rk_ref, rv_ref, rg_ref):
    mod = mod_ref[0]
    h = _modulate(x_ref[0], gain_ref[...], mod[0:1], mod[1:2]).astype(BF16)

    def mm(lo, hi):
        return _dot(h, w_ref[:, lo:hi])

    z_ref[0] = mm(EV_Z, EV_XBC).astype(BF16)
    xbc_ref[0] = mm(EV_XBC, EV_RQ)
    cos = cos_ref[...]
    sin = sin_ref[...]
    k_scale = RET_QK_DIM ** -0.5
    qk = mm(EV_RQ, EV_RV)
    for j in range(RET_HEADS):
        lo = j * RET_QK_DIM
        a = qk[:, lo:lo + RET_QK_DIM]
        rq_ref[0, :, lo:lo + RET_QK_DIM] = (a * cos + pltpu.roll(a, RET_QK_DIM // 2, 1) * sin).astype(BF16)
        a = qk[:, RET_QK + lo:RET_QK + lo + RET_QK_DIM]
        a = (a * cos + pltpu.roll(a, RET_QK_DIM // 2, 1) * sin) * k_scale
        rk_ref[0, lo:lo + RET_QK_DIM, :] = a.T.astype(BF16)
    rv_ref[0] = mm(EV_RV, EV_RG).astype(BF16)
    rg_ref[0] = mm(EV_RG, EV_DT).astype(BF16)
    dt_ref[0] = mm(EV_DT, EV_END)


def _even_in(x, mod, gain, w, cosf, sinf, tm):
    B, S, D = x.shape
    tm = _tile(S, tm)
    bm = mod.shape[0]
    mod_map = (lambda b, i: (b, 0, 0)) if bm > 1 else (lambda b, i: (0, 0, 0))
    row = lambda n: pl.BlockSpec((1, tm, n), lambda b, i: (b, i, 0))
    outs = [(SSD_INNER, BF16), (SSD_CONV_CH, F32), (LANES, F32), (RET_QK, BF16), (RET_QK, BF16),
            (RET_INNER, BF16), (RET_INNER, BF16)]
    RK_OUT = 4
    return pl.pallas_call(
        _even_in_kernel,
        grid=(B, S // tm),
        in_specs=[row(D),
                  pl.BlockSpec((1, 6, D), mod_map),
                  pl.BlockSpec((1, D), lambda b, i: (0, 0)),
                  pl.BlockSpec((D, EV_END), lambda b, i: (0, 0)),
                  pl.BlockSpec((tm, LANES), lambda b, i: (i, 0)),
                  pl.BlockSpec((tm, LANES), lambda b, i: (i, 0))],
        out_specs=[pl.BlockSpec((1, n, tm), lambda b, i: (b, 0, i)) if j == RK_OUT else row(n)
                   for j, (n, _) in enumerate(outs)],
        out_shape=[jax.ShapeDtypeStruct((B, n, S) if j == RK_OUT else (B, S, n), dt)
                   for j, (n, dt) in enumerate(outs)],
        compiler_params=_params(("arbitrary", "arbitrary")),
        name="even_in",
    )(x, mod, gain, w, cosf, sinf)


def _ssd_prep_kernel(xbc_ref, prev_ref, next_ref, dt_ref, cw_ref, cb_ref, dtb_ref, alog_ref,
                     xs_ref, bm_ref, cm_ref, acol_ref, arow_ref, wrow_ref, *, tm, chunk):
    i = pl.program_id(1)
    n = pl.num_programs(1)
    brow = lax.broadcasted_iota(jnp.int32, (chunk, 1), 0)
    for cb in range(SSD_CONV_CH // LANES):
        cols = slice(cb * LANES, (cb + 1) * LANES)
        w0, w1, w2 = cw_ref[0:1, cols], cw_ref[1:2, cols], cw_ref[2:3, cols]
        bias = cb_ref[:, cols]
        zero_row = jnp.zeros((1, LANES), F32)
        for rb in range(tm // chunk):
            r0 = rb * chunk
            x = xbc_ref[0, r0:r0 + chunk, cols]
            if rb == 0:
                above = jnp.where(i > 0, prev_ref[0, SUBLANES - 1:SUBLANES, cols], zero_row)
            else:
                above = xbc_ref[0, r0 - 1:r0, cols]
            if rb == tm // chunk - 1:
                below = jnp.where(i < n - 1, next_ref[0, 0:1, cols], zero_row)
            else:
                below = xbc_ref[0, r0 + chunk:r0 + chunk + 1, cols]
            xp = jnp.where(brow == 0, above, pltpu.roll(x, 1, 0))
            xn = jnp.where(brow == chunk - 1, below, pltpu.roll(x, chunk - 1, 0))
            y = _silu(xp * w0 + x * w1 + xn * w2 + bias)
            if cb * LANES < SSD_INNER:
                xs_ref[0, r0:r0 + chunk, cols] = y.astype(BF16)
            elif cb * LANES < SSD_INNER + SSD_BC:
                lo = cb * LANES - SSD_INNER
                bm_ref[0, lo:lo + LANES, r0:r0 + chunk] = y.T.astype(BF16)
            else:
                lo = cb * LANES - SSD_INNER - SSD_BC
                cm_ref[0, r0:r0 + chunk, lo:lo + LANES] = y.astype(BF16)

    rows = lax.broadcasted_iota(jnp.int32, (tm, 1), 0)
    t = dt_ref[0] + dtb_ref[...]
    dt = jnp.maximum(t, 0.0) + jnp.log(1.0 + jnp.exp(-jnp.abs(t)))
    la = -dt * jnp.exp(alog_ref[...])
    rmod = rows % chunk
    fwd = la
    rev = la
    sh = 1
    while sh < chunk:
        fwd = fwd + jnp.where(rmod >= sh, pltpu.roll(fwd, sh, 0), 0.0)
        rev = rev + jnp.where(rmod < chunk - sh, pltpu.roll(rev, tm - sh, 0), 0.0)
        sh *= 2
    lane = lax.broadcasted_iota(jnp.int32, (1, LANES), 1)
    acc = jnp.where(lane < SSD_HEADS, fwd, rev)
    acc_t = acc.T
    dt_t = dt.T
    hp = SSD_HEADS // SSD_GROUPS
    for j in range(2 * SSD_GROUPS):
        acol_ref[0, j] = acc[:, j * hp:(j + 1) * hp]
        arow_ref[0, j] = acc_t[j * hp:(j + 1) * hp, :]
        wrow_ref[0, j] = dt_t[j * hp:(j + 1) * hp, :]


def _ssd_prep(xbc, dt, conv_w, conv_b, dtb, alog, tm):
    B, S, _ = xbc.shape
    tm = _tile(S, tm)
    assert tm % CHUNK == 0
    nh = S // SUBLANES
    hp = SSD_HEADS // SSD_GROUPS
    gd = 2 * SSD_GROUPS
    row = lambda n: pl.BlockSpec((1, tm, n), lambda b, i: (b, i, 0))
    const = lambda r, n: pl.BlockSpec((r, n), lambda b, i: (0, 0))
    return pl.pallas_call(
        functools.partial(_ssd_prep_kernel, tm=tm, chunk=CHUNK),
        grid=(B, S // tm),
        in_specs=[row(SSD_CONV_CH),
                  pl.BlockSpec((1, SUBLANES, SSD_CONV_CH),
                               lambda b, i: (b, jnp.maximum(i * (tm // SUBLANES) - 1, 0), 0)),
                  pl.BlockSpec((1, SUBLANES, SSD_CONV_CH),
                               lambda b, i: (b, jnp.minimum((i + 1) * (tm // SUBLANES), nh - 1), 0)),
                  row(LANES),
                  const(3, SSD_CONV_CH), const(1, SSD_CONV_CH), const(1, LANES), const(1, LANES)],
        out_specs=[row(SSD_INNER), pl.BlockSpec((1, SSD_BC, tm), lambda b, i: (b, 0, i)), row(SSD_BC),
                   pl.BlockSpec((1, gd, tm, hp), lambda b, i: (b, 0, i, 0)),
                   pl.BlockSpec((1, gd, hp, tm), lambda b, i: (b, 0, 0, i)),
                   pl.BlockSpec((1, gd, hp, tm), lambda b, i: (b, 0, 0, i))],
        out_shape=[jax.ShapeDtypeStruct((B, S, SSD_INNER), BF16),
                   jax.ShapeDtypeStruct((B, SSD_BC, S), BF16),
                   jax.ShapeDtypeStruct((B, S, SSD_BC), BF16),
                   jax.ShapeDtypeStruct((B, gd, S, hp), F32),
                   jax.ShapeDtypeStruct((B, gd, hp, S), F32),
                   jax.ShapeDtypeStruct((B, gd, hp, S), F32)],
        compiler_params=_params(("arbitrary", "arbitrary")),
        name="ssd_prep",
    )(xbc, xbc, xbc, dt, conv_w, conv_b, dtb, alog)


def _scan_direction(q_ref, k_ref, v_ref, ac_ref, ar_ref, wr_ref, y_ref, h_scr, d, *, G, R, P, N, C):
    reverse = d == 1
    li = lax.broadcasted_iota(jnp.int32, (C, C), 0)
    si = lax.broadcasted_iota(jnp.int32, (C, C), 1)
    mask = (si >= li) if reverse else (li >= si)
    far = 0 if reverse else C - 1
    width = max(P, LANES)
    heads_per = width // P
    lane_head = lax.broadcasted_iota(jnp.int32, (1, width), 1) // P
    RP = R * P
    n_sub = q_ref.shape[1] // C
    per_chunk_tables = ac_ref.shape[2] > C
    for sub, g in [(sub, g) for sub in (range(n_sub - 1, -1, -1) if reverse else range(n_sub)) for g in range(G)]:
        t0 = sub * C
        tb = t0 if per_chunk_tables else 0
        q = q_ref[0, t0:t0 + C, g * N:(g + 1) * N]
        k = k_ref[0, g * N:(g + 1) * N, t0:t0 + C]
        s = _dot(q, k)
        k_t = k.astype(F32)
        ac = ac_ref[0, g, tb:tb + C, :]
        ar = ar_ref[0, g, :, tb:tb + C]
        wr = wr_ref[0, g, :, tb:tb + C]
        for u in range(RP // width):
            lo = u * width
            v = v_ref[0, t0:t0 + C, g * RP + lo:g * RP + lo + width].astype(BF16)
            h_old = h_scr[d, g, :, lo:lo + width]
            y_diag = None
            for j in range(heads_per):
                r = u * heads_per + j
                a_col = ac[:, r:r + 1]
                a_row = ar[r:r + 1, :]
                w_row = wr[r:r + 1, :]
                tot = a_row[:, far:far + 1]
                decay = jnp.exp(jnp.where(mask, a_col - a_row, NEG_BIG))
                m = (s * decay * w_row).astype(BF16)
                kw = (k_t * (jnp.exp(tot - a_row) * w_row)).astype(BF16)
                yd = _dot(m, v)
                st = _dot(kw, v)
                e_col = jnp.exp(a_col)
                e_tot = jnp.exp(tot)
                if y_diag is None:
                    y_diag, state, col_scale, tot_scale = yd, st, e_col, e_tot
                else:
                    sel = lane_head == j
                    y_diag = jnp.where(sel, yd, y_diag)
                    state = jnp.where(sel, st, state)
                    col_scale = jnp.where(sel, e_col, col_scale)
                    tot_scale = jnp.where(sel, e_tot, tot_scale)
            y = y_diag + col_scale * _dot(q, h_old.astype(BF16))
            y_ref[0, t0:t0 + C, g * RP + lo:g * RP + lo + width] = y.astype(y_ref.dtype)
            h_scr[d, g, :, lo:lo + width] = tot_scale * h_old + state


SCAN_CHUNKS_PER_STEP = 4
SCAN_INS = 14
SCAN_OUTS = 4


def _scan_kernel(*refs, families):
    n = len(families)
    ins = [refs[SCAN_INS * i:SCAN_INS * (i + 1)] for i in range(n)]
    outs = [refs[SCAN_INS * n + SCAN_OUTS * i:SCAN_INS * n + SCAN_OUTS * (i + 1)] for i in range(n)]
    scrs = refs[(SCAN_INS + SCAN_OUTS) * n:]
    c = pl.program_id(1)

    @pl.when(c == 0)
    def _():
        for fam_in, h_scr in zip(ins, scrs):
            h_scr[0] = fam_in[12][0]
            h_scr[1] = fam_in[13][0]

    for dims, fam_in, fam_out, h_scr in zip(families, ins, outs, scrs):
        qf, kf, vf, qb, kb, vb, acf, arf, wrf, acb, arb, wrb = fam_in[:12]
        _scan_direction(qf, kf, vf, acf, arf, wrf, fam_out[0], h_scr, 0, **dims)
        _scan_direction(qb, kb, vb, acb, arb, wrb, fam_out[1], h_scr, 1, **dims)

    @pl.when(c == pl.num_programs(1) - 1)
    def _():
        for fam_out, h_scr in zip(outs, scrs):
            fam_out[2][0] = h_scr[0]
            fam_out[3][0] = h_scr[1]


def _bidir_scans(families):
    B, S, _ = families[0][0].shape
    C = _tile(S, CHUNK)
    blk = _tile(S, SCAN_CHUNKS_PER_STEP * C)
    nc = S // blk
    fwd = lambda c: c
    rev = lambda c: nc - 1 - c
    in_specs, out_specs, out_shape, scratch, args, dims_list = [], [], [], [], [], []
    for q, k, v, acol, arow, wrow, h0_f, h0_b, dims in families:
        G, R, P, N = dims["G"], dims["R"], dims["P"], dims["N"]
        RP = R * P
        bx = (lambda b: b) if acol.shape[0] > 1 else (lambda b: 0)
        seq = lambda n, ci: pl.BlockSpec((1, blk, n), lambda b, c, ci=ci: (b, ci(c), 0))
        per_chunk = acol.shape[2] > C
        tc = (lambda ci: ci) if per_chunk else (lambda ci: (lambda c: 0))
        tl = blk if per_chunk else C
        col = lambda d, ci, G=G, R=R, bx=bx, tc=tc, tl=tl: pl.BlockSpec(
            (1, G, tl, R), lambda b, c: (bx(b), d, tc(ci)(c), 0))
        rowt = lambda d, ci, G=G, R=R, bx=bx, tc=tc, tl=tl: pl.BlockSpec(
            (1, G, R, tl), lambda b, c: (bx(b), d, 0, tc(ci)(c)))
        state = pl.BlockSpec((1, G, N, RP), lambda b, c: (b, 0, 0, 0))
        seq_t = lambda n, ci: pl.BlockSpec((1, n, blk), lambda b, c, ci=ci: (b, 0, ci(c)))
        in_specs += [seq(G * N, fwd), seq_t(G * N, fwd), seq(G * RP, fwd),
                     seq(G * N, rev), seq_t(G * N, rev), seq(G * RP, rev),
                     col(0, fwd), rowt(0, fwd), rowt(0, fwd), col(1, rev), rowt(1, rev), rowt(1, rev),
                     state, state]
        out_specs += [seq(G * RP, fwd), seq(G * RP, rev), state, state]
        out_shape += [jax.ShapeDtypeStruct((B, S, G * RP), BF16), jax.ShapeDtypeStruct((B, S, G * RP), BF16),
                      jax.ShapeDtypeStruct((B, G, N, RP), F32), jax.ShapeDtypeStruct((B, G, N, RP), F32)]
        scratch.append(pltpu.VMEM((2, G, N, RP), F32))
        args += [q, k, v, q, k, v, acol, arow, wrow, acol, arow, wrow, h0_f, h0_b]
        dims_list.append(dict(dims, C=C))
    res = pl.pallas_call(
        functools.partial(_scan_kernel, families=tuple(dims_list)),
        grid=(B, nc),
        in_specs=in_specs,
        out_specs=out_specs,
        out_shape=out_shape,
        scratch_shapes=scratch,
        compiler_params=_params(("arbitrary", "arbitrary")),
        name="bidir_scans",
    )(*args)
    return [res[SCAN_OUTS * i:SCAN_OUTS * (i + 1)] for i in range(len(families))]


def _even_out_kernel(ysf_ref, ysb_ref, xs_ref, z_ref, yrf_ref, yrb_ref, rg_ref, x_ref, mod_ref, dvec_ref, sn_ref,
                     w_ref, o_ref):
    f32 = lambda ref: ref[0].astype(F32)
    y = f32(ysf_ref) + f32(ysb_ref) + f32(xs_ref) * dvec_ref[...]
    y = _rms(y * _silu(f32(z_ref))) * sn_ref[...]
    out = _dot(y.astype(BF16), w_ref[0:SSD_INNER, :])
    yr = f32(yrf_ref) + f32(yrb_ref)
    rg = _silu(f32(rg_ref))
    for j in range(RET_HEADS):
        lo = j * RET_V_DIM
        r = _rms(yr[:, lo:lo + RET_V_DIM]) * rg[:, lo:lo + RET_V_DIM]
        out = out + _dot(r.astype(BF16), w_ref[SSD_INNER + lo:SSD_INNER + lo + RET_V_DIM, :])
    o_ref[0] = x_ref[0] + mod_ref[0][2:3] * out


def _even_out(ysf, ysb, xs, z, yrf, yrb, rg, x, mod, dvec, ssd_norm, w, tm):
    B, S, D = x.shape
    tm = _tile(S, tm)
    bm = mod.shape[0]
    mod_map = (lambda b, i: (b, 0, 0)) if bm > 1 else (lambda b, i: (0, 0, 0))
    row = lambda n: pl.BlockSpec((1, tm, n), lambda b, i: (b, i, 0))
    const = lambda r, n: pl.BlockSpec((r, n), lambda b, i: (0, 0))
    return pl.pallas_call(
        _even_out_kernel,
        grid=(B, S // tm),
        in_specs=[row(SSD_INNER), row(SSD_INNER), row(SSD_INNER), row(SSD_INNER),
                  row(RET_INNER), row(RET_INNER), row(RET_INNER), row(D),
                  pl.BlockSpec((1, 6, D), mod_map),
                  const(1, SSD_INNER), const(1, SSD_INNER), const(SSD_INNER + RET_INNER, D)],
        out_specs=row(D),
        out_shape=jax.ShapeDtypeStruct((B, S, D), F32),
        compiler_params=_params(("arbitrary", "arbitrary")),
        name="even_out",
    )(ysf, ysb, xs, z, yrf, yrb, rg, x, mod, dvec, ssd_norm, w)


def _ffn_kernel(x_ref, mod_ref, gain_ref, wg_ref, wu_ref, wd_ref, o_ref, h_scr, acc_scr):
    f = pl.program_id(2)
    mod = mod_ref[0]

    @pl.when(f == 0)
    def _():
        h_scr[...] = _modulate(x_ref[0], gain_ref[...], mod[3:4], mod[4:5]).astype(BF16)
        acc_scr[...] = jnp.zeros_like(acc_scr)

    h = h_scr[...]
    a = (_silu(_dot(h, wg_ref[...])) * _dot(h, wu_ref[...])).astype(BF16)
    acc_scr[...] += _dot(a, wd_ref[...])

    @pl.when(f == pl.num_programs(2) - 1)
    def _():
        o_ref[0] = x_ref[0] + mod[5:6] * acc_scr[...]


def _ffn(x, mod, gain, wg, wu, wd, tm, tf):
    B, S, D = x.shape
    F = wg.shape[1]
    tm = _tile(S, tm)
    tf = _tile(F, tf)
    bm = mod.shape[0]
    mod_map = (lambda b, i, f: (b, 0, 0)) if bm > 1 else (lambda b, i, f: (0, 0, 0))
    resident = dict(pipeline_mode=pl.Buffered(1)) if tf == F else {}
    return pl.pallas_call(
        _ffn_kernel,
        grid=(B, S // tm, F // tf),
        in_specs=[pl.BlockSpec((1, tm, D), lambda b, i, f: (b, i, 0)),
                  pl.BlockSpec((1, 6, D), mod_map),
                  pl.BlockSpec((1, D), lambda b, i, f: (0, 0)),
                  pl.BlockSpec((D, tf), lambda b, i, f: (0, f), **resident),
                  pl.BlockSpec((D, tf), lambda b, i, f: (0, f), **resident),
                  pl.BlockSpec((tf, D), lambda b, i, f: (f, 0), **resident)],
        out_specs=pl.BlockSpec((1, tm, D), lambda b, i, f: (b, i, 0)),
        out_shape=jax.ShapeDtypeStruct((B, S, D), F32),
        scratch_shapes=[pltpu.VMEM((tm, D), BF16), pltpu.VMEM((tm, D), F32)],
        compiler_params=_params(("arbitrary", "arbitrary", "arbitrary")),
        name="ffn_dense",
    )(x, mod, gain, wg, wu, wd)


def _odd_in_kernel(x_ref, mod_ref, gain_ref, w_ref, qn_ref, kn_ref, bd_ref, cos_ref, sin_ref,
                   q_ref, k_ref, v_ref):
    mod = mod_ref[0]
    h = _modulate(x_ref[0], gain_ref[...], mod[0:1], mod[1:2]).astype(BF16)
    cos = cos_ref[...]
    sin = sin_ref[...]
    bd = bd_ref[...]
    lane = lax.broadcasted_iota(jnp.int32, (1, LANES), 1)
    quarter = DIFF_HEAD_DIM // 4
    first = (lane % (2 * quarter)) < quarter
    for base, n_ref, o_ref, scale in ((0, qn_ref, q_ref, DIFF_HEAD_DIM ** -0.5),
                                      (DIFF_INNER, kn_ref, k_ref, 1.0)):
        for j in range(DIFF_INNER // MXU_N):
            lo = j * MXU_N
            a = _dot(h, w_ref[:, base + lo:base + lo + MXU_N])
            sq = a * a
            hi = sq.astype(BF16)
            rest = (sq - hi.astype(F32)).astype(BF16)
            ms = (_dot(hi, bd) + _dot(rest, bd)) * (1.0 / DIFF_HEAD_DIM)
            a = a * lax.rsqrt(ms + NORM_EPS)
            for c in range(MXU_N // LANES):
                b = a[:, c * LANES:(c + 1) * LANES] * n_ref[...]
                partner = jnp.where(first, pltpu.roll(b, LANES - quarter, 1), pltpu.roll(b, quarter, 1))
                o_ref[0, :, lo + c * LANES:lo + (c + 1) * LANES] = ((b * cos + partner * sin) * scale).astype(BF16)
    v = _dot(h, w_ref[:, 2 * DIFF_INNER:3 * DIFF_INNER]).astype(BF16)
    ones_col = jnp.where(lane == 0, 1.0, 0.0).astype(BF16) * jnp.ones((v.shape[0], 1), BF16)
    for j in range(DIFF_HEADS):
        v_ref[0, :, j * MXU_N:j * MXU_N + LANES] = v[:, j * LANES:(j + 1) * LANES]
        v_ref[0, :, j * MXU_N + LANES:(j + 1) * MXU_N] = ones_col


def _odd_in(x, mod, gain, w, qn, kn, bd, cosf, sinf, tm):
    B, S, D = x.shape
    tm = _tile(S, tm)
    bm = mod.shape[0]
    mod_map = (lambda b, i: (b, 0, 0)) if bm > 1 else (lambda b, i: (0, 0, 0))
    row = lambda n: pl.BlockSpec((1, tm, n), lambda b, i: (b, i, 0))
    const = lambda r, n: pl.BlockSpec((r, n), lambda b, i: (0, 0))
    return pl.pallas_call(
        _odd_in_kernel,
        grid=(B, S // tm),
        in_specs=[row(D), pl.BlockSpec((1, 6, D), mod_map), const(1, D), const(D, 3 * DIFF_INNER),
                  const(1, LANES), const(1, LANES), const(MXU_N, MXU_N),
                  pl.BlockSpec((tm, LANES), lambda b, i: (i, 0)),
                  pl.BlockSpec((tm, LANES), lambda b, i: (i, 0))],
        out_specs=[row(DIFF_INNER), row(DIFF_INNER), row(DIFF_HEADS * MXU_N)],
        out_shape=[jax.ShapeDtypeStruct((B, S, DIFF_INNER), BF16), jax.ShapeDtypeStruct((B, S, DIFF_INNER), BF16),
                   jax.ShapeDtypeStruct((B, S, DIFF_HEADS * MXU_N), BF16)],
        compiler_params=_params(("arbitrary", "arbitrary")),
        name="odd_in",
    )(x, mod, gain, w, qn, kn, bd, cosf, sinf)


def _attn_kernel(sc_ref, q_ref, kl_ref, vl_ref, kc_ref, vc_ref, sub_ref, o_ref,
                 q2_scr, m_scr, l_scr, acc_scr, *, tq, tk, unroll, out_scale):
    q = q_ref[0]
    lane = lax.broadcasted_iota(jnp.int32, (1, LANES), 1)
    lo_half = lane < DIFF_HEAD_DIM
    zero = jnp.zeros_like(q)
    q2_scr[0:tq, :] = jnp.where(lo_half, q, zero)
    q2_scr[tq:2 * tq, :] = jnp.where(lo_half, zero, q)
    l_scr[...] = jnp.zeros_like(l_scr)
    acc_scr[...] = jnp.zeros_like(acc_scr)
    n_lat = kl_ref.shape[1] // tk
    use_shift = sc_ref[0, 2] > 0.0

    def over_keys(update, unroll):
        def body(j, carry):
            start = pl.multiple_of(j * tk, tk)
            update(kl_ref[0, pl.ds(start, tk), :], vl_ref[0, pl.ds(start, tk), :])
            return carry

        lax.fori_loop(0, n_lat, body, 0, unroll=unroll)
        update(kc_ref[0], vc_ref[0])

    @pl.when(use_shift)
    def _():
        shift = sc_ref[0, 1]

        def update(k, v):
            p = jnp.exp(_dot_nt(q2_scr[...], k) - shift)
            acc_scr[...] += _dot(p.astype(BF16), v)

        over_keys(update, unroll)
        acc_scr[:, 0:LANES] = acc_scr[:, 0:LANES] / acc_scr[:, LANES:LANES + 1]

    @pl.when(jnp.logical_not(use_shift))
    def _():
        m_scr[...] = jnp.full_like(m_scr, -jnp.inf)

        def update(k, v):
            s = _dot_nt(q2_scr[...], k)
            m_prev = m_scr[...]
            m_new = jnp.maximum(m_prev, jnp.max(s, axis=1, keepdims=True))
            alpha = jnp.exp(m_prev - m_new)
            p = jnp.exp(s - m_new[:, 0:1])
            l_scr[...] = alpha * l_scr[...] + jnp.sum(p, axis=1, keepdims=True)
            acc_scr[...] = alpha[:, 0:1] * acc_scr[...] + _dot(p.astype(BF16), v)
            m_scr[...] = m_new

        over_keys(update, 1)
        acc_scr[:, 0:LANES] = acc_scr[:, 0:LANES] / l_scr[...]

    o = acc_scr[0:tq, 0:LANES] - sc_ref[0, 0] * acc_scr[tq:2 * tq, 0:LANES]
    o_ref[0] = (_rms(o) * sub_ref[...] * out_scale).astype(BF16)


def _attention(lam, q, k_lat, v_lat, k_ctx, v_ctx, subln, out_scale, tq, tk, unroll):
    B, L, _ = q.shape
    Lc = k_ctx.shape[1]
    tq = _tile(L, tq)
    tk = _tile(L, tk)
    unroll = min(unroll, L // tk)
    head = lambda n, w: pl.BlockSpec((1, n, w), lambda b, h, i: (b, 0, h))
    return pl.pallas_call(
        functools.partial(_attn_kernel, tq=tq, tk=tk, unroll=unroll, out_scale=out_scale),
        grid=(B, DIFF_HEADS, L // tq),
        in_specs=[pl.BlockSpec(memory_space=pltpu.SMEM),
                  pl.BlockSpec((1, tq, LANES), lambda b, h, i: (b, i, h)),
                  head(L, LANES), head(L, MXU_N), head(Lc, LANES), head(Lc, MXU_N),
                  pl.BlockSpec((1, LANES), lambda b, h, i: (0, 0))],
        out_specs=pl.BlockSpec((1, tq, LANES), lambda b, h, i: (b, i, h)),
        out_shape=jax.ShapeDtypeStruct((B, L, DIFF_INNER), BF16),
        scratch_shapes=[pltpu.VMEM((2 * tq, LANES), BF16), pltpu.VMEM((2 * tq, LANES), F32),
                        pltpu.VMEM((2 * tq, LANES), F32), pltpu.VMEM((2 * tq, MXU_N), F32)],
        compiler_params=_params(("arbitrary", "arbitrary", "arbitrary")),
        name="diff_attn",
    )(lam, q, k_lat, v_lat, k_ctx, v_ctx, subln)


def _odd_out_kernel(o_ref, x_ref, mod_ref, gain_ref, w_ref, wr_ref, xo_ref, h_ref, route_ref, rt_ref):
    mod = mod_ref[0]
    x1 = x_ref[0] + mod[2:3] * _dot(o_ref[0], w_ref[...])
    xo_ref[0] = x1
    h = _modulate(x1, gain_ref[...], mod[3:4], mod[4:5])
    h_ref[...] = h
    wr = wr_ref[...]
    h_hi, h_mid, h_lo = _split3(h)
    parts = _dot(h_lo, wr) + _dot(h_mid, wr) + _dot(h_hi, wr)
    logits = (pltpu.roll(parts, LANES - 2 * N_EXPERTS, 1) + pltpu.roll(parts, LANES - N_EXPERTS, 1)) + parts
    lane = lax.broadcasted_iota(jnp.int32, logits.shape, 1)
    lg = jnp.where(lane < N_EXPERTS, logits, -jnp.inf)
    v1 = jnp.max(lg, axis=1, keepdims=True)
    i1 = jnp.min(jnp.where(lg == v1, lane, LANES), axis=1, keepdims=True)
    lg2 = jnp.where(lane == i1, -jnp.inf, lg)
    v2 = jnp.max(lg2, axis=1, keepdims=True)
    i2 = jnp.min(jnp.where(lg2 == v2, lane, LANES), axis=1, keepdims=True)
    e = jnp.exp(v2 - v1)
    w1 = 1.0 / (1.0 + e)
    w2 = e * w1
    route = jnp.where(lane == 0, i1.astype(F32),
                      jnp.where(lane == 1, i2.astype(F32),
                                jnp.where(lane == 2, w1, jnp.where(lane == 3, w2, 0.0))))
    route_ref[...] = route
    rt_ref[...] = route.T[0:SUBLANES, :]


def _odd_out(o, x, mod, gain, w, w_router, tm):
    B, S, D = x.shape
    tm = _tile(S, tm)
    n = S // tm
    row = lambda n: pl.BlockSpec((1, tm, n), lambda b, i: (b, i, 0))
    flat = lambda w: pl.BlockSpec((tm, w), lambda b, i: (b * n + i, 0))
    const = lambda r, n: pl.BlockSpec((r, n), lambda b, i: (0, 0))
    return pl.pallas_call(
        _odd_out_kernel,
        grid=(B, n),
        in_specs=[row(DIFF_INNER), row(D), pl.BlockSpec((1, 6, D), lambda b, i: (b, 0, 0)),
                  const(1, D), const(DIFF_INNER, D), const(D, LANES)],
        out_specs=[row(D), flat(D), flat(LANES), pl.BlockSpec((SUBLANES, tm), lambda b, i: (0, b * n + i))],
        out_shape=[jax.ShapeDtypeStruct((B, S, D), F32), jax.ShapeDtypeStruct((B * S, D), F32),
                   jax.ShapeDtypeStruct((B * S, LANES), F32), jax.ShapeDtypeStruct((SUBLANES, B * S), F32)],
        compiler_params=_params(("arbitrary", "arbitrary")),
        name="odd_out",
    )(o, x, mod, gain, w, w_router)


def _row_copy(src, src_row, dst, dst_row, sem):
    return pltpu.make_async_copy(src.at[pl.ds(src_row, 1), :], dst.at[pl.ds(dst_row, 1), :], sem)


def _dispatch_kernel(last_tile_ref, slot_ref, h_ref, hs_ref, zero_buf, sem, zero_sem, *, tt, tm):
    @pl.when(pl.program_id(0) == 0)
    def _():
        zero_buf[...] = jnp.zeros_like(zero_buf)
        for e in range(2 * N_EXPERTS):
            start = pl.multiple_of(last_tile_ref[e] * tm, tm)
            clear = pltpu.make_async_copy(zero_buf, hs_ref.at[pl.ds(start, tm), :], zero_sem)
            clear.start()
            clear.wait()

    def issue(r, carry):
        _row_copy(h_ref, r, hs_ref, slot_ref[0, 0, r], sem).start()
        _row_copy(h_ref, r, hs_ref, slot_ref[0, 0, tt + r], sem).start()
        return carry

    lax.fori_loop(0, tt, issue, 0, unroll=DMA_ISSUE_UNROLL)
    for _ in range(2):
        pltpu.make_async_copy(h_ref, hs_ref.at[pl.ds(0, tt), :], sem).wait()


def _dispatch(last_tile, slots, h, n_rows, tt, tm):
    T, D = h.shape
    tt = _tile(T, tt)
    grid_spec = pltpu.PrefetchScalarGridSpec(
        num_scalar_prefetch=1,
        grid=(T // tt,),
        in_specs=[pl.BlockSpec((1, 1, 2 * tt), lambda i, lt: (i, 0, 0), memory_space=pltpu.SMEM),
                  pl.BlockSpec((tt, D), lambda i, lt: (i, 0))],
        out_specs=pl.BlockSpec(memory_space=pl.ANY),
        scratch_shapes=[pltpu.VMEM((tm, D), F32), pltpu.SemaphoreType.DMA(()), pltpu.SemaphoreType.DMA(())])
    return pl.pallas_call(
        functools.partial(_dispatch_kernel, tt=tt, tm=tm),
        grid_spec=grid_spec,
        out_shape=jax.ShapeDtypeStruct((n_rows, D), F32),
        compiler_params=_params(("arbitrary",)),
        name="moe_dispatch",
    )(last_tile, _slot_blocks(slots, tt), h)


def _expert_kernel(te_ref, tv_ref, ts_ref, hs_ref, wg_ref, wu_ref, wd_ref, ys_ref, h_scr, acc_scr):
    del te_ref, ts_ref
    i = pl.program_id(0)
    f = pl.program_id(1)

    @pl.when(tv_ref[i] > 0)
    def _():
        @pl.when(f == 0)
        def _():
            h_scr[...] = hs_ref[...].astype(BF16)
            acc_scr[...] = jnp.zeros_like(acc_scr)

        h = h_scr[...]
        g = _dot(h, wg_ref[0].astype(BF16))
        u = _dot(h, wu_ref[0].astype(BF16))
        acc_scr[...] += _dot((_silu(g) * u).astype(BF16), wd_ref[0].astype(BF16))

        @pl.when(f == pl.num_programs(1) - 1)
        def _():
            ys_ref[...] = acc_scr[...]

    @pl.when((tv_ref[i] == 0) & (f == pl.num_programs(1) - 1))
    def _():
        ys_ref[...] = jnp.zeros_like(ys_ref)


def _experts(tile_expert, tile_valid, tile_src, hs, wg, wu, wd, tm, tf):
    n_rows, D = hs.shape
    F = wg.shape[2]
    tf = _tile(F, tf)
    nf = F // tf
    fi = lambda i, f, tv: jnp.where(tv[i] > 0, f, nf - 1)
    grid_spec = pltpu.PrefetchScalarGridSpec(
        num_scalar_prefetch=3,
        grid=(n_rows // tm, nf),
        in_specs=[pl.BlockSpec((tm, D), lambda i, f, te, tv, ts: (ts[i], 0)),
                  pl.BlockSpec((1, D, tf), lambda i, f, te, tv, ts: (te[i], 0, fi(i, f, tv))),
                  pl.BlockSpec((1, D, tf), lambda i, f, te, tv, ts: (te[i], 0, fi(i, f, tv))),
                  pl.BlockSpec((1, tf, D), lambda i, f, te, tv, ts: (te[i], fi(i, f, tv), 0))],
        out_specs=pl.BlockSpec((tm, D), lambda i, f, te, tv, ts: (i, 0)),
        scratch_shapes=[pltpu.VMEM((tm, D), BF16), pltpu.VMEM((tm, D), F32)])
    return pl.pallas_call(
        _expert_kernel,
        grid_spec=grid_spec,
        out_shape=jax.ShapeDtypeStruct((n_rows, D), F32),
        compiler_params=_params(("arbitrary", "arbitrary")),
        name="moe_experts",
    )(tile_expert, tile_valid, tile_src, hs, wg, wu, wd)


def _combine_kernel(slot_ref, next_slot_ref, ys_ref, x_ref, mod_ref, route_ref, o_ref, buf, sem, *, tt):
    step = pl.program_id(0) * pl.num_programs(1) + pl.program_id(1)
    n_steps = pl.num_programs(0) * pl.num_programs(1)
    parity = step % 2

    def gather(slots, p):
        def issue(r, carry):
            _row_copy(ys_ref, slots[0, 0, r], buf.at[p, 0], r, sem.at[p]).start()
            _row_copy(ys_ref, slots[0, 0, tt + r], buf.at[p, 1], r, sem.at[p]).start()
            return carry

        lax.fori_loop(0, tt, issue, 0, unroll=DMA_ISSUE_UNROLL)

    @pl.when(step == 0)
    def _():
        gather(slot_ref, 0)

    @pl.when(step + 1 < n_steps)
    def _():
        gather(next_slot_ref, 1 - parity)

    for choice in range(2):
        pltpu.make_async_copy(ys_ref.at[pl.ds(0, tt), :], buf.at[parity, choice], sem.at[parity]).wait()
    route = route_ref[...]
    y = route[:, 2:3] * buf[parity, 0] + route[:, 3:4] * buf[parity, 1]
    o_ref[0] = x_ref[0] + mod_ref[0][5:6] * y


def _combine(slots, ys, x, mod, route, tt):
    B, S, D = x.shape
    tt = _tile(S, tt)
    n = S // tt
    blocks = _slot_blocks(slots, tt)
    last = B * n - 1
    return pl.pallas_call(
        functools.partial(_combine_kernel, tt=tt),
        grid=(B, n),
        in_specs=[pl.BlockSpec((1, 1, 2 * tt), lambda b, i: (b * n + i, 0, 0), memory_space=pltpu.SMEM),
                  pl.BlockSpec((1, 1, 2 * tt), lambda b, i: (jnp.minimum(b * n + i + 1, last), 0, 0),
                               memory_space=pltpu.SMEM),
                  pl.BlockSpec(memory_space=pl.ANY),
                  pl.BlockSpec((1, tt, D), lambda b, i: (b, i, 0)),
                  pl.BlockSpec((1, 6, D), lambda b, i: (b, 0, 0)),
                  pl.BlockSpec((tt, LANES), lambda b, i: (b * n + i, 0))],
        out_specs=pl.BlockSpec((1, tt, D), lambda b, i: (b, i, 0)),
        out_shape=jax.ShapeDtypeStruct((B, S, D), F32),
        scratch_shapes=[pltpu.VMEM((2, 2, tt, D), F32), pltpu.SemaphoreType.DMA((2,))],
        compiler_params=_params(("arbitrary", "arbitrary")),
        name="moe_combine",
    )(blocks, blocks, ys, x, mod, route)


def _slot_blocks(slots, tt):
    T = slots.shape[1]
    return slots.reshape(2, T // tt, tt).transpose(1, 0, 2).reshape(T // tt, 1, 2 * tt)


def _routing_tables(route, tm):
    T = route.shape[1]
    experts = route[0:2].astype(jnp.int32).reshape(2 * T)
    onehot = (experts[None, :] == jnp.arange(N_EXPERTS, dtype=jnp.int32)[:, None]).astype(jnp.int32)
    csum = jnp.cumsum(onehot, axis=1)
    rank = jnp.sum((csum - onehot) * onehot, axis=0)
    counts = csum[:, -1]
    tiles = (counts + tm - 1) // tm
    tile_end = jnp.cumsum(tiles)
    offsets = (tile_end - tiles) * tm
    slots = (jnp.sum(onehot * offsets[:, None], axis=0) + rank).reshape(2, T)
    n_tiles = (2 * T) // tm + N_EXPERTS
    ids = jnp.arange(n_tiles, dtype=jnp.int32)
    tile_expert = jnp.minimum(jnp.sum((ids[:, None] >= tile_end[None, :]).astype(jnp.int32), axis=1),
                              N_EXPERTS - 1)
    last_used = jnp.minimum(jnp.sum((tile_end[-1] - 1 >= tile_end).astype(jnp.int32)), N_EXPERTS - 1)
    tile_valid = (ids < tile_end[-1]).astype(jnp.int32)
    tile_expert = jnp.where(tile_valid > 0, tile_expert, last_used)
    tile_src = jnp.minimum(ids, tile_end[-1] - 1)
    last_tile = jnp.maximum(tile_end - 1, 0)
    tail = ids[n_tiles - N_EXPERTS:]
    last_tile = jnp.concatenate([last_tile, jnp.where(tail >= tile_end[-1], tail, last_tile[0])])
    return (slots.astype(jnp.int32), tile_expert.astype(jnp.int32), tile_valid, tile_src.astype(jnp.int32),
            last_tile.astype(jnp.int32), n_tiles)


def _rope_angles(pos, dim):
    inv = ROPE_THETA ** (-jnp.arange(dim // 2, dtype=F32) / (dim // 2))
    return pos.astype(F32)[:, None] * inv[None, :]


def _ret_rope_tables(pos):
    ang = _rope_angles(pos, RET_QK_DIM)
    cos, sin = jnp.cos(ang), jnp.sin(ang)
    return jnp.concatenate([cos, cos], axis=1), jnp.concatenate([-sin, sin], axis=1)


def _axial_rope_tables(L):
    n_rows = L // GRID_W
    row = jnp.broadcast_to(jnp.arange(n_rows)[:, None], (n_rows, GRID_W)).reshape(-1)
    col = jnp.broadcast_to(jnp.arange(GRID_W)[None, :], (n_rows, GRID_W)).reshape(-1)
    ar = _rope_angles(row, DIFF_HEAD_DIM // 2)
    ac = _rope_angles(col, DIFF_HEAD_DIM // 2)
    cos = jnp.concatenate([jnp.cos(ar), jnp.cos(ar), jnp.cos(ac), jnp.cos(ac)], axis=1)
    sin = jnp.concatenate([-jnp.sin(ar), jnp.sin(ar), -jnp.sin(ac), jnp.sin(ac)], axis=1)
    return jnp.tile(cos, (1, LANES // DIFF_HEAD_DIM)), jnp.tile(sin, (1, LANES // DIFF_HEAD_DIM))


def _ret_decay_tables(ret_decay, S):
    C = min(S, CHUNK)
    lam = -jnp.exp(ret_decay.astype(F32))
    pos = jnp.arange(C)
    steps = jnp.stack([pos + 1, C - pos]).astype(F32)
    a = (lam[:, :, None] * steps[:, None, :]).reshape(1, 2 * RET_HEADS, C)
    return a[:, :, :, None], a[:, :, None, :], jnp.ones((1, 2 * RET_HEADS, 1, C), F32)


def kernel(x, c, ctx, c_ctx, even_w_mod, even_b_mod, even_norm1, even_norm2, even_w_in, even_conv_w, even_conv_b, even_dt_bias, even_a_log, even_d, even_ssd_norm, even_ret_decay, even_w_out, even_ffn_gate, even_ffn_up, even_ffn_down, odd_w_mod, odd_b_mod, odd_norm1, odd_norm2, odd_w_in, odd_q_norm, odd_k_norm, odd_lambda, odd_subln, odd_w_out, odd_router, odd_exp_gate, odd_exp_up, odd_exp_down):
    B, L, D = x.shape
    Lc = ctx.shape[1]
    T = B * L
    row = lambda v: v.reshape(1, -1)
    pad_lanes = lambda v: jnp.pad(v.reshape(1, -1), ((0, 0), (0, LANES - v.size)))

    cond = jnp.concatenate([c, c_ctx[None, :], jnp.zeros((SUBLANES - B - 1, D), F32)], axis=0)

    mod = _adaln(cond, even_w_mod[0], even_b_mod[0]).reshape(SUBLANES, 6, D)
    mod_l, mod_c = mod[:B], mod[B:B + 1]
    w_in = even_w_in[0]
    cut = SSD_INNER + SSD_CONV_CH
    w_in = jnp.concatenate([w_in[:, :cut], w_in[:, cut + 2 * SSD_HEADS:], w_in[:, cut:cut + 2 * SSD_HEADS],
                            jnp.zeros((D, LANES - 2 * SSD_HEADS), F32)], axis=1).astype(BF16)
    g1 = row(even_norm1[0])
    cos_c, sin_c = _ret_rope_tables(jnp.arange(Lc))
    cos_l, sin_l = _ret_rope_tables(Lc + jnp.arange(L))
    dtb = pad_lanes(even_dt_bias[0])
    alog = pad_lanes(even_a_log[0])
    conv_b = row(even_conv_b[0])
    ssd_kw = dict(G=SSD_GROUPS, R=SSD_HEADS // SSD_GROUPS, P=SSD_HEAD_DIM, N=SSD_STATE)
    ret_kw = dict(G=RET_HEADS, R=1, P=RET_V_DIM, N=RET_QK_DIM)

    def mix(xin, modv, cosf, sinf, hs, hr, tm):
        z, xbc, dt, rq, rk, rv, rg = _even_in(xin, modv, g1, w_in, cosf, sinf, tm)
        xs, bmat, cmat, acol, arow, wrow = _ssd_prep(xbc, dt, even_conv_w[0], conv_b, dtb, alog, tm)
        racol, rarow, rwrow = _ret_decay_tables(even_ret_decay[0], xin.shape[1])
        (ysf, ysb, hs_f, hs_b), (yrf, yrb, hr_f, hr_b) = _bidir_scans([
            (cmat, bmat, xs, acol, arow, wrow, hs[0], hs[1], ssd_kw),
            (rq, rk, rv, racol, rarow, rwrow, hr[0], hr[1], ret_kw)])
        return (ysf, ysb, xs, z, yrf, yrb, rg), (hs_f, hs_b), (hr_f, hr_b)

    zs = jnp.zeros((B, SSD_GROUPS, SSD_STATE, SSD_INNER // SSD_GROUPS), F32)
    zr = jnp.zeros((B, RET_HEADS, RET_QK_DIM, RET_V_DIM), F32)
    tm_c = 256
    tm_l = 512
    mixed_c, hs, hr = mix(ctx, mod_c, cos_c, sin_c, (zs, zs), (zr, zr), tm_c)
    mixed_l, _, _ = mix(x, mod_l, cos_l, sin_l, hs, hr, tm_l)

    dvec = row(jnp.repeat(even_d[0], SSD_HEAD_DIM))
    snorm = row(even_ssd_norm[0])
    w_out = even_w_out[0].astype(BF16)
    g2 = row(even_norm2[0])
    wg = even_ffn_gate[0].astype(BF16)
    wu = even_ffn_up[0].astype(BF16)
    wd = even_ffn_down[0].astype(BF16)
    ffn_tf = wg.shape[1]

    xl = _even_out(*mixed_l, x, mod_l, dvec, snorm, w_out, tm_l)
    xl = _ffn(xl, mod_l, g2, wg, wu, wd, tm_l, ffn_tf)
    xc = _even_out(*mixed_c, ctx, mod_c, dvec, snorm, w_out, tm_c)
    xc = _ffn(xc, mod_c, g2, wg, wu, wd, tm_c, ffn_tf)

    mod = _adaln(cond, odd_w_mod[0], odd_b_mod[0]).reshape(SUBLANES, 6, D)
    mod_l, mod_c = mod[:B], mod[B:B + 1]
    lam_init = 0.8 - 0.6 * math.exp(-0.3 * 1)
    lq1, lk1, lq2, lk2 = odd_lambda[0]
    lam = jnp.exp(jnp.sum(lq1 * lk1)) - jnp.exp(jnp.sum(lq2 * lk2)) + lam_init
    bound = 1.02 * DIFF_HEAD_DIM ** 0.5 * jnp.max(jnp.abs(odd_q_norm[0])) * jnp.max(jnp.abs(odd_k_norm[0]))
    lam = jnp.stack([lam, bound - SHIFT_HEADROOM, (bound <= SHIFT_MAX_BOUND).astype(F32),
                     jnp.zeros((), F32)]).reshape(1, 4)
    w_in = odd_w_in[0].astype(BF16)
    g1 = row(odd_norm1[0])
    qn = row(jnp.tile(odd_q_norm[0], LANES // DIFF_HEAD_DIM))
    kn = row(jnp.tile(odd_k_norm[0], LANES // DIFF_HEAD_DIM))
    gid = jnp.arange(MXU_N) // DIFF_HEAD_DIM
    bd = (gid[:, None] == gid[None, :]).astype(BF16)
    cos_l, sin_l = _axial_rope_tables(L)
    cos_c, sin_c = jnp.ones((Lc, LANES), F32), jnp.zeros((Lc, LANES), F32)
    q_l, k_l, v_l = _odd_in(xl, mod_l, g1, w_in, qn, kn, bd, cos_l, sin_l, tm_l)
    _, k_c, v_c = _odd_in(xc, mod_c, g1, w_in, qn, kn, bd, cos_c, sin_c, tm_c)
    o = _attention(lam, q_l, k_l, v_l, k_c, v_c, row(odd_subln[0]), 1.0 - lam_init, 2048, 512, 4)

    w_router = jnp.pad(jnp.concatenate(_split3(odd_router[0]), axis=1), ((0, 0), (0, LANES - 3 * N_EXPERTS)))
    xl, h, route, route_t = _odd_out(o, xl, mod_l, row(odd_norm2[0]), odd_w_out[0].astype(BF16), w_router, tm_l)
    moe_tm = min(1024, T)
    slots, tile_expert, tile_valid, tile_src, last_tile, n_tiles = _routing_tables(route_t, moe_tm)
    hs_sorted = _dispatch(last_tile, slots, h, n_tiles * moe_tm, 512, moe_tm)
    ys_sorted = _experts(tile_expert, tile_valid, tile_src, hs_sorted, odd_exp_gate[0], odd_exp_up[0],
                         odd_exp_down[0], moe_tm, 512)
    return _combine(slots, ys_sorted, xl, mod_l, route, 256)
```
